```python
import math
import jax, jax.numpy as jnp
from jax import lax
import numpy as np

D_MODEL = 2048
BATCH = 2
SEQ = 4096
DEPTH = 2
DEC_BATCH = 128
DEC_SEQ = 8
PAST_LEN = 2048
PAGE_SIZE = 128

HEAD_DIM = 128
H_MOBA = 6
H_FOX = 6
D_MOBA = H_MOBA * HEAD_DIM
D_FOX = H_FOX * HEAD_DIM
D_CONV = D_MODEL - D_MOBA - D_FOX
CONV_WIDTH = 3
MOBA_BLOCK = 256
MOBA_TOPK = 3
MOBA_Q_CHUNK = 64
FOX_Q_BLOCK = 128
N_BUCKETS = 32
MAX_DISTANCE = 128
LN_EPS = 1e-5
FORGET_BIAS_INIT = 3.0
ALPHA = (2 * DEPTH) ** 0.25
BETA = (8 * DEPTH) ** -0.25
BRANCH_COLS = (D_MOBA,) * 4 + (D_FOX,) * 4 + (H_FOX,) + (D_CONV,) * 4
N_IN = sum(BRANCH_COLS)
SPLIT_AT = tuple(int(s) for s in np.cumsum(BRANCH_COLS)[:-1])

kernel_name = "hymba_moba_fox_shortconv_deepnorm_step"


def layer_norm(x, g, b):
    xf = x.astype(jnp.float32)
    mu = jnp.mean(xf, axis=-1, keepdims=True)
    var = jnp.mean(jnp.square(xf - mu), axis=-1, keepdims=True)
    return ((xf - mu) * lax.rsqrt(var + LN_EPS) * g + b).astype(x.dtype)


def t5_bucket(dist):
    max_exact = N_BUCKETS // 2
    d = jnp.maximum(dist, 0)
    large = max_exact + (jnp.log(jnp.maximum(d, 1).astype(jnp.float32) / max_exact)
                         / math.log(MAX_DISTANCE / max_exact) * (N_BUCKETS - max_exact)).astype(jnp.int32)
    large = jnp.minimum(large, N_BUCKETS - 1)
    return jnp.where(d < max_exact, d, large)


def moba_attend(q, qpos, k, v, rel_bias):
    n_q = q.shape[0]
    n_blk = k.shape[0] // MOBA_BLOCK
    kb = k.reshape(n_blk, MOBA_BLOCK, H_MOBA, HEAD_DIM).transpose(2, 0, 1, 3)
    vb = v.reshape(n_blk, MOBA_BLOCK, H_MOBA, HEAD_DIM).transpose(2, 0, 1, 3)
    kmean = jnp.mean(kb.astype(jnp.float32), axis=2)
    gate = jnp.einsum("qhd,hnd->qhn", q.astype(jnp.float32), kmean)
    own = qpos // MOBA_BLOCK
    fully_past = jnp.arange(n_blk, dtype=jnp.int32)[None, None, :] < own[:, None, None]
    gate = jnp.where(fully_past, gate, -jnp.inf)
    n_sel = min(MOBA_TOPK, n_blk)
    top_val, top_idx = lax.top_k(gate, n_sel)
    blk = jnp.concatenate([top_idx.astype(jnp.int32),
                           jnp.broadcast_to(own[:, None, None], (n_q, H_MOBA, 1))], axis=-1)
    blk_ok = jnp.concatenate([jnp.isfinite(top_val), jnp.ones((n_q, H_MOBA, 1), bool)], axis=-1)
    hidx = jnp.arange(H_MOBA)[None, :, None]
    kg = kb[hidx, blk]
    vg = vb[hidx, blk]
    kpos = blk[..., None] * MOBA_BLOCK + jnp.arange(MOBA_BLOCK, dtype=jnp.int32)
    dist = qpos[:, None, None, None] - kpos
    valid = blk_ok[..., None] & (dist >= 0)
    bias = rel_bias[t5_bucket(dist), jnp.arange(H_MOBA)[None, :, None, None]]
    logits = jnp.einsum("qhd,qhsbd->qhsb", q, kg).astype(jnp.float32) * (HEAD_DIM ** -0.5) + bias
    logits = jnp.where(valid, logits, -jnp.inf).reshape(n_q, H_MOBA, -1)
    p = jax.nn.softmax(logits, axis=-1)
    return jnp.einsum("qhn,qhnd->qhd", p.astype(v.dtype), vg.reshape(n_q, H_MOBA, -1, HEAD_DIM))


def moba_prompt(q, k, v, rel_bias):
    b, t = q.shape[:2]
    pad = (-t) % MOBA_BLOCK
    k_p = jnp.pad(k, ((0, 0), (0, pad), (0, 0), (0, 0)))
    v_p = jnp.pad(v, ((0, 0), (0, pad), (0, 0), (0, 0)))
    n_c = t // MOBA_Q_CHUNK
    qc = q.reshape(b * n_c, MOBA_Q_CHUNK, H_MOBA, HEAD_DIM)
    qpos = jnp.tile(jnp.arange(t, dtype=jnp.int32).reshape(n_c, MOBA_Q_CHUNK), (b, 1))
    bidx = jnp.repeat(jnp.arange(b, dtype=jnp.int32), n_c)

    def one(args):
        qq, pp, bi = args
        return moba_attend(qq, pp, k_p[bi], v_p[bi], rel_bias)

    return lax.map(one, (qc, qpos, bidx)).reshape(b, t, H_MOBA, HEAD_DIM)


def moba_sample(q, k_new, v_new, pool_k, pool_v, layer, page_table, rel_bias):
    t_new = q.shape[1]
    past = page_table.shape[1] * pool_k.shape[2]
    pad = (-(past + t_new)) % MOBA_BLOCK
    qpos = past + jnp.arange(t_new, dtype=jnp.int32)

    def one(args):
        qq, kn, vn, pt = args
        zpad = jnp.zeros((pad, H_MOBA, HEAD_DIM), kn.dtype)
        k_all = jnp.concatenate([pool_k[layer, pt].reshape(past, H_MOBA, HEAD_DIM).astype(kn.dtype), kn, zpad], axis=0)
        v_all = jnp.concatenate([pool_v[layer, pt].reshape(past, H_MOBA, HEAD_DIM).astype(vn.dtype), vn, zpad], axis=0)
        return moba_attend(qq, qpos, k_all, v_all, rel_bias)

    return lax.map(one, (q, k_new, v_new, page_table))


def fox_attend(q, qpos, k, v, dq, dk):
    logits = (jnp.einsum("qhd,khd->hqk", q, k).astype(jnp.float32) * (HEAD_DIM ** -0.5)
              + dq.T[:, :, None] - dk.T[:, None, :])
    causal = jnp.arange(k.shape[0], dtype=jnp.int32)[None, :] <= qpos[:, None]
    p = jax.nn.softmax(jnp.where(causal[None], logits, -jnp.inf), axis=-1)
    return jnp.einsum("hqk,khd->qhd", p.astype(v.dtype), v)


def fox_prompt(q, k, v, logf):
    b, t = q.shape[:2]
    dcum = jnp.cumsum(logf.astype(jnp.float32), axis=1)
    n_qb = t // FOX_Q_BLOCK
    qb = q.reshape(b, n_qb, FOX_Q_BLOCK, H_FOX, HEAD_DIM).swapaxes(0, 1)
    db = dcum.reshape(b, n_qb, FOX_Q_BLOCK, H_FOX).swapaxes(0, 1)
    qpos = jnp.arange(t, dtype=jnp.int32).reshape(n_qb, FOX_Q_BLOCK)
    attend = jax.vmap(fox_attend, in_axes=(0, None, 0, 0, 0, 0))
    out = lax.map(lambda a: attend(a[0], a[2], k, v, a[1], dcum), (qb, db, qpos))
    return out.swapaxes(0, 1).reshape(b, t, H_FOX, HEAD_DIM)


def fox_sample(q, k_new, v_new, logf_new, pool_k, pool_v, pool_lf, layer, page_table):
    t_new = q.shape[1]
    past = page_table.shape[1] * pool_k.shape[2]
    qpos = past + jnp.arange(t_new, dtype=jnp.int32)

    def one(args):
        qq, kn, vn, ln, pt = args
        k_all = jnp.concatenate([pool_k[layer, pt].reshape(past, H_FOX, HEAD_DIM).astype(kn.dtype), kn], axis=0)
        v_all = jnp.concatenate([pool_v[layer, pt].reshape(past, H_FOX, HEAD_DIM).astype(vn.dtype), vn], axis=0)
        lf_all = jnp.concatenate([pool_lf[layer, pt].reshape(past, H_FOX).astype(jnp.float32),
                                  ln.astype(jnp.float32)], axis=0)
        dcum = jnp.cumsum(lf_all, axis=0)
        return fox_attend(qq, qpos, k_all, v_all, dcum[past:], dcum)

    return lax.map(one, (q, k_new, v_new, logf_new, page_table))


def causal_dwconv(u_hist, u, w):
    ext = jnp.concatenate([u_hist.astype(u.dtype), u], axis=1)
    y = lax.conv_general_dilated(ext, w[:, None, :].astype(u.dtype), window_strides=(1,), padding="VALID",
                                 dimension_numbers=("NWC", "WIO", "NWC"), feature_group_count=D_CONV)
    return y, ext[:, -(CONV_WIDTH - 1):]


def branch_inputs(x, w_in, b_f):
    n, t, _ = x.shape
    z = jnp.einsum("ntd,de->nte", x, w_in)
    (q_a, k_a, v_a, g_a, q_f, k_f, v_f, g_f, f_logit, b_c, c_c, h_c, g_c) = jnp.split(z, SPLIT_AT, axis=-1)
    heads = lambda a: a.reshape(n, t, -1, HEAD_DIM)
    logf = jax.nn.log_sigmoid((f_logit + b_f).astype(jnp.float32))
    return (heads(q_a), heads(k_a), heads(v_a), g_a,
            heads(q_f), heads(k_f), heads(v_f), g_f, logf, b_c, c_c * h_c, g_c)


def combine(x, o_a, g_a, o_f, g_f, o_c, g_c, w_out, ln_g, ln_b):
    n, t, _ = x.shape
    o = jnp.concatenate([o_a.reshape(n, t, D_MOBA) * jax.nn.silu(g_a),
                         o_f.reshape(n, t, D_FOX) * jax.nn.silu(g_f),
                         o_c * jax.nn.silu(g_c)], axis=-1)
    return layer_norm(ALPHA * x + jnp.einsum("nte,ed->ntd", o, w_out), ln_g, ln_b)


def setup_inputs(seed: int = 0) -> dict:
    key = jax.random.key(seed)
    ks = jax.random.split(key, 16)
    f32 = jnp.float32
    n_pages = PAST_LEN // PAGE_SIZE
    n_used = DEC_BATCH * n_pages
    n_phys = n_used + max(1, n_used // 4)
    kv_shape = (DEPTH, n_phys, PAGE_SIZE, H_MOBA, HEAD_DIM)
    fkv_shape = (DEPTH, n_phys, PAGE_SIZE, H_FOX, HEAD_DIM)
    return {
        "x_prompt": jax.random.normal(ks[0], (BATCH, SEQ, D_MODEL), f32),
        "x_sample": jax.random.normal(ks[1], (DEC_BATCH, DEC_SEQ, D_MODEL), f32),
        "cache_moba_k": jax.random.normal(ks[2], kv_shape, f32),
        "cache_moba_v": jax.random.normal(ks[3], kv_shape, f32),
        "cache_fox_k": jax.random.normal(ks[4], fkv_shape, f32),
        "cache_fox_v": jax.random.normal(ks[5], fkv_shape, f32),
        "cache_fox_logf": jax.nn.log_sigmoid(FORGET_BIAS_INIT + jax.random.normal(ks[6], (DEPTH, n_phys, PAGE_SIZE, H_FOX), f32)),
        "state_conv": jax.random.normal(ks[7], (DEPTH, DEC_BATCH, CONV_WIDTH - 1, D_CONV), f32),
        "page_table": jax.random.permutation(ks[8], n_phys)[:n_used].reshape(DEC_BATCH, n_pages).astype(jnp.int32),
        "rel_bias": 0.5 * jax.random.normal(ks[9], (N_BUCKETS, H_MOBA), f32),
        "w_in": jax.random.normal(ks[10], (DEPTH, D_MODEL, N_IN), f32) * (D_MODEL ** -0.5),
        "b_f": FORGET_BIAS_INIT + 0.5 * jax.random.normal(ks[11], (DEPTH, H_FOX), f32),
        "conv_w": jax.random.normal(ks[12], (DEPTH, CONV_WIDTH, D_CONV), f32) * (CONV_WIDTH ** -0.5),
        "w_out": jax.random.normal(ks[13], (DEPTH, D_MODEL, D_MODEL), f32) * (D_MODEL ** -0.5 * BETA),
        "ln_g": 1.0 + 0.02 * jax.random.normal(ks[14], (DEPTH, D_MODEL), f32),
        "ln_b": 0.02 * jax.random.normal(ks[15], (DEPTH, D_MODEL), f32),
    }


def reference(x_prompt, x_sample, cache_moba_k, cache_moba_v, cache_fox_k, cache_fox_v, cache_fox_logf,
              state_conv, page_table, rel_bias, w_in, b_f, conv_w, w_out, ln_g, ln_b):
    hp, hs = x_prompt, x_sample
    p_ka, p_va, p_kf, p_vf, p_lf, p_cv = [], [], [], [], [], []
    s_ka, s_va, s_kf, s_vf, s_lf, s_cv = [], [], [], [], [], []
    for layer in range(DEPTH):
        (q_a, k_a, v_a, g_a, q_f, k_f, v_f, g_f, logf, b_c, u, g_c) = branch_inputs(hp, w_in[layer], b_f[layer])
        o_a = moba_prompt(q_a, k_a, v_a, rel_bias)
        o_f = fox_prompt(q_f, k_f, v_f, logf)
        zero_hist = jnp.zeros((hp.shape[0], CONV_WIDTH - 1, D_CONV), u.dtype)
        conv_y, conv_st = causal_dwconv(zero_hist, u, conv_w[layer])
        hp_next = combine(hp, o_a, g_a, o_f, g_f, b_c * conv_y, g_c, w_out[layer], ln_g[layer], ln_b[layer])
        p_ka.append(k_a); p_va.append(v_a); p_kf.append(k_f); p_vf.append(v_f); p_lf.append(logf); p_cv.append(conv_st)

        (q_a, k_a, v_a, g_a, q_f, k_f, v_f, g_f, logf, b_c, u, g_c) = branch_inputs(hs, w_in[layer], b_f[layer])
        o_a = moba_sample(q_a, k_a, v_a, cache_moba_k, cache_moba_v, layer, page_table, rel_bias)
        o_f = fox_sample(q_f, k_f, v_f, logf, cache_fox_k, cache_fox_v, cache_fox_logf, layer, page_table)
        conv_y, conv_st = causal_dwconv(state_conv[layer], u, conv_w[layer])
        hs_next = combine(hs, o_a, g_a, o_f, g_f, b_c * conv_y, g_c, w_out[layer], ln_g[layer], ln_b[layer])
        s_ka.append(k_a); s_va.append(v_a); s_kf.append(k_f); s_vf.append(v_f); s_lf.append(logf); s_cv.append(conv_st)

        hp, hs = hp_next, hs_next
    return (hp, hs,
            jnp.stack(p_ka), jnp.stack(p_va), jnp.stack(p_kf), jnp.stack(p_vf), jnp.stack(p_lf), jnp.stack(p_cv),
            jnp.stack(s_ka), jnp.stack(s_va), jnp.stack(s_kf), jnp.stack(s_vf), jnp.stack(s_lf), jnp.stack(s_cv))
```

```python
import functools
import math

import jax
import jax.numpy as jnp
from jax import lax
from jax.experimental import pallas as pl
from jax.experimental.pallas import tpu as pltpu

F32 = jnp.float32
BF16 = jnp.bfloat16
HIGHEST = lax.Precision.HIGHEST

D_MODEL = 2048
HEAD_DIM = 128
N_HEADS = 6
D_ATT = N_HEADS * HEAD_DIM
D_CONV = 512
CONV_WIDTH = 3
MOBA_BLOCK = 256
MOBA_TOPK = 3
N_BUCKETS = 32
MAX_DISTANCE = 128
LN_EPS = 1e-5
SCALE = HEAD_DIM ** -0.5

SEG = 768
SEG_QA, SEG_KA, SEG_VA, SEG_GA, SEG_QF, SEG_KF, SEG_VF, SEG_GF = range(8)
N_HEAD_SEG = 8
N_CONV_SEG = 3
N_MAIN = N_HEAD_SEG * SEG
CBLK = 256
LANE = 128
SUBLANE = 8
F_SEG, F_LANEBLK = (4 * D_CONV) // SEG, ((4 * D_CONV) % SEG) // LANE
NEG = -1e30
VMEM_LIMIT = 56 * 1024 * 1024

NT_DIMS = (((1,), (1,)), ((), ()))
TN_DIMS = (((0,), (0,)), ((), ()))


def _cparams(sem):
    return pltpu.CompilerParams(dimension_semantics=sem, vmem_limit_bytes=VMEM_LIMIT)


def _conv_piece(idx):
    out = []
    for half in range(2):
        col = idx * D_CONV + half * CBLK
        out.append((col // SEG, (col % SEG) // CBLK))
    return out


def _inproj_heads_kernel(x_ref, w_ref, o_ref, *, sb, sl):
    res = lax.dot_general(x_ref[...], w_ref[...], NT_DIMS, preferred_element_type=F32)
    for h in range(N_HEADS):
        o_ref[0, :, h] = res[:, h * HEAD_DIM:(h + 1) * HEAD_DIM].reshape(sb, sl, HEAD_DIM)


def inproj_heads(xb, wt, n_seq, seq_len, tm):
    t = xb.shape[0]
    assert t == n_seq * seq_len and t % tm == 0 and wt.shape == (N_MAIN, D_MODEL)
    if seq_len >= tm:
        assert seq_len % tm == 0
        sb, sl, per_seq = 1, tm, seq_len // tm
        out_map = lambda j, i: (j, i // per_seq, 0, i % per_seq, 0)
    else:
        assert tm % seq_len == 0
        sb, sl = tm // seq_len, seq_len
        out_map = lambda j, i: (j, i, 0, 0, 0)
    return pl.pallas_call(
        functools.partial(_inproj_heads_kernel, sb=sb, sl=sl),
        out_shape=jax.ShapeDtypeStruct((N_HEAD_SEG, n_seq, N_HEADS, seq_len, HEAD_DIM), F32),
        grid=(N_HEAD_SEG, t // tm),
        in_specs=[pl.BlockSpec((tm, D_MODEL), lambda j, i: (i, 0)),
                  pl.BlockSpec((SEG, D_MODEL), lambda j, i: (j, 0))],
        out_specs=pl.BlockSpec((1, sb, N_HEADS, sl, HEAD_DIM), out_map),
        compiler_params=_cparams(("parallel", "parallel")),
        name="inproj_heads",
    )(xb, wt)


def _inproj_conv_kernel(x_ref, w_ref, o_ref):
    o_ref[0] = lax.dot_general(x_ref[...], w_ref[...], NT_DIMS, preferred_element_type=F32)


def inproj_conv(xb, wt, tm):
    t = xb.shape[0]
    assert t % tm == 0 and wt.shape == (N_CONV_SEG * SEG, D_MODEL)
    return pl.pallas_call(
        _inproj_conv_kernel,
        out_shape=jax.ShapeDtypeStruct((N_CONV_SEG, t, SEG), F32),
        grid=(N_CONV_SEG, t // tm),
        in_specs=[pl.BlockSpec((tm, D_MODEL), lambda j, i: (i, 0)),
                  pl.BlockSpec((SEG, D_MODEL), lambda j, i: (j, 0))],
        out_specs=pl.BlockSpec((1, tm, SEG), lambda j, i: (j, i, 0)),
        compiler_params=_cparams(("parallel", "parallel")),
        name="inproj_conv",
    )(xb, wt)


def _t5_bucket(d):
    max_exact = N_BUCKETS // 2
    df = jnp.maximum(d, 1).astype(F32)
    large = max_exact + (jnp.log(df / max_exact) / math.log(MAX_DISTANCE / max_exact)
                         * (N_BUCKETS - max_exact)).astype(jnp.int32)
    large = jnp.minimum(large, N_BUCKETS - 1)
    return jnp.where(d < max_exact, d, large)


def _prompt_bias_kernel(rb_ref, o_ref):
    h = pl.program_id(0)
    s = lax.broadcasted_iota(jnp.int32, (MOBA_BLOCK, MOBA_BLOCK), 0)
    t = lax.broadcasted_iota(jnp.int32, (MOBA_BLOCK, MOBA_BLOCK), 1)
    for k in range(2):
        dist = MOBA_BLOCK * k + t - s
        bucket = _t5_bucket(jnp.maximum(dist, 0))
        val = jnp.zeros((MOBA_BLOCK, MOBA_BLOCK), F32)
        for b in range(N_BUCKETS):
            val = jnp.where(bucket == b, rb_ref[b, h], val)
        o_ref[0, k] = jnp.where(dist >= 0, val, NEG)


def prompt_bias_tables(rel_bias):
    return pl.pallas_call(
        _prompt_bias_kernel,
        out_shape=jax.ShapeDtypeStruct((N_HEADS, 2, MOBA_BLOCK, MOBA_BLOCK), F32),
        grid=(N_HEADS,),
        in_specs=[pl.BlockSpec(memory_space=pltpu.SMEM)],
        out_specs=pl.BlockSpec((1, 2, MOBA_BLOCK, MOBA_BLOCK), lambda h: (h, 0, 0, 0)),
        compiler_params=_cparams(("parallel",)),
        name="prompt_bias_tables",
    )(rel_bias)


def _sample_bias_kernel(rb_ref, o_ref, *, t_new, past):
    rows = N_HEADS * t_new
    r = lax.broadcasted_iota(jnp.int32, (rows, LANE), 0)
    lane = lax.broadcasted_iota(jnp.int32, (rows, LANE), 1)
    t = r % t_new
    hh = r // t_new

    def lookup(dist):
        bucket = _t5_bucket(jnp.maximum(dist, 0))
        val = jnp.zeros((rows, LANE), F32)
        for b in range(N_BUCKETS):
            rb = jnp.zeros((rows, LANE), F32)
            for h in range(N_HEADS):
                rb = jnp.where(hh == h, rb_ref[b, h], rb)
            val = jnp.where(bucket == b, rb, val)
        return val

    o_ref[0] = lookup(LANE + t - lane)
    dist_new = t - lane
    o_ref[1] = jnp.where(dist_new >= 0, lookup(dist_new), NEG)
    o_ref[2] = lookup(jnp.full((rows, LANE), past, jnp.int32))


def sample_bias_tables(rel_bias, t_new, past):
    rows = N_HEADS * t_new
    return pl.pallas_call(
        functools.partial(_sample_bias_kernel, t_new=t_new, past=past),
        out_shape=jax.ShapeDtypeStruct((3, rows, LANE), F32),
        in_specs=[pl.BlockSpec(memory_space=pltpu.SMEM)],
        out_specs=pl.BlockSpec(memory_space=pltpu.VMEM),
        name="sample_bias_tables",
    )(rel_bias)


def _kmean_kernel(k_ref, o_ref, *, n_blk):
    for h in range(N_HEADS):
        k = k_ref[0, 0, h].reshape(n_blk, MOBA_BLOCK, HEAD_DIM)
        o_ref[0, h] = jnp.sum(k, axis=1) * (1.0 / MOBA_BLOCK)


def moba_kmean(zh, n_batch, seq):
    n_blk = seq // MOBA_BLOCK
    return pl.pallas_call(
        functools.partial(_kmean_kernel, n_blk=n_blk),
        out_shape=jax.ShapeDtypeStruct((n_batch, N_HEADS, n_blk, HEAD_DIM), F32),
        grid=(n_batch,),
        in_specs=[pl.BlockSpec((1, 1, N_HEADS, seq, HEAD_DIM), lambda b: (SEG_KA, b, 0, 0, 0))],
        out_specs=pl.BlockSpec((1, N_HEADS, n_blk, HEAD_DIM), lambda b: (b, 0, 0, 0)),
        compiler_params=_cparams(("parallel",)),
        name="moba_kmean",
    )(zh)


def _log_sigmoid(x):
    return -(jnp.maximum(-x, 0.0) + jnp.log1p(jnp.exp(-jnp.abs(x))))


def _fox_prep_kernel(f_ref, bf_ref, lf_ref, dkb_ref, carry_ref, *, blk):
    i = pl.program_id(1)

    @pl.when(i == 0)
    def _():
        carry_ref[...] = jnp.zeros_like(carry_ref)

    lane = lax.broadcasted_iota(jnp.int32, (blk, LANE), 1)
    lf = jnp.where(lane < N_HEADS, _log_sigmoid(f_ref[0] + bf_ref[...]), 0.0)
    lf_ref[...] = lf
    row = lax.broadcasted_iota(jnp.int32, (blk, blk), 0)
    col = lax.broadcasted_iota(jnp.int32, (blk, blk), 1)
    tril = (row >= col).astype(F32)
    sel_row = lax.broadcasted_iota(jnp.int32, (LANE, LANE), 0)
    for h in range(N_HEADS):
        pick = (sel_row == h).astype(F32)
        xb = jnp.dot(lf, pick, precision=HIGHEST, preferred_element_type=F32)
        cum = jnp.dot(tril, xb, precision=HIGHEST, preferred_element_type=F32) + carry_ref[h:h + 1, :]
        dkb_ref[0, h] = cum
        carry_ref[h:h + 1, :] = cum[blk - 1:blk, :]


def fox_prep(zc, bf_row, n_batch, seq, blk=256):
    n_i = seq // blk
    return pl.pallas_call(
        functools.partial(_fox_prep_kernel, blk=blk),
        out_shape=(jax.ShapeDtypeStruct((n_batch * seq, LANE), F32),
                   jax.ShapeDtypeStruct((n_batch, N_HEADS, seq, LANE), F32)),
        grid=(n_batch, n_i),
        in_specs=[pl.BlockSpec((1, blk, LANE), lambda b, i: (F_SEG, b * n_i + i, F_LANEBLK)),
                  pl.BlockSpec((1, LANE), lambda b, i: (0, 0))],
        out_specs=(pl.BlockSpec((blk, LANE), lambda b, i: (b * n_i + i, 0)),
                   pl.BlockSpec((1, N_HEADS, blk, LANE), lambda b, i: (b, 0, i, 0))),
        scratch_shapes=[pltpu.VMEM((SUBLANE, LANE), F32)],
        compiler_params=_cparams(("parallel", "arbitrary")),
        name="fox_prep",
    )(zc, bf_row)


def _attn_tile(qb, k_ref, v_ref, n, bias_fn, carry, tk):
    m, l, acc = carry
    start = pl.multiple_of(n * tk, tk)
    kt = k_ref[0, 0, 0, pl.ds(start, tk), :].astype(BF16)
    s = lax.dot_general(kt, qb, NT_DIMS, preferred_element_type=F32)
    s = bias_fn(s, n, start)
    m_new = jnp.maximum(m, jnp.max(s, axis=0, keepdims=True))
    alpha = jnp.exp(m - m_new)
    p = jnp.exp(s - m_new)
    l = alpha * l + jnp.sum(p, axis=0, keepdims=True)
    vt = v_ref[0, 0, 0, pl.ds(start, tk), :].astype(BF16)
    pv = lax.dot_general(vt, p.astype(BF16), TN_DIMS, preferred_element_type=F32)
    return m_new, l, alpha * acc + pv


def _attn_init(tq):
    return (jnp.full((1, tq), NEG, F32), jnp.zeros((1, tq), F32), jnp.zeros((HEAD_DIM, tq), F32))


def _fox_prompt_kernel(q_ref, k_ref, v_ref, dkb_ref, o_ref, *, tq):
    j = pl.program_id(2)
    qb = (q_ref[0, 0, 0] * SCALE).astype(BF16)
    qstart = pl.multiple_of(j * tq, tq)
    dq = dkb_ref[0, 0, pl.ds(qstart, tq), :].T[0:1, :]
    key = lax.broadcasted_iota(jnp.int32, (tq, tq), 0)
    qry = lax.broadcasted_iota(jnp.int32, (tq, tq), 1)

    def bias(diag):
        def fn(s, n, start):
            dk = dkb_ref[0, 0, pl.ds(start, tq), :]
            s = s + dq - jnp.concatenate([dk] * (tq // LANE), axis=1)
            if diag:
                s = jnp.where(key <= qry, s, NEG)
            return s
        return fn

    carry = lax.fori_loop(0, j, lambda n, c: _attn_tile(qb, k_ref, v_ref, n, bias(False), c, tq), _attn_init(tq))
    m, l, acc = _attn_tile(qb, k_ref, v_ref, j, bias(True), carry, tq)
    o_ref[0, 0] = (acc / l).T


def _head_spec(seg, rows, row_map):
    return pl.BlockSpec((1, 1, 1, rows, HEAD_DIM), lambda b, h, j: (seg, b, h, row_map(j), 0))


def fox_prompt(zh, dkb, n_batch, seq, tq=256):
    nq = seq // tq
    return pl.pallas_call(
        functools.partial(_fox_prompt_kernel, tq=tq),
        out_shape=jax.ShapeDtypeStruct((n_batch, N_HEADS, seq, HEAD_DIM), F32),
        grid=(n_batch, N_HEADS, nq),
        in_specs=[_head_spec(SEG_QF, tq, lambda j: j),
                  _head_spec(SEG_KF, seq, lambda j: 0),
                  _head_spec(SEG_VF, seq, lambda j: 0),
                  pl.BlockSpec((1, 1, seq, LANE), lambda b, h, j: (b, h, 0, 0))],
        out_specs=pl.BlockSpec((1, 1, tq, HEAD_DIM), lambda b, h, j: (b, h, j, 0)),
        compiler_params=_cparams(("parallel", "parallel", "parallel")),
        name="fox_prompt",
    )(zh, zh, zh, dkb)


def _top_blocks(gate, n_valid, axis):
    n_blk = gate.shape[axis]
    blk = lax.broadcasted_iota(jnp.int32, gate.shape, axis)
    blk_f = blk.astype(F32)
    g = jnp.where(blk < n_valid, gate, -jnp.inf)
    sel = jnp.zeros(gate.shape, jnp.bool_)
    for _ in range(MOBA_TOPK):
        mx = jnp.max(g, axis=axis, keepdims=True)
        cand = (g == mx) & (mx > -jnp.inf)
        first = jnp.min(jnp.where(cand, blk_f, float(n_blk)), axis=axis, keepdims=True)
        pick = blk_f == first
        sel = sel | pick
        g = jnp.where(pick, -jnp.inf, g)
    return sel


def _moba_prompt_kernel(rb_ref, q_ref, k_ref, v_ref, km_ref, tab_ref, o_ref, mb_ref, *, tq):
    h = pl.program_id(1)
    j = pl.program_id(2)
    far = rb_ref[N_BUCKETS - 1, h]
    q = q_ref[0, 0, 0]
    qb = (q * SCALE).astype(BF16)
    gate = lax.dot_general(km_ref[0, 0], q, NT_DIMS, precision=HIGHEST, preferred_element_type=F32)
    sel = _top_blocks(gate, j, 0)
    mb_ref[...] = jnp.where(sel, far, NEG)

    def far_bias(s, n, start):
        return s + mb_ref[pl.ds(n, 1), :]

    carry = lax.fori_loop(0, jnp.maximum(j - 1, 0),
                          lambda n, c: _attn_tile(qb, k_ref, v_ref, n, far_bias, c, tq), _attn_init(tq))

    def prev_tile(c):
        def fn(s, n, start):
            return s + tab_ref[0, 1] + (mb_ref[pl.ds(n, 1), :] - far)
        return _attn_tile(qb, k_ref, v_ref, j - 1, fn, c, tq)

    carry = lax.cond(j > 0, prev_tile, lambda c: c, carry)
    m, l, acc = _attn_tile(qb, k_ref, v_ref, j, lambda s, n, start: s + tab_ref[0, 0], carry, tq)
    o_ref[0, 0] = (acc / l).T


def moba_prompt(zh, kmean, tab, rel_bias, n_batch, seq):
    tq = MOBA_BLOCK
    nq = seq // tq
    return pl.pallas_call(
        functools.partial(_moba_prompt_kernel, tq=tq),
        out_shape=jax.ShapeDtypeStruct((n_batch, N_HEADS, seq, HEAD_DIM), F32),
        grid=(n_batch, N_HEADS, nq),
        in_specs=[pl.BlockSpec(memory_space=pltpu.SMEM),
                  _head_spec(SEG_QA, tq, lambda j: j),
                  _head_spec(SEG_KA, seq, lambda j: 0),
                  _head_spec(SEG_VA, seq, lambda j: 0),
                  pl.BlockSpec((1, 1, nq, HEAD_DIM), lambda b, h, j: (b, h, 0, 0)),
                  pl.BlockSpec((1, 2, tq, tq), lambda b, h, j: (h, 0, 0, 0))],
        out_specs=pl.BlockSpec((1, 1, tq, HEAD_DIM), lambda b, h, j: (b, h, j, 0)),
        scratch_shapes=[pltpu.VMEM((nq, tq), F32)],
        compiler_params=_cparams(("parallel", "parallel", "arbitrary")),
        name="moba_prompt",
    )(rel_bias, zh, zh, zh, kmean, tab)


def _pad_rows(x, rows):
    return jnp.concatenate([x, jnp.zeros((rows - x.shape[0], x.shape[1]), x.dtype)], axis=0)


def _softmax_pv(s_list, v_list):
    m = s_list[0]
    for s in s_list[1:]:
        m = jnp.maximum(m, s)
    m = jnp.max(m, axis=1, keepdims=True)
    acc = None
    l = None
    for s, v in zip(s_list, v_list):
        p = jnp.exp(s - m)
        l = p if l is None else l + p
        pv = jnp.dot(p.astype(BF16), v, preferred_element_type=F32)
        acc = pv if acc is None else acc + pv
    return acc / jnp.sum(l, axis=1, keepdims=True)


def _fox_sample_kernel(pt_ref, q_ref, k_ref, v_ref, f_ref, bf_ref, *rest, t_new, n_pages, page):
    k_pages = rest[:n_pages]
    v_pages = rest[n_pages:2 * n_pages]
    lf_pages = rest[2 * n_pages:3 * n_pages]
    o_ref, lf_ref = rest[3 * n_pages:]

    n_r = n_pages * SUBLANE
    lane = lax.broadcasted_iota(jnp.int32, (n_r, page), 1)
    srow = lax.broadcasted_iota(jnp.int32, (n_r, page), 0)
    lfc = jnp.concatenate([r[0, 0] for r in lf_pages], axis=0)
    incl = lfc
    step = 1
    while step < page:
        incl = incl + jnp.where(lane + step < page, pltpu.roll(incl, page - step, axis=1), 0.0)
        step *= 2
    tot = jnp.broadcast_to(jnp.sum(jnp.where(lane == 0, incl, 0.0), axis=1, keepdims=True), (n_r, page))
    run = tot
    step = SUBLANE
    while step < n_r:
        run = run + jnp.where(srow + step < n_r, pltpu.roll(run, n_r - step, axis=0), 0.0)
        step *= 2
    suf = (incl - lfc) + (run - tot)

    flane = lax.broadcasted_iota(jnp.int32, (t_new, LANE), 1)
    lf_new = jnp.where(flane < N_HEADS, _log_sigmoid(f_ref[0] + bf_ref[...]), 0.0)
    lf_ref[...] = lf_new
    cum = _pad_rows(lf_new, LANE).T
    lane2 = lax.broadcasted_iota(jnp.int32, (LANE, LANE), 1)
    step = 1
    while step < t_new:
        cum = cum + jnp.where(lane2 >= step, pltpu.roll(cum, step, axis=1), 0.0)
        step *= 2
    r = lax.broadcasted_iota(jnp.int32, (t_new, page), 0)
    c = lax.broadcasted_iota(jnp.int32, (t_new, page), 1)

    for h in range(N_HEADS):
        qh = (q_ref[0, 0, h] * SCALE).astype(BF16)
        s_list, v_list = [], []
        for p in range(n_pages):
            s = lax.dot_general(qh, k_pages[p][0, 0, h].astype(BF16), NT_DIMS, preferred_element_type=F32)
            s_list.append(s + suf[p * SUBLANE + h:p * SUBLANE + h + 1, :])
            v_list.append(v_pages[p][0, 0, h].astype(BF16))
        kn = _pad_rows(k_ref[0, 0, h], page).astype(BF16)
        s_new = lax.dot_general(qh, kn, NT_DIMS, preferred_element_type=F32)
        s_list.append(jnp.where(c <= r, s_new - cum[h:h + 1, :], NEG))
        v_list.append(_pad_rows(v_ref[0, 0, h], page).astype(BF16))
        o_ref[0, h] = _softmax_pv(s_list, v_list)


def _page_specs(n_pages, layer, shape):
    def spec(p):
        return pl.BlockSpec((1, 1) + shape, lambda b, pt: (layer, pt[b, p]) + (0,) * len(shape))
    return [spec(p) for p in range(n_pages)]


def _new_spec(seg, t_new):
    return pl.BlockSpec((1, 1, N_HEADS, t_new, HEAD_DIM), lambda b, pt: (seg, b, 0, 0, 0))


def fox_sample(zh, zc, bf_row, pool_k, pool_v, pool_lft, page_table, layer, t_new):
    n_seq, n_pages = page_table.shape
    page = pool_k.shape[3]
    in_specs = ([_new_spec(SEG_QF, t_new), _new_spec(SEG_KF, t_new), _new_spec(SEG_VF, t_new),
                 pl.BlockSpec((1, t_new, LANE), lambda b, pt: (F_SEG, b, F_LANEBLK)),
                 pl.BlockSpec((1, LANE), lambda b, pt: (0, 0))]
                + _page_specs(n_pages, layer, (N_HEADS, page, HEAD_DIM))
                + _page_specs(n_pages, layer, (N_HEADS, page, HEAD_DIM))
                + _page_specs(n_pages, layer, (SUBLANE, page)))
    return pl.pallas_call(
        functools.partial(_fox_sample_kernel, t_new=t_new, n_pages=n_pages, page=page),
        out_shape=(jax.ShapeDtypeStruct((n_seq, N_HEADS, t_new, HEAD_DIM), F32),
                   jax.ShapeDtypeStruct((n_seq * t_new, LANE), F32)),
        grid_spec=pltpu.PrefetchScalarGridSpec(
            num_scalar_prefetch=1, grid=(n_seq,), in_specs=in_specs,
            out_specs=(pl.BlockSpec((1, N_HEADS, t_new, HEAD_DIM), lambda b, pt: (b, 0, 0, 0)),
                       pl.BlockSpec((t_new, LANE), lambda b, pt: (b, 0)))),
        compiler_params=_cparams(("parallel",)),
        name="fox_sample",
    )(page_table, zh, zh, zh, zc, bf_row, *([pool_k] * n_pages), *([pool_v] * n_pages), *([pool_lft] * n_pages))


def _moba_sample_kernel(pt_ref, q_ref, k_ref, v_ref, tab_ref, *rest, t_new, n_pages, page):
    k_pages = rest[:n_pages]
    v_pages = rest[n_pages:2 * n_pages]
    o_ref = rest[2 * n_pages]
    per_blk = MOBA_BLOCK // page
    n_blk = n_pages // per_blk

    for h in range(N_HEADS):
        q = q_ref[0, 0, h]
        qh = (q * SCALE).astype(BF16)
        k32 = [kp[0, 0, h] for kp in k_pages]
        means = []
        for n in range(n_blk):
            tot = k32[n * per_blk]
            for p in range(n * per_blk + 1, (n + 1) * per_blk):
                tot = tot + k32[p]
            means.append(jnp.sum(tot, axis=0, keepdims=True))
        kmean = _pad_rows(jnp.concatenate(means, axis=0) * (1.0 / MOBA_BLOCK), LANE)
        gate = lax.dot_general(q, kmean, NT_DIMS, precision=HIGHEST, preferred_element_type=F32)
        sel = _top_blocks(gate, n_blk, 1)
        mask = jnp.where(sel, 0.0, NEG)
        rows = slice(h * t_new, (h + 1) * t_new)
        s_list, v_list = [], []
        for p in range(n_pages):
            s = lax.dot_general(qh, k32[p].astype(BF16), NT_DIMS, preferred_element_type=F32)
            bias = tab_ref[0, rows, :] if p == n_pages - 1 else tab_ref[2, rows, :]
            n = p // per_blk
            s_list.append(s + bias + mask[:, n:n + 1])
            v_list.append(v_pages[p][0, 0, h].astype(BF16))
        kn = _pad_rows(k_ref[0, 0, h], page).astype(BF16)
        s_new = lax.dot_general(qh, kn, NT_DIMS, preferred_element_type=F32)
        s_list.append(s_new + tab_ref[1, rows, :])
        v_list.append(_pad_rows(v_ref[0, 0, h], page).astype(BF16))
        o_ref[0, h] = _softmax_pv(s_list, v_list)


def moba_sample(zh, tab, pool_k, pool_v, page_table, layer, t_new):
    n_seq, n_pages = page_table.shape
    page = pool_k.shape[3]
    rows = N_HEADS * t_new
    in_specs = ([_new_spec(SEG_QA, t_new), _new_spec(SEG_KA, t_new), _new_spec(SEG_VA, t_new),
                 pl.BlockSpec((3, rows, LANE), lambda b, pt: (0, 0, 0))]
                + _page_specs(n_pages, layer, (N_HEADS, page, HEAD_DIM))
                + _page_specs(n_pages, layer, (N_HEADS, page, HEAD_DIM)))
    return pl.pallas_call(
        functools.partial(_moba_sample_kernel, t_new=t_new, n_pages=n_pages, page=page),
        out_shape=jax.ShapeDtypeStruct((n_seq, N_HEADS, t_new, HEAD_DIM), F32),
        grid_spec=pltpu.PrefetchScalarGridSpec(
            num_scalar_prefetch=1, grid=(n_seq,), in_specs=in_specs,
            out_specs=pl.BlockSpec((1, N_HEADS, t_new, HEAD_DIM), lambda b, pt: (b, 0, 0, 0))),
        compiler_params=_cparams(("parallel",)),
        name="moba_sample",
    )(page_table, zh, zh, zh, tab, *([pool_k] * n_pages), *([pool_v] * n_pages))


def _silu(g):
    return g / (1.0 + jnp.exp(-g))


def _token_major(v, tm):
    return jnp.concatenate([v[:, h].reshape(tm, HEAD_DIM) for h in range(N_HEADS)], axis=1)


def _combine_kernel(*refs, tm, seq_tiles, seq_rows, alpha):
    (x_ref, oa_ref, of_ref, ga_ref, gf_ref, b0, b1, c0, c1, h0, h1, g0, g1) = refs[:13]
    if seq_rows is None:
        pc0, pc1, ph0, ph1 = refs[13:17]
        rest = refs[17:]
    else:
        hist1_ref, hist2_ref = refs[13:15]
        rest = refs[15:]
    wout_ref, cw_ref, lng_ref, lnb_ref, y_ref, yb_ref, u_ref = rest
    cat = lambda a, b: jnp.concatenate([a[0], b[0]], axis=1)
    u = cat(c0, c1) * cat(h0, h1)
    if seq_rows is None:
        keep = (pl.program_id(0) % seq_tiles != 0).astype(F32)
        uprev = cat(pc0, pc1) * cat(ph0, ph1) * keep
        ext = jnp.concatenate([uprev, u], axis=0)
        u1 = pltpu.roll(ext, 1, axis=0)[SUBLANE:]
        u2 = pltpu.roll(ext, 2, axis=0)[SUBLANE:]
        u_ref[0] = u[tm - SUBLANE:, :]
    else:
        r = lax.broadcasted_iota(jnp.int32, (tm, D_CONV), 0) % seq_rows
        u1 = jnp.where(r >= 1, pltpu.roll(u, 1, axis=0), hist1_ref[...])
        u2 = jnp.where(r >= 2, pltpu.roll(u, 2, axis=0), hist2_ref[...])
        u_ref[...] = u
    conv = cw_ref[0:1, :] * u2 + cw_ref[1:2, :] * u1 + cw_ref[2:3, :] * u
    oc = (cat(b0, b1) * conv * _silu(cat(g0, g1))).astype(BF16)
    oa = (_token_major(oa_ref[...], tm) * _silu(_token_major(ga_ref[0], tm))).astype(BF16)
    of = (_token_major(of_ref[...], tm) * _silu(_token_major(gf_ref[0], tm))).astype(BF16)
    proj = (jnp.dot(oa, wout_ref[0:D_ATT, :], preferred_element_type=F32)
            + jnp.dot(of, wout_ref[D_ATT:2 * D_ATT, :], preferred_element_type=F32)
            + jnp.dot(oc, wout_ref[2 * D_ATT:, :], preferred_element_type=F32))
    res = alpha * x_ref[...] + proj
    mu = jnp.mean(res, axis=-1, keepdims=True)
    cen = res - mu
    var = jnp.mean(cen * cen, axis=-1, keepdims=True)
    y = cen * lax.rsqrt(var + LN_EPS) * lng_ref[...] + lnb_ref[...]
    y_ref[...] = y
    yb_ref[...] = y.astype(BF16)


def combine(x, oa, of, zh, zc, wout, cw, lng, lnb, alpha, tm, hist=None):
    t = x.shape[0]
    n_seq, _, seq_len, _ = oa.shape
    assert t == n_seq * seq_len and t % tm == 0
    if seq_len >= tm:
        assert seq_len % tm == 0
        sb, sl, per_seq = 1, tm, seq_len // tm
        hmap = lambda i: (i // per_seq, 0, i % per_seq, 0)
    else:
        assert tm % seq_len == 0
        sb, sl, per_seq = tm // seq_len, seq_len, 1
        hmap = lambda i: (i, 0, 0, 0)
    hblk = (sb, N_HEADS, sl, HEAD_DIM)
    zblk = lambda sc: pl.BlockSpec((1, tm, CBLK), lambda i: (sc[0], i, sc[1]))
    pieces = [sc for idx in (0, 1, 2, 3) for sc in _conv_piece(idx)]
    in_specs = [pl.BlockSpec((tm, D_MODEL), lambda i: (i, 0)),
                pl.BlockSpec(hblk, hmap),
                pl.BlockSpec(hblk, hmap),
                pl.BlockSpec((1,) + hblk, lambda i: (SEG_GA,) + hmap(i)),
                pl.BlockSpec((1,) + hblk, lambda i: (SEG_GF,) + hmap(i))] + [zblk(sc) for sc in pieces]
    args = [x, oa, of, zh, zh] + [zc] * 8
    if hist is None:
        seq_rows = None
        tb = tm // SUBLANE
        prev = lambda sc: pl.BlockSpec((1, SUBLANE, CBLK), lambda i: (sc[0], jnp.maximum(i * tb - 1, 0), sc[1]))
        in_specs += [prev(sc) for sc in pieces[2:6]]
        args += [zc] * 4
        u_shape = jax.ShapeDtypeStruct((n_seq, SUBLANE, D_CONV), F32)
        u_spec = pl.BlockSpec((1, SUBLANE, D_CONV), lambda i: (i // per_seq, 0, 0))
    else:
        seq_rows = seq_len
        in_specs += [pl.BlockSpec((tm, D_CONV), lambda i: (i, 0))] * 2
        args += list(hist)
        u_shape = jax.ShapeDtypeStruct((t, D_CONV), F32)
        u_spec = pl.BlockSpec((tm, D_CONV), lambda i: (i, 0))
    in_specs += [pl.BlockSpec((D_MODEL, D_MODEL), lambda i: (0, 0)),
                 pl.BlockSpec((SUBLANE, D_CONV), lambda i: (0, 0)),
                 pl.BlockSpec((1, D_MODEL), lambda i: (0, 0)),
                 pl.BlockSpec((1, D_MODEL), lambda i: (0, 0))]
    args += [wout, cw, lng, lnb]
    return pl.pallas_call(
        functools.partial(_combine_kernel, tm=tm, seq_tiles=per_seq, seq_rows=seq_rows, alpha=alpha),
        out_shape=(jax.ShapeDtypeStruct((t, D_MODEL), F32), jax.ShapeDtypeStruct((t, D_MODEL), BF16), u_shape),
        grid=(t // tm,),
        in_specs=in_specs,
        out_specs=(pl.BlockSpec((tm, D_MODEL), lambda i: (i, 0)), pl.BlockSpec((tm, D_MODEL), lambda i: (i, 0)),
                   u_spec),
        compiler_params=_cparams(("arbitrary",)),
        name="combine_prompt" if hist is None else "combine_sample",
    )(*args)


def kernel(x_prompt, x_sample, cache_moba_k, cache_moba_v, cache_fox_k, cache_fox_v, cache_fox_logf,
           state_conv, page_table, rel_bias, w_in, b_f, conv_w, w_out, ln_g, ln_b):
    depth = w_in.shape[0]
    alpha = (2 * depth) ** 0.25
    n_batch, seq, _ = x_prompt.shape
    n_seq, t_new, _ = x_sample.shape
    n_phys, page = cache_moba_k.shape[1:3]
    past = page_table.shape[1] * page
    assert page == LANE and past % MOBA_BLOCK == 0 and t_new == SUBLANE and seq % MOBA_BLOCK == 0
    assert past >= MAX_DISTANCE and w_in.shape[2] == N_MAIN + N_HEADS + 4 * D_CONV

    xp = x_prompt.reshape(n_batch * seq, D_MODEL)
    xs = x_sample.reshape(n_seq * t_new, D_MODEL)
    xpb, xsb = xp.astype(BF16), xs.astype(BF16)
    pool = lambda c: jnp.swapaxes(c, 2, 3)
    pk_a, pv_a, pk_f, pv_f = pool(cache_moba_k), pool(cache_moba_v), pool(cache_fox_k), pool(cache_fox_v)
    pool_lft = jnp.pad(jnp.swapaxes(cache_fox_logf, 2, 3), ((0, 0), (0, 0), (0, SUBLANE - N_HEADS), (0, 0)))
    tab_p = prompt_bias_tables(rel_bias)
    tab_s = sample_bias_tables(rel_bias, t_new, past)
    tm_s = min(256, n_seq * t_new)

    outs = {k: [] for k in ("p_ka", "p_va", "p_kf", "p_vf", "p_lf", "p_cv",
                            "s_ka", "s_va", "s_kf", "s_vf", "s_lf", "s_cv")}
    for layer in range(depth):
        wt = jnp.swapaxes(w_in[layer], 0, 1)
        n_f = N_MAIN + N_HEADS
        wt_h = wt[:N_MAIN].astype(BF16)
        wt_c = jnp.concatenate(
            [wt[n_f:], wt[N_MAIN:n_f],
             jnp.zeros((N_CONV_SEG * SEG - 4 * D_CONV - N_HEADS, D_MODEL), wt.dtype)], axis=0).astype(BF16)
        wo = w_out[layer].astype(BF16)
        bf_row = jnp.pad(b_f[layer], (0, LANE - N_HEADS)).reshape(1, LANE)
        cw = jnp.pad(conv_w[layer], ((0, SUBLANE - CONV_WIDTH), (0, 0)))
        lng, lnb = ln_g[layer].reshape(1, D_MODEL), ln_b[layer].reshape(1, D_MODEL)

        zh = inproj_heads(xpb, wt_h, n_batch, seq, tm=1024)
        zc = inproj_conv(xpb, wt_c, tm=1024)
        oa = moba_prompt(zh, moba_kmean(zh, n_batch, seq), tab_p, rel_bias, n_batch, seq)
        lf, dkb = fox_prep(zc, bf_row, n_batch, seq)
        of = fox_prompt(zh, dkb, n_batch, seq)
        xp, xpb, ulast = combine(xp, oa, of, zh, zc, wo, cw, lng, lnb, alpha, tm=256)
        outs["p_ka"].append(zh[SEG_KA])
        outs["p_va"].append(zh[SEG_VA])
        outs["p_kf"].append(zh[SEG_KF])
        outs["p_vf"].append(zh[SEG_VF])
        outs["p_lf"].append(lf[:, :N_HEADS].reshape(n_batch, seq, N_HEADS))
        outs["p_cv"].append(ulast[:, SUBLANE - (CONV_WIDTH - 1):, :])

        zhs = inproj_heads(xsb, wt_h, n_seq, t_new, tm=n_seq * t_new)
        zcs = inproj_conv(xsb, wt_c, tm=n_seq * t_new)
        oas = moba_sample(zhs, tab_s, pk_a, pv_a, page_table, layer, t_new)
        ofs, lfs = fox_sample(zhs, zcs, bf_row, pk_f, pv_f, pool_lft, page_table, layer, t_new)
        st = state_conv[layer]
        zero = jnp.zeros((n_seq, t_new, D_CONV), F32)
        hist1 = zero.at[:, 0].set(st[:, 1]).reshape(n_seq * t_new, D_CONV)
        hist2 = zero.at[:, 0].set(st[:, 0]).at[:, 1].set(st[:, 1]).reshape(n_seq * t_new, D_CONV)
        xs, xsb, us = combine(xs, oas, ofs, zhs, zcs, wo, cw, lng, lnb, alpha, tm=tm_s, hist=(hist1, hist2))
        outs["s_ka"].append(zhs[SEG_KA])
        outs["s_va"].append(zhs[SEG_VA])
        outs["s_kf"].append(zhs[SEG_KF])
        outs["s_vf"].append(zhs[SEG_VF])
        outs["s_lf"].append(lfs[:, :N_HEADS].reshape(n_seq, t_new, N_HEADS))
        outs["s_cv"].append(us.reshape(n_seq, t_new, D_CONV)[:, t_new - (CONV_WIDTH - 1):, :])

    st = {k: jnp.stack(v) for k, v in outs.items()}
    tok_major = lambda a: jnp.swapaxes(a, 2, 3)
    return (xp.reshape(n_batch, seq, D_MODEL), xs.reshape(n_seq, t_new, D_MODEL),
            tok_major(st["p_ka"]), tok_major(st["p_va"]), tok_major(st["p_kf"]), tok_major(st["p_vf"]),
            st["p_lf"], st["p_cv"],
            tok_major(st["s_ka"]), tok_major(st["s_va"]), tok_major(st["s_kf"]), tok_major(st["s_vf"]),
            st["s_lf"], st["s_cv"])
```

```python
import functools
import math

import jax
import jax.numpy as jnp
from jax import lax
from jax.experimental import pallas as pl
from jax.experimental.pallas import tpu as pltpu

F32 = jnp.float32
BF16 = jnp.bfloat16
HIGHEST = lax.Precision.HIGHEST

D_MODEL = 2048
HEAD_DIM = 128
N_HEADS = 6
D_ATT = N_HEADS * HEAD_DIM
D_CONV = 512
CONV_WIDTH = 3
MOBA_BLOCK = 256
MOBA_TOPK = 3
N_BUCKETS = 32
MAX_DISTANCE = 128
LN_EPS = 1e-5
SCALE = HEAD_DIM ** -0.5
LOG2E = math.log2(math.e)
QSCALE = SCALE * LOG2E

SEG = 768
SEG_QA, SEG_KA, SEG_VA, SEG_GA, SEG_QF, SEG_KF, SEG_VF, SEG_GF = range(8)
N_HEAD_SEG = 8
N_CONV_SEG = 3
N_MAIN = N_HEAD_SEG * SEG
CBLK = 256
LANE = 128
SUBLANE = 8
F_SEG, F_LANEBLK = (4 * D_CONV) // SEG, ((4 * D_CONV) % SEG) // LANE
NEG = -1e30
VMEM_LIMIT = 56 * 1024 * 1024

NT_DIMS = (((1,), (1,)), ((), ()))
TN_DIMS = (((0,), (0,)), ((), ()))


def _cparams(sem):
    return pltpu.CompilerParams(dimension_semantics=sem, vmem_limit_bytes=VMEM_LIMIT)


def _conv_piece(idx):
    out = []
    for half in range(2):
        col = idx * D_CONV + half * CBLK
        out.append((col // SEG, (col % SEG) // CBLK))
    return out


def _inproj_heads_kernel(x_ref, w_ref, o_ref, *maybe_ob_ref, sb, sl):
    res = lax.dot_general(x_ref[...], w_ref[...], NT_DIMS, preferred_element_type=F32)
    for h in range(N_HEADS):
        o_ref[0, :, h] = res[:, h * HEAD_DIM:(h + 1) * HEAD_DIM].reshape(sb, sl, HEAD_DIM)
    if maybe_ob_ref:
        j = pl.program_id(0)
        scaled = (res * jnp.where((j == SEG_QA) | (j == SEG_QF), QSCALE, 1.0)).astype(BF16)
        for h in range(N_HEADS):
            maybe_ob_ref[0][0, :, h] = scaled[:, h * HEAD_DIM:(h + 1) * HEAD_DIM].reshape(sb, sl, HEAD_DIM)


def inproj_heads(xb, wt, n_seq, seq_len, tm, with_bf16):
    t = xb.shape[0]
    assert t == n_seq * seq_len and t % tm == 0 and wt.shape == (N_MAIN, D_MODEL)
    if seq_len >= tm:
        assert seq_len % tm == 0
        sb, sl, per_seq = 1, tm, seq_len // tm
        out_map = lambda j, i: (j, i // per_seq, 0, i % per_seq, 0)
    else:
        assert tm % seq_len == 0
        sb, sl = tm // seq_len, seq_len
        out_map = lambda j, i: (j, i, 0, 0, 0)
    shape = (N_HEAD_SEG, n_seq, N_HEADS, seq_len, HEAD_DIM)
    spec = pl.BlockSpec((1, sb, N_HEADS, sl, HEAD_DIM), out_map)
    return pl.pallas_call(
        functools.partial(_inproj_heads_kernel, sb=sb, sl=sl),
        out_shape=(jax.ShapeDtypeStruct(shape, F32),) + ((jax.ShapeDtypeStruct(shape, BF16),) if with_bf16 else ()),
        grid=(N_HEAD_SEG, t // tm),
        in_specs=[pl.BlockSpec((tm, D_MODEL), lambda j, i: (i, 0)),
                  pl.BlockSpec((SEG, D_MODEL), lambda j, i: (j, 0))],
        out_specs=(spec,) + ((spec,) if with_bf16 else ()),
        compiler_params=_cparams(("parallel", "parallel")),
        name="inproj_heads",
    )(xb, wt)


def _inproj_conv_kernel(x_ref, w_ref, o_ref):
    o_ref[0] = lax.dot_general(x_ref[...], w_ref[...], NT_DIMS, preferred_element_type=F32)


def inproj_conv(xb, wt, tm):
    t = xb.shape[0]
    assert t % tm == 0 and wt.shape == (N_CONV_SEG * SEG, D_MODEL)
    return pl.pallas_call(
        _inproj_conv_kernel,
        out_shape=jax.ShapeDtypeStruct((N_CONV_SEG, t, SEG), F32),
        grid=(N_CONV_SEG, t // tm),
        in_specs=[pl.BlockSpec((tm, D_MODEL), lambda j, i: (i, 0)),
                  pl.BlockSpec((SEG, D_MODEL), lambda j, i: (j, 0))],
        out_specs=pl.BlockSpec((1, tm, SEG), lambda j, i: (j, i, 0)),
        compiler_params=_cparams(("parallel", "parallel")),
        name="inproj_conv",
    )(xb, wt)


def _t5_bucket(d):
    max_exact = N_BUCKETS // 2
    df = jnp.maximum(d, 1).astype(F32)
    large = max_exact + (jnp.log(df / max_exact) / math.log(MAX_DISTANCE / max_exact)
                         * (N_BUCKETS - max_exact)).astype(jnp.int32)
    large = jnp.minimum(large, N_BUCKETS - 1)
    return jnp.where(d < max_exact, d, large)


def _prompt_bias_kernel(rb_ref, o_ref):
    h = pl.program_id(0)
    s = lax.broadcasted_iota(jnp.int32, (MOBA_BLOCK, MOBA_BLOCK), 0)
    t = lax.broadcasted_iota(jnp.int32, (MOBA_BLOCK, MOBA_BLOCK), 1)
    for k in range(2):
        dist = MOBA_BLOCK * k + t - s
        bucket = _t5_bucket(jnp.maximum(dist, 0))
        val = jnp.zeros((MOBA_BLOCK, MOBA_BLOCK), F32)
        for b in range(N_BUCKETS):
            val = jnp.where(bucket == b, rb_ref[b, h], val)
        o_ref[0, k] = jnp.where(dist >= 0, val * LOG2E, NEG)


def prompt_bias_tables(rel_bias):
    return pl.pallas_call(
        _prompt_bias_kernel,
        out_shape=jax.ShapeDtypeStruct((N_HEADS, 2, MOBA_BLOCK, MOBA_BLOCK), F32),
        grid=(N_HEADS,),
        in_specs=[pl.BlockSpec(memory_space=pltpu.SMEM)],
        out_specs=pl.BlockSpec((1, 2, MOBA_BLOCK, MOBA_BLOCK), lambda h: (h, 0, 0, 0)),
        compiler_params=_cparams(("parallel",)),
        name="prompt_bias_tables",
    )(rel_bias)


def _sample_bias_kernel(rb_ref, o_ref, *, t_new, past):
    rows = N_HEADS * t_new
    r = lax.broadcasted_iota(jnp.int32, (rows, LANE), 0)
    lane = lax.broadcasted_iota(jnp.int32, (rows, LANE), 1)
    t = r % t_new
    hh = r // t_new

    def lookup(dist):
        bucket = _t5_bucket(jnp.maximum(dist, 0))
        val = jnp.zeros((rows, LANE), F32)
        for b in range(N_BUCKETS):
            rb = jnp.zeros((rows, LANE), F32)
            for h in range(N_HEADS):
                rb = jnp.where(hh == h, rb_ref[b, h], rb)
            val = jnp.where(bucket == b, rb, val)
        return val

    o_ref[0] = lookup(LANE + t - lane)
    dist_new = t - lane
    o_ref[1] = jnp.where(dist_new >= 0, lookup(dist_new), NEG)
    o_ref[2] = lookup(jnp.full((rows, LANE), past, jnp.int32))


def sample_bias_tables(rel_bias, t_new, past):
    rows = N_HEADS * t_new
    return pl.pallas_call(
        functools.partial(_sample_bias_kernel, t_new=t_new, past=past),
        out_shape=jax.ShapeDtypeStruct((3, rows, LANE), F32),
        in_specs=[pl.BlockSpec(memory_space=pltpu.SMEM)],
        out_specs=pl.BlockSpec(memory_space=pltpu.VMEM),
        name="sample_bias_tables",
    )(rel_bias)


def _kmean_kernel(k_ref, o_ref, *, n_blk):
    for h in range(N_HEADS):
        k = k_ref[0, 0, h].reshape(n_blk, MOBA_BLOCK, HEAD_DIM)
        o_ref[0, h] = jnp.sum(k, axis=1) * (1.0 / MOBA_BLOCK)


def moba_kmean(zh, n_batch, seq):
    n_blk = seq // MOBA_BLOCK
    return pl.pallas_call(
        functools.partial(_kmean_kernel, n_blk=n_blk),
        out_shape=jax.ShapeDtypeStruct((n_batch, N_HEADS, n_blk, HEAD_DIM), F32),
        grid=(n_batch,),
        in_specs=[pl.BlockSpec((1, 1, N_HEADS, seq, HEAD_DIM), lambda b: (SEG_KA, b, 0, 0, 0))],
        out_specs=pl.BlockSpec((1, N_HEADS, n_blk, HEAD_DIM), lambda b: (b, 0, 0, 0)),
        compiler_params=_cparams(("parallel",)),
        name="moba_kmean",
    )(zh)


def _log_sigmoid(x):
    return -(jnp.maximum(-x, 0.0) + jnp.log1p(jnp.exp(-jnp.abs(x))))


def _split3(x):
    hi = x.astype(BF16).astype(F32)
    r1 = x - hi
    mid = r1.astype(BF16).astype(F32)
    lo = (r1 - mid).astype(BF16).astype(F32)
    return hi, mid, lo


def _fox_prep_kernel(f_ref, bf_ref, lf_ref, kx_ref, qx_ref, carry_ref, *, blk):
    i = pl.program_id(1)

    @pl.when(i == 0)
    def _():
        carry_ref[...] = jnp.zeros_like(carry_ref)

    lane = lax.broadcasted_iota(jnp.int32, (blk, LANE), 1)
    lf = jnp.where(lane < N_HEADS, _log_sigmoid(f_ref[0] + bf_ref[...]), 0.0)
    lf_ref[...] = lf
    row = lax.broadcasted_iota(jnp.int32, (blk, blk), 0)
    col = lax.broadcasted_iota(jnp.int32, (blk, blk), 1)
    tril = (row >= col).astype(F32)
    sel_row = lax.broadcasted_iota(jnp.int32, (LANE, LANE), 0)
    for h in range(N_HEADS):
        pick = (sel_row == h).astype(F32)
        xb = jnp.dot(lf, pick, precision=HIGHEST, preferred_element_type=F32)
        cum = jnp.dot(tril, xb, precision=HIGHEST, preferred_element_type=F32) + carry_ref[h:h + 1, :]
        carry_ref[h:h + 1, :] = cum[blk - 1:blk, :]
        hi, mid, lo = _split3(cum * LOG2E)
        kx_ref[0, h] = jnp.where(lane == 0, -hi, jnp.where(lane == 1, -mid, jnp.where(
            lane == 2, -lo, jnp.where(lane < 6, 1.0, 0.0)))).astype(BF16)
        qx_ref[0, h] = jnp.where(lane == 3, hi, jnp.where(lane == 4, mid, jnp.where(
            lane == 5, lo, jnp.where(lane < 3, 1.0, 0.0)))).astype(BF16)


def fox_prep(zc, bf_row, n_batch, seq, blk=256):
    n_i = seq // blk
    xshape = jax.ShapeDtypeStruct((n_batch, N_HEADS, seq, LANE), BF16)
    xspec = pl.BlockSpec((1, N_HEADS, blk, LANE), lambda b, i: (b, 0, i, 0))
    return pl.pallas_call(
        functools.partial(_fox_prep_kernel, blk=blk),
        out_shape=(jax.ShapeDtypeStruct((n_batch * seq, LANE), F32), xshape, xshape),
        grid=(n_batch, n_i),
        in_specs=[pl.BlockSpec((1, blk, LANE), lambda b, i: (F_SEG, b * n_i + i, F_LANEBLK)),
                  pl.BlockSpec((1, LANE), lambda b, i: (0, 0))],
        out_specs=(pl.BlockSpec((blk, LANE), lambda b, i: (b * n_i + i, 0)), xspec, xspec),
        scratch_shapes=[pltpu.VMEM((SUBLANE, LANE), F32)],
        compiler_params=_cparams(("parallel", "arbitrary")),
        name="fox_prep",
    )(zc, bf_row)


HEADS_PER_STEP = 6
HEAD_GROUP = 3


def _attn_tiles(qs, k_ref, kx_ref, v_ref, n, bias_fn, carry, tk):
    start = pl.multiple_of(n * tk, tk)
    hp = len(qs)
    scores, out = {}, [None] * (3 * hp)

    def logits(heads):
        for i in heads:
            kt = k_ref[0, 0, i, pl.ds(start, tk), :]
            if kx_ref is not None:
                kt = jnp.concatenate([kt, kx_ref[0, i, pl.ds(start, tk), :]], axis=1)
            scores[i] = bias_fn(lax.dot_general(kt, qs[i], NT_DIMS, preferred_element_type=F32), i, n)

    def update(heads):
        probs = {}
        for i in heads:
            m, l, _ = carry[3 * i:3 * i + 3]
            m_new = jnp.maximum(m, jnp.max(scores[i], axis=0, keepdims=True))
            alpha = jnp.exp2(m - m_new)
            p = jnp.exp2(scores.pop(i) - m_new)
            out[3 * i:3 * i + 2] = [m_new, alpha * l + jnp.sum(p, axis=0, keepdims=True)]
            probs[i] = (p.astype(BF16), alpha)
        for i in heads:
            p, alpha = probs[i]
            vt = v_ref[0, 0, i, pl.ds(start, tk), :]
            pv = lax.dot_general(vt, p, TN_DIMS, preferred_element_type=F32)
            out[3 * i + 2] = alpha * carry[3 * i + 2] + pv

    groups = [range(i, min(i + HEAD_GROUP, hp)) for i in range(0, hp, HEAD_GROUP)]
    logits(groups[0])
    for gi, heads in enumerate(groups):
        if gi + 1 < len(groups):
            logits(groups[gi + 1])
        update(heads)
    return tuple(out)


def _attn_init(tq, n_heads):
    return (jnp.full((1, tq), NEG, F32), jnp.zeros((1, tq), F32), jnp.zeros((HEAD_DIM, tq), F32)) * n_heads


def _attn_finish(o_ref, carry):
    for i in range(len(carry) // 3):
        m, l, acc = carry[3 * i:3 * i + 3]
        o_ref[0, i] = (acc / l).T


def _fox_prompt_kernel(q_ref, qx_ref, k_ref, kx_ref, v_ref, o_ref, *, tq, hp):
    j = pl.program_id(2)
    qs = [jnp.concatenate([q_ref[0, 0, i], qx_ref[0, i]], axis=1) for i in range(hp)]
    key = lax.broadcasted_iota(jnp.int32, (tq, tq), 0)
    qry = lax.broadcasted_iota(jnp.int32, (tq, tq), 1)
    causal = lambda s, i, n: jnp.where(key <= qry, s, NEG)
    carry = lax.fori_loop(0, j, lambda n, c: _attn_tiles(qs, k_ref, kx_ref, v_ref, n, lambda s, i, n: s, c, tq),
                          _attn_init(tq, hp))
    _attn_finish(o_ref, _attn_tiles(qs, k_ref, kx_ref, v_ref, j, causal, carry, tq))


def _head_spec(seg, hp, rows, row_map):
    return pl.BlockSpec((1, 1, hp, rows, HEAD_DIM), lambda b, g, j: (seg, b, g, row_map(j), 0))


def fox_prompt(zb, kx, qx, n_batch, seq, tq=256, hp=HEADS_PER_STEP):
    nq = seq // tq
    return pl.pallas_call(
        functools.partial(_fox_prompt_kernel, tq=tq, hp=hp),
        out_shape=jax.ShapeDtypeStruct((n_batch, N_HEADS, seq, HEAD_DIM), F32),
        grid=(n_batch, N_HEADS // hp, nq),
        in_specs=[_head_spec(SEG_QF, hp, tq, lambda j: j),
                  pl.BlockSpec((1, hp, tq, LANE), lambda b, g, j: (b, g, j, 0)),
                  _head_spec(SEG_KF, hp, seq, lambda j: 0),
                  pl.BlockSpec((1, hp, seq, LANE), lambda b, g, j: (b, g, 0, 0)),
                  _head_spec(SEG_VF, hp, seq, lambda j: 0)],
        out_specs=pl.BlockSpec((1, hp, tq, HEAD_DIM), lambda b, g, j: (b, g, j, 0)),
        compiler_params=_cparams(("parallel", "parallel", "parallel")),
        name="fox_prompt",
    )(zb, qx, zb, kx, zb)


def _top_blocks(gate, n_valid, axis):
    n_blk = gate.shape[axis]
    blk = lax.broadcasted_iota(jnp.int32, gate.shape, axis)
    blk_f = blk.astype(F32)
    g = jnp.where(blk < n_valid, gate, -jnp.inf)
    sel = jnp.zeros(gate.shape, jnp.bool_)
    for _ in range(MOBA_TOPK):
        mx = jnp.max(g, axis=axis, keepdims=True)
        cand = (g == mx) & (mx > -jnp.inf)
        first = jnp.min(jnp.where(cand, blk_f, float(n_blk)), axis=axis, keepdims=True)
        pick = blk_f == first
        sel = sel | pick
        g = jnp.where(pick, -jnp.inf, g)
    return sel


def _moba_prompt_kernel(rb_ref, q32_ref, q_ref, k_ref, v_ref, km_ref, tab_ref, o_ref, mb_ref, *, tq, hp):
    g = pl.program_id(1)
    j = pl.program_id(2)
    qs = [q_ref[0, 0, i] for i in range(hp)]
    fars = []
    for i in range(hp):
        far = rb_ref[N_BUCKETS - 1, g * hp + i] * LOG2E
        gate = lax.dot_general(km_ref[0, i], q32_ref[0, 0, i], NT_DIMS, precision=HIGHEST,
                               preferred_element_type=F32)
        mb_ref[i] = jnp.where(_top_blocks(gate, j, 0), far, NEG)
        fars.append(far)

    far_bias = lambda s, i, n: s + mb_ref[i, pl.ds(n, 1), :]
    carry = lax.fori_loop(0, jnp.maximum(j - 1, 0),
                          lambda n, c: _attn_tiles(qs, k_ref, None, v_ref, n, far_bias, c, tq), _attn_init(tq, hp))
    prev_bias = lambda s, i, n: s + tab_ref[i, 1] + (mb_ref[i, pl.ds(n, 1), :] - fars[i])
    carry = lax.cond(j > 0, lambda c: _attn_tiles(qs, k_ref, None, v_ref, j - 1, prev_bias, c, tq),
                     lambda c: c, carry)
    own_bias = lambda s, i, n: s + tab_ref[i, 0]
    _attn_finish(o_ref, _attn_tiles(qs, k_ref, None, v_ref, j, own_bias, carry, tq))


def moba_prompt(zh, zb, kmean, tab, rel_bias, n_batch, seq, hp=HEADS_PER_STEP):
    tq = MOBA_BLOCK
    nq = seq // tq
    return pl.pallas_call(
        functools.partial(_moba_prompt_kernel, tq=tq, hp=hp),
        out_shape=jax.ShapeDtypeStruct((n_batch, N_HEADS, seq, HEAD_DIM), F32),
        grid=(n_batch, N_HEADS // hp, nq),
        in_specs=[pl.BlockSpec(memory_space=pltpu.SMEM),
                  _head_spec(SEG_QA, hp, tq, lambda j: j),
                  _head_spec(SEG_QA, hp, tq, lambda j: j),
                  _head_spec(SEG_KA, hp, seq, lambda j: 0),
                  _head_spec(SEG_VA, hp, seq, lambda j: 0),
                  pl.BlockSpec((1, hp, nq, HEAD_DIM), lambda b, g, j: (b, g, 0, 0)),
                  pl.BlockSpec((hp, 2, tq, tq), lambda b, g, j: (g, 0, 0, 0))],
        out_specs=pl.BlockSpec((1, hp, tq, HEAD_DIM), lambda b, g, j: (b, g, j, 0)),
        scratch_shapes=[pltpu.VMEM((hp, nq, tq), F32)],
        compiler_params=_cparams(("parallel", "parallel", "arbitrary")),
        name="moba_prompt",
    )(rel_bias, zh, zb, zb, zb, kmean, tab)


def _pad_rows(x, rows):
    return jnp.concatenate([x, jnp.zeros((rows - x.shape[0], x.shape[1]), x.dtype)], axis=0)


def _lanes(a, b):
    return jnp.concatenate([a, b], axis=1)


def _rows(a, b):
    return jnp.concatenate([a, b], axis=0)


def _pair_queries(q_ref, g, scale):
    q0, q1 = q_ref[0, 0, 2 * g], q_ref[0, 0, 2 * g + 1]
    z = jnp.zeros_like(q0)
    return _rows(_lanes(q0, z), _lanes(z, q1)) * scale


def _pair_tile(pages, p0, g):
    t = lambda p: _lanes(pages[p][0, 0, 2 * g].astype(BF16), pages[p][0, 0, 2 * g + 1].astype(BF16))
    return _rows(t(p0), t(p0 + 1))


def _pair_new(x_ref, g, page):
    return _lanes(_pad_rows(x_ref[0, 0, 2 * g], page), _pad_rows(x_ref[0, 0, 2 * g + 1], page)).astype(BF16)


def _pair_bias(top, bottom, t_new):
    return _rows(jnp.broadcast_to(top, (t_new, top.shape[1])), jnp.broadcast_to(bottom, (t_new, bottom.shape[1])))


def _softmax_pv_pair(o_ref, g, s_list, s_new, v_list, v_new, t_new):
    m = s_list[0]
    for s in s_list[1:]:
        m = jnp.maximum(m, s)
    m = jnp.max(jnp.maximum(jnp.maximum(m[:, :LANE], m[:, LANE:]), s_new), axis=1, keepdims=True)
    acc = None
    l = None
    for s, v in zip(s_list, v_list):
        p = jnp.exp(s - m)
        l = p if l is None else l + p
        pv = jnp.dot(p.astype(BF16), v, preferred_element_type=F32)
        acc = pv if acc is None else acc + pv
    p = jnp.exp(s_new - m)
    l = jnp.sum(l[:, :LANE] + l[:, LANE:] + p, axis=1, keepdims=True)
    out = (acc + jnp.dot(p.astype(BF16), v_new, preferred_element_type=F32)) / l
    o_ref[0, 2 * g] = out[:t_new, :HEAD_DIM]
    o_ref[0, 2 * g + 1] = out[t_new:, HEAD_DIM:]


def _fox_sample_kernel(pt_ref, q_ref, k_ref, v_ref, f_ref, bf_ref, *rest, t_new, n_pages, page):
    k_pages = rest[:n_pages]
    v_pages = rest[n_pages:2 * n_pages]
    lf_pages = rest[2 * n_pages:3 * n_pages]
    o_ref, lf_ref = rest[3 * n_pages:]

    n_r = n_pages * SUBLANE
    lane = lax.broadcasted_iota(jnp.int32, (n_r, page), 1)
    srow = lax.broadcasted_iota(jnp.int32, (n_r, page), 0)
    lfc = jnp.concatenate([r[0, 0] for r in lf_pages], axis=0)
    incl = lfc
    step = 1
    while step < page:
        incl = incl + jnp.where(lane + step < page, pltpu.roll(incl, page - step, axis=1), 0.0)
        step *= 2
    tot = jnp.broadcast_to(jnp.sum(jnp.where(lane == 0, incl, 0.0), axis=1, keepdims=True), (n_r, page))
    run = tot
    step = SUBLANE
    while step < n_r:
        run = run + jnp.where(srow + step < n_r, pltpu.roll(run, n_r - step, axis=0), 0.0)
        step *= 2
    suf = (incl - lfc) + (run - tot)

    flane = lax.broadcasted_iota(jnp.int32, (t_new, LANE), 1)
    lf_new = jnp.where(flane < N_HEADS, _log_sigmoid(f_ref[0] + bf_ref[...]), 0.0)
    lf_ref[...] = lf_new
    cum = _pad_rows(lf_new, LANE).T
    lane2 = lax.broadcasted_iota(jnp.int32, (LANE, LANE), 1)
    step = 1
    while step < t_new:
        cum = cum + jnp.where(lane2 >= step, pltpu.roll(cum, step, axis=1), 0.0)
        step *= 2
    r = lax.broadcasted_iota(jnp.int32, (2 * t_new, page), 0) % t_new
    c = lax.broadcasted_iota(jnp.int32, (2 * t_new, page), 1)
    suf_row = lambda p, h: suf[p * SUBLANE + h:p * SUBLANE + h + 1, :]

    for g in range(N_HEADS // 2):
        h0, h1 = 2 * g, 2 * g + 1
        qp = _pair_queries(q_ref, g, SCALE).astype(BF16)
        s_list, v_list = [], []
        for p0 in range(0, n_pages, 2):
            s = lax.dot_general(qp, _pair_tile(k_pages, p0, g), NT_DIMS, preferred_element_type=F32)
            bias = _pair_bias(_lanes(suf_row(p0, h0), suf_row(p0 + 1, h0)),
                              _lanes(suf_row(p0, h1), suf_row(p0 + 1, h1)), t_new)
            s_list.append(s + bias)
            v_list.append(_pair_tile(v_pages, p0, g))
        s_new = lax.dot_general(qp, _pair_new(k_ref, g, page), NT_DIMS, preferred_element_type=F32)
        s_new = jnp.where(c <= r, s_new - _pair_bias(cum[h0:h0 + 1, :], cum[h1:h1 + 1, :], t_new), NEG)
        _softmax_pv_pair(o_ref, g, s_list, s_new, v_list, _pair_new(v_ref, g, page), t_new)


def _page_specs(n_pages, layer, shape):
    def spec(p):
        return pl.BlockSpec((1, 1) + shape, lambda b, pt: (layer, pt[b, p]) + (0,) * len(shape))
    return [spec(p) for p in range(n_pages)]


def _new_spec(seg, t_new):
    return pl.BlockSpec((1, 1, N_HEADS, t_new, HEAD_DIM), lambda b, pt: (seg, b, 0, 0, 0))


def fox_sample(zh, zc, bf_row, pool_k, pool_v, pool_lft, page_table, layer, t_new):
    n_seq, n_pages = page_table.shape
    page = pool_k.shape[3]
    in_specs = ([_new_spec(SEG_QF, t_new), _new_spec(SEG_KF, t_new), _new_spec(SEG_VF, t_new),
                 pl.BlockSpec((1, t_new, LANE), lambda b, pt: (F_SEG, b, F_LANEBLK)),
                 pl.BlockSpec((1, LANE), lambda b, pt: (0, 0))]
                + _page_specs(n_pages, layer, (N_HEADS, page, HEAD_DIM))
                + _page_specs(n_pages, layer, (N_HEADS, page, HEAD_DIM))
                + _page_specs(n_pages, layer, (SUBLANE, page)))
    return pl.pallas_call(
        functools.partial(_fox_sample_kernel, t_new=t_new, n_pages=n_pages, page=page),
        out_shape=(jax.ShapeDtypeStruct((n_seq, N_HEADS, t_new, HEAD_DIM), F32),
                   jax.ShapeDtypeStruct((n_seq * t_new, LANE), F32)),
        grid_spec=pltpu.PrefetchScalarGridSpec(
            num_scalar_prefetch=1, grid=(n_seq,), in_specs=in_specs,
            out_specs=(pl.BlockSpec((1, N_HEADS, t_new, HEAD_DIM), lambda b, pt: (b, 0, 0, 0)),
                       pl.BlockSpec((t_new, LANE), lambda b, pt: (b, 0)))),
        compiler_params=_cparams(("parallel",)),
        name="fox_sample",
    )(page_table, zh, zh, zh, zc, bf_row, *([pool_k] * n_pages), *([pool_v] * n_pages), *([pool_lft] * n_pages))


def _moba_sample_kernel(pt_ref, q_ref, k_ref, v_ref, tab_ref, *rest, t_new, n_pages, page):
    k_pages = rest[:n_pages]
    v_pages = rest[n_pages:2 * n_pages]
    o_ref = rest[2 * n_pages]
    assert MOBA_BLOCK == 2 * page
    n_blk = n_pages // 2

    def block_means(h):
        sums = [jnp.sum(k_pages[2 * n][0, 0, h] + k_pages[2 * n + 1][0, 0, h], axis=0, keepdims=True)
                for n in range(n_blk)]
        return _pad_rows(jnp.concatenate(sums, axis=0) * (1.0 / MOBA_BLOCK), LANE)

    for g in range(N_HEADS // 2):
        q32 = _pair_queries(q_ref, g, 1.0)
        qp = (q32 * SCALE).astype(BF16)
        kmean = _lanes(block_means(2 * g), block_means(2 * g + 1))
        gate = lax.dot_general(q32, kmean, NT_DIMS, precision=HIGHEST, preferred_element_type=F32)
        sel = _top_blocks(gate, n_blk, 1)
        mask = jnp.where(sel, 0.0, NEG)
        rows = slice(2 * g * t_new, (2 * g + 2) * t_new)
        far = tab_ref[2, rows, :]
        s_list, v_list = [], []
        for n in range(n_blk):
            s = lax.dot_general(qp, _pair_tile(k_pages, 2 * n, g), NT_DIMS, preferred_element_type=F32)
            bias = _lanes(far, tab_ref[0, rows, :] if n == n_blk - 1 else far)
            s_list.append(s + bias + mask[:, n:n + 1])
            v_list.append(_pair_tile(v_pages, 2 * n, g))
        s_new = lax.dot_general(qp, _pair_new(k_ref, g, page), NT_DIMS, preferred_element_type=F32)
        _softmax_pv_pair(o_ref, g, s_list, s_new + tab_ref[1, rows, :], v_list, _pair_new(v_ref, g, page), t_new)


def moba_sample(zh, tab, pool_k, pool_v, page_table, layer, t_new):
    n_seq, n_pages = page_table.shape
    page = pool_k.shape[3]
    rows = N_HEADS * t_new
    in_specs = ([_new_spec(SEG_QA, t_new), _new_spec(SEG_KA, t_new), _new_spec(SEG_VA, t_new),
                 pl.BlockSpec((3, rows, LANE), lambda b, pt: (0, 0, 0))]
                + _page_specs(n_pages, layer, (N_HEADS, page, HEAD_DIM))
                + _page_specs(n_pages, layer, (N_HEADS, page, HEAD_DIM)))
    return pl.pallas_call(
        functools.partial(_moba_sample_kernel, t_new=t_new, n_pages=n_pages, page=page),
        out_shape=jax.ShapeDtypeStruct((n_seq, N_HEADS, t_new, HEAD_DIM), F32),
        grid_spec=pltpu.PrefetchScalarGridSpec(
            num_scalar_prefetch=1, grid=(n_seq,), in_specs=in_specs,
            out_specs=pl.BlockSpec((1, N_HEADS, t_new, HEAD_DIM), lambda b, pt: (b, 0, 0, 0))),
        compiler_params=_cparams(("parallel",)),
        name="moba_sample",
    )(page_table, zh, zh, zh, tab, *([pool_k] * n_pages), *([pool_v] * n_pages))


def _silu(g):
    return g / (1.0 + jnp.exp(-g))


def _token_major(v, tm):
    return jnp.concatenate([v[:, h].reshape(tm, HEAD_DIM) for h in range(N_HEADS)], axis=1)


def _combine_kernel(*refs, tm, seq_tiles, seq_rows, alpha):
    (x_ref, oa_ref, of_ref, ga_ref, gf_ref, b0, b1, c0, c1, h0, h1, g0, g1) = refs[:13]
    if seq_rows is None:
        pc0, pc1, ph0, ph1 = refs[13:17]
        rest = refs[17:]
    else:
        hist1_ref, hist2_ref = refs[13:15]
        rest = refs[15:]
    wout_ref, cw_ref, lng_ref, lnb_ref, y_ref, yb_ref, u_ref = rest
    cat = lambda a, b: jnp.concatenate([a[0], b[0]], axis=1)
    u = cat(c0, c1) * cat(h0, h1)
    if seq_rows is None:
        keep = (pl.program_id(0) % seq_tiles != 0).astype(F32)
        uprev = cat(pc0, pc1) * cat(ph0, ph1) * keep
        ext = jnp.concatenate([uprev, u], axis=0)
        u1 = pltpu.roll(ext, 1, axis=0)[SUBLANE:]
        u2 = pltpu.roll(ext, 2, axis=0)[SUBLANE:]
        u_ref[0] = u[tm - SUBLANE:, :]
    else:
        r = lax.broadcasted_iota(jnp.int32, (tm, D_CONV), 0) % seq_rows
        u1 = jnp.where(r >= 1, pltpu.roll(u, 1, axis=0), hist1_ref[...])
        u2 = jnp.where(r >= 2, pltpu.roll(u, 2, axis=0), hist2_ref[...])
        u_ref[...] = u
    conv = cw_ref[0:1, :] * u2 + cw_ref[1:2, :] * u1 + cw_ref[2:3, :] * u
    oc = (cat(b0, b1) * conv * _silu(cat(g0, g1))).astype(BF16)
    oa = (_token_major(oa_ref[...], tm) * _silu(_token_major(ga_ref[0], tm))).astype(BF16)
    of = (_token_major(of_ref[...], tm) * _silu(_token_major(gf_ref[0], tm))).astype(BF16)
    proj = (jnp.dot(oa, wout_ref[0:D_ATT, :], preferred_element_type=F32)
            + jnp.dot(of, wout_ref[D_ATT:2 * D_ATT, :], preferred_element_type=F32)
            + jnp.dot(oc, wout_ref[2 * D_ATT:, :], preferred_element_type=F32))
    res = alpha * x_ref[...] + proj
    mu = jnp.mean(res, axis=-1, keepdims=True)
    cen = res - mu
    var = jnp.mean(cen * cen, axis=-1, keepdims=True)
    y = cen * lax.rsqrt(var + LN_EPS) * lng_ref[...] + lnb_ref[...]
    y_ref[...] = y
    yb_ref[...] = y.astype(BF16)


def combine(x, oa, of, zh, zc, wout, cw, lng, lnb, alpha, tm, hist=None):
    t = x.shape[0]
    n_seq, _, seq_len, _ = oa.shape
    assert t == n_seq * seq_len and t % tm == 0
    if seq_len >= tm:
        assert seq_len % tm == 0
        sb, sl, per_seq = 1, tm, seq_len // tm
        hmap = lambda i: (i // per_seq, 0, i % per_seq, 0)
    else:
        assert tm % seq_len == 0
        sb, sl, per_seq = tm // seq_len, seq_len, 1
        hmap = lambda i: (i, 0, 0, 0)
    hblk = (sb, N_HEADS, sl, HEAD_DIM)
    zblk = lambda sc: pl.BlockSpec((1, tm, CBLK), lambda i: (sc[0], i, sc[1]))
    pieces = [sc for idx in (0, 1, 2, 3) for sc in _conv_piece(idx)]
    in_specs = [pl.BlockSpec((tm, D_MODEL), lambda i: (i, 0)),
                pl.BlockSpec(hblk, hmap),
                pl.BlockSpec(hblk, hmap),
                pl.BlockSpec((1,) + hblk, lambda i: (SEG_GA,) + hmap(i)),
                pl.BlockSpec((1,) + hblk, lambda i: (SEG_GF,) + hmap(i))] + [zblk(sc) for sc in pieces]
    args = [x, oa, of, zh, zh] + [zc] * 8
    if hist is None:
        seq_rows = None
        tb = tm // SUBLANE
        prev = lambda sc: pl.BlockSpec((1, SUBLANE, CBLK), lambda i: (sc[0], jnp.maximum(i * tb - 1, 0), sc[1]))
        in_specs += [prev(sc) for sc in pieces[2:6]]
        args += [zc] * 4
        u_shape = jax.ShapeDtypeStruct((n_seq, SUBLANE, D_CONV), F32)
        u_spec = pl.BlockSpec((1, SUBLANE, D_CONV), lambda i: (i // per_seq, 0, 0))
    else:
        seq_rows = seq_len
        in_specs += [pl.BlockSpec((tm, D_CONV), lambda i: (i, 0))] * 2
        args += list(hist)
        u_shape = jax.ShapeDtypeStruct((t, D_CONV), F32)
        u_spec = pl.BlockSpec((tm, D_CONV), lambda i: (i, 0))
    in_specs += [pl.BlockSpec((D_MODEL, D_MODEL), lambda i: (0, 0)),
                 pl.BlockSpec((SUBLANE, D_CONV), lambda i: (0, 0)),
                 pl.BlockSpec((1, D_MODEL), lambda i: (0, 0)),
                 pl.BlockSpec((1, D_MODEL), lambda i: (0, 0))]
    args += [wout, cw, lng, lnb]
    return pl.pallas_call(
        functools.partial(_combine_kernel, tm=tm, seq_tiles=per_seq, seq_rows=seq_rows, alpha=alpha),
        out_shape=(jax.ShapeDtypeStruct((t, D_MODEL), F32), jax.ShapeDtypeStruct((t, D_MODEL), BF16), u_shape),
        grid=(t // tm,),
        in_specs=in_specs,
        out_specs=(pl.BlockSpec((tm, D_MODEL), lambda i: (i, 0)), pl.BlockSpec((tm, D_MODEL), lambda i: (i, 0)),
                   u_spec),
        compiler_params=_cparams(("arbitrary",)),
        name="combine_prompt" if hist is None else "combine_sample",
    )(*args)


def kernel(x_prompt, x_sample, cache_moba_k, cache_moba_v, cache_fox_k, cache_fox_v, cache_fox_logf,
           state_conv, page_table, rel_bias, w_in, b_f, conv_w, w_out, ln_g, ln_b):
    depth = w_in.shape[0]
    alpha = (2 * depth) ** 0.25
    n_batch, seq, _ = x_prompt.shape
    n_seq, t_new, _ = x_sample.shape
    n_phys, page = cache_moba_k.shape[1:3]
    past = page_table.shape[1] * page
    assert page == LANE and past % MOBA_BLOCK == 0 and t_new == SUBLANE and seq % MOBA_BLOCK == 0
    assert past >= MAX_DISTANCE and w_in.shape[2] == N_MAIN + N_HEADS + 4 * D_CONV

    xp = x_prompt.reshape(n_batch * seq, D_MODEL)
    xs = x_sample.reshape(n_seq * t_new, D_MODEL)
    xpb, xsb = xp.astype(BF16), xs.astype(BF16)
    pool = lambda c: jnp.swapaxes(c, 2, 3)
    pk_a, pv_a, pk_f, pv_f = pool(cache_moba_k), pool(cache_moba_v), pool(cache_fox_k), pool(cache_fox_v)
    pool_lft = jnp.pad(jnp.swapaxes(cache_fox_logf, 2, 3), ((0, 0), (0, 0), (0, SUBLANE - N_HEADS), (0, 0)))
    tab_p = prompt_bias_tables(rel_bias)
    tab_s = sample_bias_tables(rel_bias, t_new, past)
    tm_s = min(256, n_seq * t_new)

    outs = {k: [] for k in ("p_ka", "p_va", "p_kf", "p_vf", "p_lf", "p_cv",
                            "s_ka", "s_va", "s_kf", "s_vf", "s_lf", "s_cv")}
    for layer in range(depth):
        wt = jnp.swapaxes(w_in[layer], 0, 1)
        n_f = N_MAIN + N_HEADS
        wt_h = wt[:N_MAIN].astype(BF16)
        wt_c = jnp.concatenate(
            [wt[n_f:], wt[N_MAIN:n_f],
             jnp.zeros((N_CONV_SEG * SEG - 4 * D_CONV - N_HEADS, D_MODEL), wt.dtype)], axis=0).astype(BF16)
        wo = w_out[layer].astype(BF16)
        bf_row = jnp.pad(b_f[layer], (0, LANE - N_HEADS)).reshape(1, LANE)
        cw = jnp.pad(conv_w[layer], ((0, SUBLANE - CONV_WIDTH), (0, 0)))
        lng, lnb = ln_g[layer].reshape(1, D_MODEL), ln_b[layer].reshape(1, D_MODEL)

        zh, zb = inproj_heads(xpb, wt_h, n_batch, seq, tm=1024, with_bf16=True)
        zc = inproj_conv(xpb, wt_c, tm=1024)
        oa = moba_prompt(zh, zb, moba_kmean(zh, n_batch, seq), tab_p, rel_bias, n_batch, seq)
        lf, kx, qx = fox_prep(zc, bf_row, n_batch, seq)
        of = fox_prompt(zb, kx, qx, n_batch, seq)
        xp, xpb, ulast = combine(xp, oa, of, zh, zc, wo, cw, lng, lnb, alpha, tm=256)
        outs["p_ka"].append(zh[SEG_KA])
        outs["p_va"].append(zh[SEG_VA])
        outs["p_kf"].append(zh[SEG_KF])
        outs["p_vf"].append(zh[SEG_VF])
        outs["p_lf"].append(lf[:, :N_HEADS].reshape(n_batch, seq, N_HEADS))
        outs["p_cv"].append(ulast[:, SUBLANE - (CONV_WIDTH - 1):, :])

        (zhs,) = inproj_heads(xsb, wt_h, n_seq, t_new, tm=n_seq * t_new, with_bf16=False)
        zcs = inproj_conv(xsb, wt_c, tm=n_seq * t_new)
        oas = moba_sample(zhs, tab_s, pk_a, pv_a, page_table, layer, t_new)
        ofs, lfs = fox_sample(zhs, zcs, bf_row, pk_f, pv_f, pool_lft, page_table, layer, t_new)
        st = state_conv[layer]
        zero = jnp.zeros((n_seq, t_new, D_CONV), F32)
        hist1 = zero.at[:, 0].set(st[:, 1]).reshape(n_seq * t_new, D_CONV)
        hist2 = zero.at[:, 0].set(st[:, 0]).at[:, 1].set(st[:, 1]).reshape(n_seq * t_new, D_CONV)
        xs, xsb, us = combine(xs, oas, ofs, zhs, zcs, wo, cw, lng, lnb, alpha, tm=tm_s, hist=(hist1, hist2))
        outs["s_ka"].append(zhs[SEG_KA])
        outs["s_va"].append(zhs[SEG_VA])
        outs["s_kf"].append(zhs[SEG_KF])
        outs["s_vf"].append(zhs[SEG_VF])
        outs["s_lf"].append(lfs[:, :N_HEADS].reshape(n_seq, t_new, N_HEADS))
        outs["s_cv"].append(us.reshape(n_seq, t_new, D_CONV)[:, t_new - (CONV_WIDTH - 1):, :])

    st = {k: jnp.stack(v) for k, v in outs.items()}
    tok_major = lambda a: jnp.swapaxes(a, 2, 3)
    return (xp.reshape(n_batch, seq, D_MODEL), xs.reshape(n_seq, t_new, D_MODEL),
            tok_major(st["p_ka"]), tok_major(st["p_va"]), tok_major(st["p_kf"]), tok_major(st["p_vf"]),
            st["p_lf"], st["p_cv"],
            tok_major(st["s_ka"]), tok_major(st["s_va"]), tok_major(st["s_kf"]), tok_major(st["s_vf"]),
            st["s_lf"], st["s_cv"])
```

```python
import functools
import math

import jax
import jax.numpy as jnp
from jax import lax
from jax.experimental import pallas as pl
from jax.experimental.pallas import tpu as pltpu

F32 = jnp.float32
BF16 = jnp.bfloat16
HIGHEST = lax.Precision.HIGHEST

D_MODEL = 2048
HEAD_DIM = 128
N_HEADS = 6
D_ATT = N_HEADS * HEAD_DIM
D_CONV = 512
CONV_WIDTH = 3
MOBA_BLOCK = 256
MOBA_TOPK = 3
N_BUCKETS = 32
MAX_DISTANCE = 128
LN_EPS = 1e-5
SCALE = HEAD_DIM ** -0.5
LOG2E = math.log2(math.e)
QSCALE = SCALE * LOG2E

SEG = 768
SEG_QA, SEG_KA, SEG_VA, SEG_GA, SEG_QF, SEG_KF, SEG_VF, SEG_GF = range(8)
N_HEAD_SEG = 8
N_CONV_SEG = 3
N_MAIN = N_HEAD_SEG * SEG
CBLK = 256
LANE = 128
SUBLANE = 8
F_SEG, F_LANEBLK = (4 * D_CONV) // SEG, ((4 * D_CONV) % SEG) // LANE
NEG = -1e30
VMEM_LIMIT = 56 * 1024 * 1024

NT_DIMS = (((1,), (1,)), ((), ()))
TN_DIMS = (((0,), (0,)), ((), ()))


def _cparams(sem):
    return pltpu.CompilerParams(dimension_semantics=sem, vmem_limit_bytes=VMEM_LIMIT)


def _conv_piece(idx):
    out = []
    for half in range(2):
        col = idx * D_CONV + half * CBLK
        out.append((col // SEG, (col % SEG) // CBLK))
    return out


def _inproj_heads_kernel(x_ref, w_ref, *refs, sb, sl, n_in, bf16_scale):
    o_ref = refs[n_in]
    res = lax.dot_general(x_ref[...], w_ref[...], NT_DIMS, preferred_element_type=F32)
    for h in range(N_HEADS):
        o_ref[0, :, h] = res[:, h * HEAD_DIM:(h + 1) * HEAD_DIM].reshape(sb, sl, HEAD_DIM)
    if bf16_scale is not None:
        scaled = (res * bf16_scale).astype(BF16)
        for h in range(N_HEADS):
            refs[n_in + 1][0, :, h] = scaled[:, h * HEAD_DIM:(h + 1) * HEAD_DIM].reshape(sb, sl, HEAD_DIM)


def inproj_heads(xb, wt, segs, n_seq, seq_len, tm, bf16_scale=None, stack=None):
    t = xb.shape[0]
    assert t == n_seq * seq_len and t % tm == 0 and wt.shape == (N_MAIN, D_MODEL)
    if seq_len >= tm:
        assert seq_len % tm == 0
        sb, sl, per_seq = 1, tm, seq_len // tm
        rows = lambda i: (i // per_seq, 0, i % per_seq, 0)
    else:
        assert tm % seq_len == 0
        sb, sl = tm // seq_len, seq_len
        rows = lambda i: (i, 0, 0, 0)
    blk = (1, sb, N_HEADS, sl, HEAD_DIM)
    tail = (n_seq, N_HEADS, seq_len, HEAD_DIM)

    def seg_of(j):
        s = segs[0]
        for idx in range(1, len(segs)):
            s = jnp.where(j == idx, segs[idx], s)
        return s

    in_specs = [pl.BlockSpec((tm, D_MODEL), lambda j, i: (i, 0)),
                pl.BlockSpec((SEG, D_MODEL), lambda j, i: (seg_of(j), 0))]
    args = [xb, wt]
    aliases = {}
    if stack is None:
        shapes = [jax.ShapeDtypeStruct((len(segs),) + tail, F32)]
        specs = [pl.BlockSpec(blk, lambda j, i: (j,) + rows(i))]
    else:
        depth, layer, prev = stack
        assert len(segs) == 1
        shapes = [jax.ShapeDtypeStruct((depth,) + tail, F32)]
        specs = [pl.BlockSpec(blk, lambda j, i: (layer,) + rows(i))]
        if prev is not None:
            in_specs.append(pl.BlockSpec(memory_space=pl.ANY))
            args.append(prev)
            aliases = {2: 0}
    if bf16_scale is not None:
        shapes.append(jax.ShapeDtypeStruct((len(segs),) + tail, BF16))
        specs.append(pl.BlockSpec(blk, lambda j, i: (j,) + rows(i)))
    return pl.pallas_call(
        functools.partial(_inproj_heads_kernel, sb=sb, sl=sl, n_in=len(args) - 2, bf16_scale=bf16_scale),
        out_shape=tuple(shapes),
        grid=(len(segs), t // tm),
        in_specs=in_specs,
        out_specs=tuple(specs),
        input_output_aliases=aliases,
        compiler_params=_cparams(("parallel", "parallel")),
        name="inproj_heads",
    )(*args)


def _inproj_conv_kernel(x_ref, w_ref, o_ref):
    o_ref[0] = lax.dot_general(x_ref[...], w_ref[...], NT_DIMS, preferred_element_type=F32)


def inproj_conv(xb, wt, tm):
    t = xb.shape[0]
    assert t % tm == 0 and wt.shape == (N_CONV_SEG * SEG, D_MODEL)
    return pl.pallas_call(
        _inproj_conv_kernel,
        out_shape=jax.ShapeDtypeStruct((N_CONV_SEG, t, SEG), F32),
        grid=(N_CONV_SEG, t // tm),
        in_specs=[pl.BlockSpec((tm, D_MODEL), lambda j, i: (i, 0)),
                  pl.BlockSpec((SEG, D_MODEL), lambda j, i: (j, 0))],
        out_specs=pl.BlockSpec((1, tm, SEG), lambda j, i: (j, i, 0)),
        compiler_params=_cparams(("parallel", "parallel")),
        name="inproj_conv",
    )(xb, wt)


def _t5_bucket(d):
    max_exact = N_BUCKETS // 2
    df = jnp.maximum(d, 1).astype(F32)
    large = max_exact + (jnp.log(df / max_exact) / math.log(MAX_DISTANCE / max_exact)
                         * (N_BUCKETS - max_exact)).astype(jnp.int32)
    large = jnp.minimum(large, N_BUCKETS - 1)
    return jnp.where(d < max_exact, d, large)


def _prompt_bias_kernel(rb_ref, o_ref):
    h = pl.program_id(0)
    s = lax.broadcasted_iota(jnp.int32, (MOBA_BLOCK, MOBA_BLOCK), 0)
    t = lax.broadcasted_iota(jnp.int32, (MOBA_BLOCK, MOBA_BLOCK), 1)
    for k in range(2):
        dist = MOBA_BLOCK * k + t - s
        bucket = _t5_bucket(jnp.maximum(dist, 0))
        val = jnp.zeros((MOBA_BLOCK, MOBA_BLOCK), F32)
        for b in range(N_BUCKETS):
            val = jnp.where(bucket == b, rb_ref[b, h], val)
        o_ref[0, k] = jnp.where(dist >= 0, val * LOG2E, NEG)


def prompt_bias_tables(rel_bias):
    return pl.pallas_call(
        _prompt_bias_kernel,
        out_shape=jax.ShapeDtypeStruct((N_HEADS, 2, MOBA_BLOCK, MOBA_BLOCK), F32),
        grid=(N_HEADS,),
        in_specs=[pl.BlockSpec(memory_space=pltpu.SMEM)],
        out_specs=pl.BlockSpec((1, 2, MOBA_BLOCK, MOBA_BLOCK), lambda h: (h, 0, 0, 0)),
        compiler_params=_cparams(("parallel",)),
        name="prompt_bias_tables",
    )(rel_bias)


def _sample_bias_kernel(rb_ref, o_ref, *, t_new, past):
    rows = N_HEADS * t_new
    r = lax.broadcasted_iota(jnp.int32, (rows, LANE), 0)
    lane = lax.broadcasted_iota(jnp.int32, (rows, LANE), 1)
    t = r % t_new
    hh = r // t_new

    def lookup(dist):
        bucket = _t5_bucket(jnp.maximum(dist, 0))
        val = jnp.zeros((rows, LANE), F32)
        for b in range(N_BUCKETS):
            rb = jnp.zeros((rows, LANE), F32)
            for h in range(N_HEADS):
                rb = jnp.where(hh == h, rb_ref[b, h], rb)
            val = jnp.where(bucket == b, rb, val)
        return val

    o_ref[0] = lookup(LANE + t - lane)
    dist_new = t - lane
    o_ref[1] = jnp.where(dist_new >= 0, lookup(dist_new), NEG)
    o_ref[2] = lookup(jnp.full((rows, LANE), past, jnp.int32))


def sample_bias_tables(rel_bias, t_new, past):
    rows = N_HEADS * t_new
    return pl.pallas_call(
        functools.partial(_sample_bias_kernel, t_new=t_new, past=past),
        out_shape=jax.ShapeDtypeStruct((3, rows, LANE), F32),
        in_specs=[pl.BlockSpec(memory_space=pltpu.SMEM)],
        out_specs=pl.BlockSpec(memory_space=pltpu.VMEM),
        name="sample_bias_tables",
    )(rel_bias)


def _kmean_kernel(k_ref, o_ref, *, n_blk):
    for h in range(N_HEADS):
        k = k_ref[0, 0, h].reshape(n_blk, MOBA_BLOCK, HEAD_DIM)
        o_ref[0, h] = jnp.sum(k, axis=1) * (1.0 / MOBA_BLOCK)


def moba_kmean(k, n_batch, seq):
    n_blk = seq // MOBA_BLOCK
    return pl.pallas_call(
        functools.partial(_kmean_kernel, n_blk=n_blk),
        out_shape=jax.ShapeDtypeStruct((n_batch, N_HEADS, n_blk, HEAD_DIM), F32),
        grid=(n_batch,),
        in_specs=[pl.BlockSpec((1, 1, N_HEADS, seq, HEAD_DIM), lambda b: (k[1], b, 0, 0, 0))],
        out_specs=pl.BlockSpec((1, N_HEADS, n_blk, HEAD_DIM), lambda b: (b, 0, 0, 0)),
        compiler_params=_cparams(("parallel",)),
        name="moba_kmean",
    )(k[0])


def _log_sigmoid(x):
    return -(jnp.maximum(-x, 0.0) + jnp.log1p(jnp.exp(-jnp.abs(x))))


def _split3(x):
    hi = x.astype(BF16).astype(F32)
    r1 = x - hi
    mid = r1.astype(BF16).astype(F32)
    lo = (r1 - mid).astype(BF16).astype(F32)
    return hi, mid, lo


def _fox_prep_kernel(f_ref, bf_ref, lf_ref, kx_ref, qx_ref, carry_ref, *, blk):
    i = pl.program_id(1)

    @pl.when(i == 0)
    def _():
        carry_ref[...] = jnp.zeros_like(carry_ref)

    lane = lax.broadcasted_iota(jnp.int32, (blk, LANE), 1)
    lf = jnp.where(lane < N_HEADS, _log_sigmoid(f_ref[0] + bf_ref[...]), 0.0)
    lf_ref[...] = lf
    row = lax.broadcasted_iota(jnp.int32, (blk, blk), 0)
    col = lax.broadcasted_iota(jnp.int32, (blk, blk), 1)
    tril = (row >= col).astype(BF16)
    cum_all = carry_ref[0:1, :]
    for piece in _split3(lf):
        cum_all = cum_all + jnp.dot(tril, piece.astype(BF16), preferred_element_type=F32)
    carry_ref[0:1, :] = cum_all[blk - 1:blk, :]
    for h in range(N_HEADS):
        cum = jnp.broadcast_to(cum_all[:, h:h + 1], (blk, LANE))
        hi, mid, lo = _split3(cum * LOG2E)
        kx_ref[0, h] = jnp.where(lane == 0, -hi, jnp.where(lane == 1, -mid, jnp.where(
            lane == 2, -lo, jnp.where(lane < 6, 1.0, 0.0)))).astype(BF16)
        qx_ref[0, h] = jnp.where(lane == 3, hi, jnp.where(lane == 4, mid, jnp.where(
            lane == 5, lo, jnp.where(lane < 3, 1.0, 0.0)))).astype(BF16)


def fox_prep(zc, bf_row, n_batch, seq, blk=256):
    n_i = seq // blk
    xshape = jax.ShapeDtypeStruct((n_batch, N_HEADS, seq, LANE), BF16)
    xspec = pl.BlockSpec((1, N_HEADS, blk, LANE), lambda b, i: (b, 0, i, 0))
    return pl.pallas_call(
        functools.partial(_fox_prep_kernel, blk=blk),
        out_shape=(jax.ShapeDtypeStruct((n_batch * seq, LANE), F32), xshape, xshape),
        grid=(n_batch, n_i),
        in_specs=[pl.BlockSpec((1, blk, LANE), lambda b, i: (F_SEG, b * n_i + i, F_LANEBLK)),
                  pl.BlockSpec((1, LANE), lambda b, i: (0, 0))],
        out_specs=(pl.BlockSpec((blk, LANE), lambda b, i: (b * n_i + i, 0)), xspec, xspec),
        scratch_shapes=[pltpu.VMEM((SUBLANE, LANE), F32)],
        compiler_params=_cparams(("parallel", "arbitrary")),
        name="fox_prep",
    )(zc, bf_row)


HEADS_PER_STEP = 6
HEAD_GROUP = 3


def _attn_tiles(qs, k_ref, kx_ref, v_ref, b0, nb, bias_fn, state):
    m_ref, l_ref, acc_ref = state
    start = pl.multiple_of(b0 * MOBA_BLOCK, MOBA_BLOCK)
    tk = nb * MOBA_BLOCK
    hp = len(qs)
    scores = {}

    def logits(heads):
        for i in heads:
            kt = k_ref[0, 0, i, pl.ds(start, tk), :]
            if kx_ref is not None:
                kt = jnp.concatenate([kt, kx_ref[0, i, pl.ds(start, tk), :]], axis=1)
            scores[i] = bias_fn(lax.dot_general(kt, qs[i], NT_DIMS, preferred_element_type=F32), i, b0)

    def update(heads):
        probs = {}
        for i in heads:
            m = m_ref[i]
            m_new = jnp.maximum(m, jnp.max(scores[i], axis=0, keepdims=True))
            alpha = jnp.exp2(m - m_new)
            p = jnp.exp2(scores.pop(i) - m_new)
            m_ref[i] = m_new
            l_ref[i] = alpha * l_ref[i] + jnp.sum(p, axis=0, keepdims=True)
            probs[i] = (p.astype(BF16), alpha)
        for i in heads:
            p, alpha = probs[i]
            vt = v_ref[0, 0, i, pl.ds(start, tk), :]
            pv = lax.dot_general(vt, p, TN_DIMS, preferred_element_type=F32)
            acc_ref[i] = alpha * acc_ref[i] + pv

    groups = [range(i, min(i + HEAD_GROUP, hp)) for i in range(0, hp, HEAD_GROUP)]
    logits(groups[0])
    for gi, heads in enumerate(groups):
        if gi + 1 < len(groups):
            logits(groups[gi + 1])
        update(heads)


def _attn_state_shapes(tq, hp):
    return [pltpu.VMEM((hp, 1, tq), F32), pltpu.VMEM((hp, 1, tq), F32), pltpu.VMEM((hp, HEAD_DIM, tq), F32)]


def _attn_init(state):
    m_ref, l_ref, acc_ref = state
    m_ref[...] = jnp.full(m_ref.shape, NEG, F32)
    l_ref[...] = jnp.zeros(l_ref.shape, F32)
    acc_ref[...] = jnp.zeros(acc_ref.shape, F32)


def _attn_finish(o_ref, state):
    m_ref, l_ref, acc_ref = state
    for i in range(acc_ref.shape[0]):
        o_ref[0, i] = (acc_ref[i] / l_ref[i]).T


PAST_BLOCKS = 4


def _attn_drive(j, tiles):
    jf = jnp.maximum(j - 1, 0)

    @pl.loop(0, jf // PAST_BLOCKS)
    def _(n):
        tiles(n * PAST_BLOCKS, PAST_BLOCKS, False)

    nb = PAST_BLOCKS // 2
    while nb >= 1:
        @pl.when(jf % (2 * nb) >= nb)
        def _(nb=nb):
            tiles((jf // (2 * nb)) * (2 * nb), nb, False)
        nb //= 2

    @pl.when(j > 0)
    def _():
        tiles(j - 1, 2, True)

    @pl.when(j == 0)
    def _():
        tiles(0, 1, True)


def _fox_prompt_kernel(q_ref, qx_ref, k_ref, kx_ref, v_ref, o_ref, *state, tq, hp):
    j = pl.program_id(2)
    qs = [jnp.concatenate([q_ref[0, 0, i], qx_ref[0, i]], axis=1) for i in range(hp)]

    def tiles(b0, nb, last):
        def bias(s, i, b0):
            if not last:
                return s
            key = lax.broadcasted_iota(jnp.int32, s.shape, 0) - (nb - 1) * tq
            qry = lax.broadcasted_iota(jnp.int32, s.shape, 1)
            return jnp.where(key <= qry, s, NEG)
        _attn_tiles(qs, k_ref, kx_ref, v_ref, b0, nb, bias, state)

    _attn_init(state)
    _attn_drive(j, tiles)
    _attn_finish(o_ref, state)


def _head_spec(seg, hp, rows, row_map):
    return pl.BlockSpec((1, 1, hp, rows, HEAD_DIM), lambda b, g, j: (seg, b, g, row_map(j), 0))


def fox_prompt(q, k, v, kx, qx, n_batch, seq, tq=256, hp=HEADS_PER_STEP):
    nq = seq // tq
    return pl.pallas_call(
        functools.partial(_fox_prompt_kernel, tq=tq, hp=hp),
        out_shape=jax.ShapeDtypeStruct((n_batch, N_HEADS, seq, HEAD_DIM), F32),
        grid=(n_batch, N_HEADS // hp, nq),
        in_specs=[_head_spec(q[1], hp, tq, lambda j: j),
                  pl.BlockSpec((1, hp, tq, LANE), lambda b, g, j: (b, g, j, 0)),
                  _head_spec(k[1], hp, seq, lambda j: 0),
                  pl.BlockSpec((1, hp, seq, LANE), lambda b, g, j: (b, g, 0, 0)),
                  _head_spec(v[1], hp, seq, lambda j: 0)],
        out_specs=pl.BlockSpec((1, hp, tq, HEAD_DIM), lambda b, g, j: (b, g, j, 0)),
        scratch_shapes=_attn_state_shapes(tq, hp),
        compiler_params=_cparams(("parallel", "parallel", "arbitrary")),
        name="fox_prompt",
    )(q[0], qx, k[0], kx, v[0])


def _top_blocks(gate, n_valid, axis):
    n_blk = gate.shape[axis]
    blk = lax.broadcasted_iota(jnp.int32, gate.shape, axis)
    blk_f = blk.astype(F32)
    g = jnp.where(blk < n_valid, gate, -jnp.inf)
    sel = jnp.zeros(gate.shape, jnp.bool_)
    for _ in range(MOBA_TOPK):
        mx = jnp.max(g, axis=axis, keepdims=True)
        cand = (g == mx) & (mx > -jnp.inf)
        first = jnp.min(jnp.where(cand, blk_f, float(n_blk)), axis=axis, keepdims=True)
        pick = blk_f == first
        sel = sel | pick
        g = jnp.where(pick, -jnp.inf, g)
    return sel


def _moba_prompt_kernel(rb_ref, q32_ref, q_ref, k_ref, v_ref, km_ref, tab_ref, o_ref, mb_ref, *state, tq, hp):
    g = pl.program_id(1)
    j = pl.program_id(2)
    qs = [q_ref[0, 0, i] for i in range(hp)]
    fars = []
    for i in range(hp):
        far = rb_ref[N_BUCKETS - 1, g * hp + i] * LOG2E
        gate = lax.dot_general(km_ref[0, i], q32_ref[0, 0, i], NT_DIMS, precision=HIGHEST,
                               preferred_element_type=F32)
        mb_ref[i] = jnp.where(_top_blocks(gate, j, 0), far, NEG)
        fars.append(far)

    def tiles(b0, nb, last):
        def bias(s, i, b0):
            rows = lambda r: s[r * tq:(r + 1) * tq]
            if not last:
                return jnp.concatenate([rows(r) + mb_ref[i, pl.ds(b0 + r, 1), :] for r in range(nb)], axis=0)
            own = rows(nb - 1) + tab_ref[i, 0]
            if nb == 1:
                return own
            prev = rows(0) + tab_ref[i, 1] + (mb_ref[i, pl.ds(b0, 1), :] - fars[i])
            return jnp.concatenate([prev, own], axis=0)
        _attn_tiles(qs, k_ref, None, v_ref, b0, nb, bias, state)

    _attn_init(state)
    _attn_drive(j, tiles)
    _attn_finish(o_ref, state)


def moba_prompt(q32, q, k, v, kmean, tab, rel_bias, n_batch, seq, hp=HEADS_PER_STEP):
    tq = MOBA_BLOCK
    nq = seq // tq
    return pl.pallas_call(
        functools.partial(_moba_prompt_kernel, tq=tq, hp=hp),
        out_shape=jax.ShapeDtypeStruct((n_batch, N_HEADS, seq, HEAD_DIM), F32),
        grid=(n_batch, N_HEADS // hp, nq),
        in_specs=[pl.BlockSpec(memory_space=pltpu.SMEM),
                  _head_spec(q32[1], hp, tq, lambda j: j),
                  _head_spec(q[1], hp, tq, lambda j: j),
                  _head_spec(k[1], hp, seq, lambda j: 0),
                  _head_spec(v[1], hp, seq, lambda j: 0),
                  pl.BlockSpec((1, hp, nq, HEAD_DIM), lambda b, g, j: (b, g, 0, 0)),
                  pl.BlockSpec((hp, 2, tq, tq), lambda b, g, j: (g, 0, 0, 0))],
        out_specs=pl.BlockSpec((1, hp, tq, HEAD_DIM), lambda b, g, j: (b, g, j, 0)),
        scratch_shapes=[pltpu.VMEM((hp, nq, tq), F32)] + _attn_state_shapes(tq, hp),
        compiler_params=_cparams(("parallel", "parallel", "arbitrary")),
        name="moba_prompt",
    )(rel_bias, q32[0], q[0], k[0], v[0], kmean, tab)


def _pad_rows(x, rows):
    return jnp.concatenate([x, jnp.zeros((rows - x.shape[0], x.shape[1]), x.dtype)], axis=0)


def _lanes(a, b):
    return jnp.concatenate([a, b], axis=1)


def _rows(a, b):
    return jnp.concatenate([a, b], axis=0)


def _pair_queries(q_ref, g, scale):
    q0, q1 = q_ref[0, 0, 2 * g], q_ref[0, 0, 2 * g + 1]
    z = jnp.zeros_like(q0)
    return _rows(_lanes(q0, z), _lanes(z, q1)) * scale


def _pair_tile(pages, p0, g):
    t = lambda p: _lanes(pages[p][0, 0, 2 * g].astype(BF16), pages[p][0, 0, 2 * g + 1].astype(BF16))
    return _rows(t(p0), t(p0 + 1))


def _pair_new(x_ref, g, page):
    return _lanes(_pad_rows(x_ref[0, 0, 2 * g], page), _pad_rows(x_ref[0, 0, 2 * g + 1], page)).astype(BF16)


def _pair_bias(top, bottom, t_new):
    return _rows(jnp.broadcast_to(top, (t_new, top.shape[1])), jnp.broadcast_to(bottom, (t_new, bottom.shape[1])))


def _softmax_pv_pair(o_ref, g, s_list, s_new, v_list, v_new, t_new):
    m = s_list[0]
    for s in s_list[1:]:
        m = jnp.maximum(m, s)
    m = jnp.max(jnp.maximum(jnp.maximum(m[:, :LANE], m[:, LANE:]), s_new), axis=1, keepdims=True)
    acc = None
    l = None
    for s, v in zip(s_list, v_list):
        p = jnp.exp(s - m)
        l = p if l is None else l + p
        pv = jnp.dot(p.astype(BF16), v, preferred_element_type=F32)
        acc = pv if acc is None else acc + pv
    p = jnp.exp(s_new - m)
    l = jnp.sum(l[:, :LANE] + l[:, LANE:] + p, axis=1, keepdims=True)
    out = (acc + jnp.dot(p.astype(BF16), v_new, preferred_element_type=F32)) / l
    o_ref[0, 2 * g] = out[:t_new, :HEAD_DIM]
    o_ref[0, 2 * g + 1] = out[t_new:, HEAD_DIM:]


def _fox_sample_kernel(pt_ref, q_ref, k_ref, v_ref, f_ref, bf_ref, *rest, t_new, n_pages, page):
    k_pages = rest[:n_pages]
    v_pages = rest[n_pages:2 * n_pages]
    lf_pages = rest[2 * n_pages:3 * n_pages]
    o_ref, lf_ref = rest[3 * n_pages:]

    n_r = n_pages * SUBLANE
    lane = lax.broadcasted_iota(jnp.int32, (n_r, page), 1)
    srow = lax.broadcasted_iota(jnp.int32, (n_r, page), 0)
    lfc = jnp.concatenate([r[0, 0] for r in lf_pages], axis=0)
    incl = lfc
    step = 1
    while step < page:
        incl = incl + jnp.where(lane + step < page, pltpu.roll(incl, page - step, axis=1), 0.0)
        step *= 2
    tot = jnp.broadcast_to(jnp.sum(jnp.where(lane == 0, incl, 0.0), axis=1, keepdims=True), (n_r, page))
    run = tot
    step = SUBLANE
    while step < n_r:
        run = run + jnp.where(srow + step < n_r, pltpu.roll(run, n_r - step, axis=0), 0.0)
        step *= 2
    suf = (incl - lfc) + (run - tot)

    flane = lax.broadcasted_iota(jnp.int32, (t_new, LANE), 1)
    lf_new = jnp.where(flane < N_HEADS, _log_sigmoid(f_ref[0] + bf_ref[...]), 0.0)
    lf_ref[...] = lf_new
    cum = _pad_rows(lf_new, LANE).T
    lane2 = lax.broadcasted_iota(jnp.int32, (LANE, LANE), 1)
    step = 1
    while step < t_new:
        cum = cum + jnp.where(lane2 >= step, pltpu.roll(cum, step, axis=1), 0.0)
        step *= 2
    r = lax.broadcasted_iota(jnp.int32, (2 * t_new, page), 0) % t_new
    c = lax.broadcasted_iota(jnp.int32, (2 * t_new, page), 1)
    suf_row = lambda p, h: suf[p * SUBLANE + h:p * SUBLANE + h + 1, :]

    for g in range(N_HEADS // 2):
        h0, h1 = 2 * g, 2 * g + 1
        qp = _pair_queries(q_ref, g, SCALE).astype(BF16)
        s_list, v_list = [], []
        for p0 in range(0, n_pages, 2):
            s = lax.dot_general(qp, _pair_tile(k_pages, p0, g), NT_DIMS, preferred_element_type=F32)
            bias = _pair_bias(_lanes(suf_row(p0, h0), suf_row(p0 + 1, h0)),
                              _lanes(suf_row(p0, h1), suf_row(p0 + 1, h1)), t_new)
            s_list.append(s + bias)
            v_list.append(_pair_tile(v_pages, p0, g))
        s_new = lax.dot_general(qp, _pair_new(k_ref, g, page), NT_DIMS, preferred_element_type=F32)
        s_new = jnp.where(c <= r, s_new - _pair_bias(cum[h0:h0 + 1, :], cum[h1:h1 + 1, :], t_new), NEG)
        _softmax_pv_pair(o_ref, g, s_list, s_new, v_list, _pair_new(v_ref, g, page), t_new)


def _page_specs(n_pages, layer, shape):
    def spec(p):
        return pl.BlockSpec((1, 1) + shape, lambda b, pt: (layer, pt[b, p]) + (0,) * len(shape))
    return [spec(p) for p in range(n_pages)]


def _new_spec(seg, t_new):
    return pl.BlockSpec((1, 1, N_HEADS, t_new, HEAD_DIM), lambda b, pt: (seg, b, 0, 0, 0))


def fox_sample(q, k, v, zc, bf_row, pool_k, pool_v, pool_lft, page_table, layer, t_new):
    n_seq, n_pages = page_table.shape
    page = pool_k.shape[3]
    in_specs = ([_new_spec(q[1], t_new), _new_spec(k[1], t_new), _new_spec(v[1], t_new),
                 pl.BlockSpec((1, t_new, LANE), lambda b, pt: (F_SEG, b, F_LANEBLK)),
                 pl.BlockSpec((1, LANE), lambda b, pt: (0, 0))]
                + _page_specs(n_pages, layer, (N_HEADS, page, HEAD_DIM))
                + _page_specs(n_pages, layer, (N_HEADS, page, HEAD_DIM))
                + _page_specs(n_pages, layer, (SUBLANE, page)))
    return pl.pallas_call(
        functools.partial(_fox_sample_kernel, t_new=t_new, n_pages=n_pages, page=page),
        out_shape=(jax.ShapeDtypeStruct((n_seq, N_HEADS, t_new, HEAD_DIM), F32),
                   jax.ShapeDtypeStruct((n_seq * t_new, LANE), F32)),
        grid_spec=pltpu.PrefetchScalarGridSpec(
            num_scalar_prefetch=1, grid=(n_seq,), in_specs=in_specs,
            out_specs=(pl.BlockSpec((1, N_HEADS, t_new, HEAD_DIM), lambda b, pt: (b, 0, 0, 0)),
                       pl.BlockSpec((t_new, LANE), lambda b, pt: (b, 0)))),
        compiler_params=_cparams(("parallel",)),
        name="fox_sample",
    )(page_table, q[0], k[0], v[0], zc, bf_row,
      *([pool_k] * n_pages), *([pool_v] * n_pages), *([pool_lft] * n_pages))


def _moba_sample_kernel(pt_ref, q_ref, k_ref, v_ref, tab_ref, *rest, t_new, n_pages, page):
    k_pages = rest[:n_pages]
    v_pages = rest[n_pages:2 * n_pages]
    o_ref = rest[2 * n_pages]
    assert MOBA_BLOCK == 2 * page
    n_blk = n_pages // 2

    def block_means(h):
        sums = [jnp.sum(k_pages[2 * n][0, 0, h] + k_pages[2 * n + 1][0, 0, h], axis=0, keepdims=True)
                for n in range(n_blk)]
        return _pad_rows(jnp.concatenate(sums, axis=0) * (1.0 / MOBA_BLOCK), LANE)

    for g in range(N_HEADS // 2):
        q32 = _pair_queries(q_ref, g, 1.0)
        qp = (q32 * SCALE).astype(BF16)
        kmean = _lanes(block_means(2 * g), block_means(2 * g + 1))
        gate = lax.dot_general(q32, kmean, NT_DIMS, precision=HIGHEST, preferred_element_type=F32)
        sel = _top_blocks(gate, n_blk, 1)
        mask = jnp.where(sel, 0.0, NEG)
        rows = slice(2 * g * t_new, (2 * g + 2) * t_new)
        far = tab_ref[2, rows, :]
        s_list, v_list = [], []
        for n in range(n_blk):
            s = lax.dot_general(qp, _pair_tile(k_pages, 2 * n, g), NT_DIMS, preferred_element_type=F32)
            bias = _lanes(far, tab_ref[0, rows, :] if n == n_blk - 1 else far)
            s_list.append(s + bias + mask[:, n:n + 1])
            v_list.append(_pair_tile(v_pages, 2 * n, g))
        s_new = lax.dot_general(qp, _pair_new(k_ref, g, page), NT_DIMS, preferred_element_type=F32)
        _softmax_pv_pair(o_ref, g, s_list, s_new + tab_ref[1, rows, :], v_list, _pair_new(v_ref, g, page), t_new)


def moba_sample(q, k, v, tab, pool_k, pool_v, page_table, layer, t_new):
    n_seq, n_pages = page_table.shape
    page = pool_k.shape[3]
    rows = N_HEADS * t_new
    in_specs = ([_new_spec(q[1], t_new), _new_spec(k[1], t_new), _new_spec(v[1], t_new),
                 pl.BlockSpec((3, rows, LANE), lambda b, pt: (0, 0, 0))]
                + _page_specs(n_pages, layer, (N_HEADS, page, HEAD_DIM))
                + _page_specs(n_pages, layer, (N_HEADS, page, HEAD_DIM)))
    return pl.pallas_call(
        functools.partial(_moba_sample_kernel, t_new=t_new, n_pages=n_pages, page=page),
        out_shape=jax.ShapeDtypeStruct((n_seq, N_HEADS, t_new, HEAD_DIM), F32),
        grid_spec=pltpu.PrefetchScalarGridSpec(
            num_scalar_prefetch=1, grid=(n_seq,), in_specs=in_specs,
            out_specs=pl.BlockSpec((1, N_HEADS, t_new, HEAD_DIM), lambda b, pt: (b, 0, 0, 0))),
        compiler_params=_cparams(("parallel",)),
        name="moba_sample",
    )(page_table, q[0], k[0], v[0], tab, *([pool_k] * n_pages), *([pool_v] * n_pages))


def _silu(g):
    return g / (1.0 + jnp.exp(-g))


def _token_major(v, tm):
    return jnp.concatenate([v[:, h].reshape(tm, HEAD_DIM) for h in range(N_HEADS)], axis=1)


def _combine_kernel(*refs, tm, seq_tiles, seq_rows, alpha):
    (x_ref, oa_ref, of_ref, ga_ref, gf_ref, b0, b1, c0, c1, h0, h1, g0, g1) = refs[:13]
    if seq_rows is None:
        pc0, pc1, ph0, ph1 = refs[13:17]
        rest = refs[17:]
    else:
        hist1_ref, hist2_ref = refs[13:15]
        rest = refs[15:]
    wout_ref, cw_ref, lng_ref, lnb_ref, y_ref, yb_ref, u_ref = rest
    cat = lambda a, b: jnp.concatenate([a[0], b[0]], axis=1)
    u = cat(c0, c1) * cat(h0, h1)
    if seq_rows is None:
        keep = (pl.program_id(0) % seq_tiles != 0).astype(F32)
        uprev = cat(pc0, pc1) * cat(ph0, ph1) * keep
        ext = jnp.concatenate([uprev, u], axis=0)
        u1 = pltpu.roll(ext, 1, axis=0)[SUBLANE:]
        u2 = pltpu.roll(ext, 2, axis=0)[SUBLANE:]
        u_ref[0] = u[tm - SUBLANE:, :]
    else:
        r = lax.broadcasted_iota(jnp.int32, (tm, D_CONV), 0) % seq_rows
        u1 = jnp.where(r >= 1, pltpu.roll(u, 1, axis=0), hist1_ref[...])
        u2 = jnp.where(r >= 2, pltpu.roll(u, 2, axis=0), hist2_ref[...])
        u_ref[...] = u
    conv = cw_ref[0:1, :] * u2 + cw_ref[1:2, :] * u1 + cw_ref[2:3, :] * u
    oc = (cat(b0, b1) * conv * _silu(cat(g0, g1))).astype(BF16)
    oa = (_token_major(oa_ref[...], tm) * _silu(_token_major(ga_ref[0], tm))).astype(BF16)
    of = (_token_major(of_ref[...], tm) * _silu(_token_major(gf_ref[0], tm))).astype(BF16)
    proj = (jnp.dot(oa, wout_ref[0:D_ATT, :], preferred_element_type=F32)
            + jnp.dot(of, wout_ref[D_ATT:2 * D_ATT, :], preferred_element_type=F32)
            + jnp.dot(oc, wout_ref[2 * D_ATT:, :], preferred_element_type=F32))
    res = alpha * x_ref[...] + proj
    mu = jnp.mean(res, axis=-1, keepdims=True)
    cen = res - mu
    var = jnp.mean(cen * cen, axis=-1, keepdims=True)
    y = cen * lax.rsqrt(var + LN_EPS) * lng_ref[...] + lnb_ref[...]
    y_ref[...] = y
    yb_ref[...] = y.astype(BF16)


def combine(x, oa, of, zg, zc, wout, cw, lng, lnb, alpha, tm, hist=None):
    t = x.shape[0]
    n_seq, _, seq_len, _ = oa.shape
    assert t == n_seq * seq_len and t % tm == 0
    if seq_len >= tm:
        assert seq_len % tm == 0
        sb, sl, per_seq = 1, tm, seq_len // tm
        hmap = lambda i: (i // per_seq, 0, i % per_seq, 0)
    else:
        assert tm % seq_len == 0
        sb, sl, per_seq = tm // seq_len, seq_len, 1
        hmap = lambda i: (i, 0, 0, 0)
    hblk = (sb, N_HEADS, sl, HEAD_DIM)
    zblk = lambda sc: pl.BlockSpec((1, tm, CBLK), lambda i: (sc[0], i, sc[1]))
    pieces = [sc for idx in (0, 1, 2, 3) for sc in _conv_piece(idx)]
    in_specs = [pl.BlockSpec((tm, D_MODEL), lambda i: (i, 0)),
                pl.BlockSpec(hblk, hmap),
                pl.BlockSpec(hblk, hmap),
                pl.BlockSpec((1,) + hblk, lambda i: (0,) + hmap(i)),
                pl.BlockSpec((1,) + hblk, lambda i: (1,) + hmap(i))] + [zblk(sc) for sc in pieces]
    args = [x, oa, of, zg, zg] + [zc] * 8
    if hist is None:
        seq_rows = None
        tb = tm // SUBLANE
        prev = lambda sc: pl.BlockSpec((1, SUBLANE, CBLK), lambda i: (sc[0], jnp.maximum(i * tb - 1, 0), sc[1]))
        in_specs += [prev(sc) for sc in pieces[2:6]]
        args += [zc] * 4
        u_shape = jax.ShapeDtypeStruct((n_seq, SUBLANE, D_CONV), F32)
        u_spec = pl.BlockSpec((1, SUBLANE, D_CONV), lambda i: (i // per_seq, 0, 0))
    else:
        seq_rows = seq_len
        in_specs += [pl.BlockSpec((tm, D_CONV), lambda i: (i, 0))] * 2
        args += list(hist)
        u_shape = jax.ShapeDtypeStruct((t, D_CONV), F32)
        u_spec = pl.BlockSpec((tm, D_CONV), lambda i: (i, 0))
    in_specs += [pl.BlockSpec((D_MODEL, D_MODEL), lambda i: (0, 0)),
                 pl.BlockSpec((SUBLANE, D_CONV), lambda i: (0, 0)),
                 pl.BlockSpec((1, D_MODEL), lambda i: (0, 0)),
                 pl.BlockSpec((1, D_MODEL), lambda i: (0, 0))]
    args += [wout, cw, lng, lnb]
    return pl.pallas_call(
        functools.partial(_combine_kernel, tm=tm, seq_tiles=per_seq, seq_rows=seq_rows, alpha=alpha),
        out_shape=(jax.ShapeDtypeStruct((t, D_MODEL), F32), jax.ShapeDtypeStruct((t, D_MODEL), BF16), u_shape),
        grid=(t // tm,),
        in_specs=in_specs,
        out_specs=(pl.BlockSpec((tm, D_MODEL), lambda i: (i, 0)), pl.BlockSpec((tm, D_MODEL), lambda i: (i, 0)),
                   u_spec),
        compiler_params=_cparams(("arbitrary",)),
        name="combine_prompt" if hist is None else "combine_sample",
    )(*args)


def kernel(x_prompt, x_sample, cache_moba_k, cache_moba_v, cache_fox_k, cache_fox_v, cache_fox_logf,
           state_conv, page_table, rel_bias, w_in, b_f, conv_w, w_out, ln_g, ln_b):
    depth = w_in.shape[0]
    alpha = (2 * depth) ** 0.25
    n_batch, seq, _ = x_prompt.shape
    n_seq, t_new, _ = x_sample.shape
    n_phys, page = cache_moba_k.shape[1:3]
    past = page_table.shape[1] * page
    assert page == LANE and past % MOBA_BLOCK == 0 and t_new == SUBLANE and seq % MOBA_BLOCK == 0
    assert past >= MAX_DISTANCE and w_in.shape[2] == N_MAIN + N_HEADS + 4 * D_CONV

    xp = x_prompt.reshape(n_batch * seq, D_MODEL)
    xs = x_sample.reshape(n_seq * t_new, D_MODEL)
    xpb, xsb = xp.astype(BF16), xs.astype(BF16)
    pool = lambda c: jnp.swapaxes(c, 2, 3)
    pk_a, pv_a, pk_f, pv_f = pool(cache_moba_k), pool(cache_moba_v), pool(cache_fox_k), pool(cache_fox_v)
    pool_lft = jnp.pad(jnp.swapaxes(cache_fox_logf, 2, 3), ((0, 0), (0, 0), (0, SUBLANE - N_HEADS), (0, 0)))
    tab_p = prompt_bias_tables(rel_bias)
    tab_s = sample_bias_tables(rel_bias, t_new, past)
    tm_s = min(256, n_seq * t_new)

    outs = {k: [] for k in ("p_lf", "p_cv", "s_lf", "s_cv")}
    kv_segs = {"ka": SEG_KA, "va": SEG_VA, "kf": SEG_KF, "vf": SEG_VF}
    p_kv = dict.fromkeys(kv_segs)
    s_kv = dict.fromkeys(kv_segs)
    for layer in range(depth):
        wt = jnp.swapaxes(w_in[layer], 0, 1)
        n_f = N_MAIN + N_HEADS
        wt_h = wt[:N_MAIN].astype(BF16)
        wt_c = jnp.concatenate(
            [wt[n_f:], wt[N_MAIN:n_f],
             jnp.zeros((N_CONV_SEG * SEG - 4 * D_CONV - N_HEADS, D_MODEL), wt.dtype)], axis=0).astype(BF16)
        wo = w_out[layer].astype(BF16)
        bf_row = jnp.pad(b_f[layer], (0, LANE - N_HEADS)).reshape(1, LANE)
        cw = jnp.pad(conv_w[layer], ((0, SUBLANE - CONV_WIDTH), (0, 0)))
        lng, lnb = ln_g[layer].reshape(1, D_MODEL), ln_b[layer].reshape(1, D_MODEL)

        proj = functools.partial(inproj_heads, xpb, wt_h, n_seq=n_batch, seq_len=seq, tm=1024)
        zq, zqb = proj((SEG_QA, SEG_QF), bf16_scale=QSCALE)
        (zg,) = proj((SEG_GA, SEG_GF))
        kvb = {}
        for name, seg in kv_segs.items():
            p_kv[name], kvb[name] = proj((seg,), bf16_scale=1.0, stack=(depth, layer, p_kv[name]))
        zc = inproj_conv(xpb, wt_c, tm=1024)
        kmean = moba_kmean((p_kv["ka"], layer), n_batch, seq)
        oa = moba_prompt((zq, 0), (zqb, 0), (kvb["ka"], 0), (kvb["va"], 0), kmean, tab_p, rel_bias, n_batch, seq)
        lf, kx, qx = fox_prep(zc, bf_row, n_batch, seq)
        of = fox_prompt((zqb, 1), (kvb["kf"], 0), (kvb["vf"], 0), kx, qx, n_batch, seq)
        xp, xpb, ulast = combine(xp, oa, of, zg, zc, wo, cw, lng, lnb, alpha, tm=256)
        outs["p_lf"].append(lf[:, :N_HEADS].reshape(n_batch, seq, N_HEADS))
        outs["p_cv"].append(ulast[:, SUBLANE - (CONV_WIDTH - 1):, :])

        proj = functools.partial(inproj_heads, xsb, wt_h, n_seq=n_seq, seq_len=t_new, tm=n_seq * t_new)
        (zqs,) = proj((SEG_QA, SEG_QF))
        (zgs,) = proj((SEG_GA, SEG_GF))
        for name, seg in kv_segs.items():
            (s_kv[name],) = proj((seg,), stack=(depth, layer, s_kv[name]))
        zcs = inproj_conv(xsb, wt_c, tm=n_seq * t_new)
        oas = moba_sample((zqs, 0), (s_kv["ka"], layer), (s_kv["va"], layer), tab_s, pk_a, pv_a, page_table,
                          layer, t_new)
        ofs, lfs = fox_sample((zqs, 1), (s_kv["kf"], layer), (s_kv["vf"], layer), zcs, bf_row, pk_f, pv_f,
                              pool_lft, page_table, layer, t_new)
        st = state_conv[layer]
        zero = jnp.zeros((n_seq, t_new, D_CONV), F32)
        hist1 = zero.at[:, 0].set(st[:, 1]).reshape(n_seq * t_new, D_CONV)
        hist2 = zero.at[:, 0].set(st[:, 0]).at[:, 1].set(st[:, 1]).reshape(n_seq * t_new, D_CONV)
        xs, xsb, us = combine(xs, oas, ofs, zgs, zcs, wo, cw, lng, lnb, alpha, tm=tm_s, hist=(hist1, hist2))
        outs["s_lf"].append(lfs[:, :N_HEADS].reshape(n_seq, t_new, N_HEADS))
        outs["s_cv"].append(us.reshape(n_seq, t_new, D_CONV)[:, t_new - (CONV_WIDTH - 1):, :])

    st = {k: jnp.stack(v) for k, v in outs.items()}
    tok_major = lambda a: jnp.swapaxes(a, 2, 3)
    return (xp.reshape(n_batch, seq, D_MODEL), xs.reshape(n_seq, t_new, D_MODEL),
            tok_major(p_kv["ka"]), tok_major(p_kv["va"]), tok_major(p_kv["kf"]), tok_major(p_kv["vf"]),
            st["p_lf"], st["p_cv"],
            tok_major(s_kv["ka"]), tok_major(s_kv["va"]), tok_major(s_kv["kf"]), tok_major(s_kv["vf"]),
            st["s_lf"], st["s_cv"])
```

```python
import functools
import math
from typing import Callable, NamedTuple

import jax
import jax.numpy as jnp
from jax import lax
from jax.experimental import pallas as pl
from jax.experimental.pallas import tpu as pltpu

F32 = jnp.float32
BF16 = jnp.bfloat16
HIGHEST = lax.Precision.HIGHEST

D_MODEL = 2048
HEAD_DIM = 128
N_HEADS = 6
D_ATT = N_HEADS * HEAD_DIM
D_CONV = 512
CONV_WIDTH = 3
MOBA_BLOCK = 256
MOBA_TOPK = 3
N_BUCKETS = 32
MAX_DISTANCE = 128
LN_EPS = 1e-5
SCALE = HEAD_DIM ** -0.5
LOG2E = math.log2(math.e)
QSCALE = SCALE * LOG2E

SEG = 768
SEG_QA, SEG_KA, SEG_VA, SEG_GA, SEG_QF, SEG_KF, SEG_VF, SEG_GF = range(8)
N_HEAD_SEG = 8
N_CONV_SEG = 3
N_MAIN = N_HEAD_SEG * SEG
CBLK = 256
LANE = 128
SUBLANE = 8
F_SEG, F_LANEBLK = (4 * D_CONV) // SEG, ((4 * D_CONV) % SEG) // LANE
NEG = -1e30
VMEM_LIMIT = 56 * 1024 * 1024

NT_DIMS = (((1,), (1,)), ((), ()))
TN_DIMS = (((0,), (0,)), ((), ()))


def _cparams(sem):
    return pltpu.CompilerParams(dimension_semantics=sem, vmem_limit_bytes=VMEM_LIMIT)


class _Call(NamedTuple):
    name: str
    body: Callable
    grid: tuple
    in_specs: list
    out_specs: list
    out_shape: list
    args: list
    aliases: dict
    page_table: object = None


def run(c):
    if c.page_table is None:
        return pl.pallas_call(
            c.body, out_shape=tuple(c.out_shape), grid=c.grid, in_specs=c.in_specs, out_specs=tuple(c.out_specs),
            input_output_aliases=c.aliases, compiler_params=_cparams(("arbitrary",) * len(c.grid)), name=c.name,
        )(*c.args)
    assert not c.aliases
    return pl.pallas_call(
        c.body, out_shape=tuple(c.out_shape),
        grid_spec=pltpu.PrefetchScalarGridSpec(num_scalar_prefetch=1, grid=c.grid, in_specs=c.in_specs,
                                               out_specs=tuple(c.out_specs)),
        compiler_params=_cparams(("arbitrary",) * len(c.grid)), name=c.name,
    )(c.page_table, *c.args)


def run_zipped(a, b):
    n = b.grid[0]
    assert a.page_table is None and b.page_table is not None and len(b.grid) == 1 and math.prod(a.grid) == n

    def a_idx(k):
        return (k,) if len(a.grid) == 1 else (k // a.grid[1], k % a.grid[1])

    def respec(sp, fn):
        if sp.block_shape is None:
            return sp
        return pl.BlockSpec(sp.block_shape, fn(sp.index_map), memory_space=sp.memory_space)

    for_a = lambda m: (lambda k, pt: m(*a_idx(k)))
    for_b = lambda m: m
    na_in, nb_in, na_out = len(a.in_specs), len(b.in_specs), len(a.out_specs)

    def body(pt_ref, *refs):
        a.body(*(refs[:na_in] + refs[na_in + nb_in:na_in + nb_in + na_out]))
        b.body(pt_ref, *(refs[na_in:na_in + nb_in] + refs[na_in + nb_in + na_out:]))

    outs = pl.pallas_call(
        body, out_shape=tuple(a.out_shape) + tuple(b.out_shape),
        grid_spec=pltpu.PrefetchScalarGridSpec(
            num_scalar_prefetch=1, grid=(n,),
            in_specs=[respec(sp, for_a) for sp in a.in_specs] + [respec(sp, for_b) for sp in b.in_specs],
            out_specs=tuple([respec(sp, for_a) for sp in a.out_specs] + [respec(sp, for_b) for sp in b.out_specs])),
        input_output_aliases={1 + i: o for i, o in a.aliases.items()},
        compiler_params=_cparams(("arbitrary",)), name=a.name + "_zip_" + b.name,
    )(b.page_table, *a.args, *b.args)
    return outs[:na_out], outs[na_out:]


def _conv_piece(idx):
    out = []
    for half in range(2):
        col = idx * D_CONV + half * CBLK
        out.append((col // SEG, (col % SEG) // CBLK))
    return out


def _inproj_heads_kernel(x_ref, w_ref, *refs, sb, sl, n_in, bf16_scale):
    o_ref = refs[n_in]
    res = lax.dot_general(x_ref[...], w_ref[...], NT_DIMS, preferred_element_type=F32)
    for h in range(N_HEADS):
        o_ref[0, :, h] = res[:, h * HEAD_DIM:(h + 1) * HEAD_DIM].reshape(sb, sl, HEAD_DIM)
    if bf16_scale is not None:
        scaled = (res * bf16_scale).astype(BF16)
        for h in range(N_HEADS):
            refs[n_in + 1][0, :, h] = scaled[:, h * HEAD_DIM:(h + 1) * HEAD_DIM].reshape(sb, sl, HEAD_DIM)


def inproj_heads(xb, wt, segs, n_seq, seq_len, tm, bf16_scale=None, stack=None):
    t = xb.shape[0]
    assert t == n_seq * seq_len and t % tm == 0 and wt.shape == (N_MAIN, D_MODEL)
    if seq_len >= tm:
        assert seq_len % tm == 0
        sb, sl, per_seq = 1, tm, seq_len // tm
        rows = lambda i: (i // per_seq, 0, i % per_seq, 0)
    else:
        assert tm % seq_len == 0
        sb, sl = tm // seq_len, seq_len
        rows = lambda i: (i, 0, 0, 0)
    blk = (1, sb, N_HEADS, sl, HEAD_DIM)
    tail = (n_seq, N_HEADS, seq_len, HEAD_DIM)

    def seg_of(j):
        s = segs[0]
        for idx in range(1, len(segs)):
            s = jnp.where(j == idx, segs[idx], s)
        return s

    in_specs = [pl.BlockSpec((tm, D_MODEL), lambda j, i: (i, 0)),
                pl.BlockSpec((SEG, D_MODEL), lambda j, i: (seg_of(j), 0))]
    args = [xb, wt]
    aliases = {}
    if stack is None:
        shapes = [jax.ShapeDtypeStruct((len(segs),) + tail, F32)]
        specs = [pl.BlockSpec(blk, lambda j, i: (j,) + rows(i))]
    else:
        depth, layer, prev = stack
        assert len(segs) == 1
        shapes = [jax.ShapeDtypeStruct((depth,) + tail, F32)]
        specs = [pl.BlockSpec(blk, lambda j, i: (layer,) + rows(i))]
        if prev is not None:
            in_specs.append(pl.BlockSpec(memory_space=pl.ANY))
            args.append(prev)
            aliases = {2: 0}
    if bf16_scale is not None:
        shapes.append(jax.ShapeDtypeStruct((len(segs),) + tail, BF16))
        specs.append(pl.BlockSpec(blk, lambda j, i: (j,) + rows(i)))
    body = functools.partial(_inproj_heads_kernel, sb=sb, sl=sl, n_in=len(args) - 2, bf16_scale=bf16_scale)
    return _Call("inproj_heads", body, (len(segs), t // tm), in_specs, specs, shapes, args, aliases)


def _inproj_conv_kernel(x_ref, w_ref, o_ref):
    o_ref[0] = lax.dot_general(x_ref[...], w_ref[...], NT_DIMS, preferred_element_type=F32)


def inproj_conv(xb, wt, tm):
    t = xb.shape[0]
    assert t % tm == 0 and wt.shape == (N_CONV_SEG * SEG, D_MODEL)
    return _Call("inproj_conv", _inproj_conv_kernel, (N_CONV_SEG, t // tm),
                 [pl.BlockSpec((tm, D_MODEL), lambda j, i: (i, 0)),
                  pl.BlockSpec((SEG, D_MODEL), lambda j, i: (j, 0))],
                 [pl.BlockSpec((1, tm, SEG), lambda j, i: (j, i, 0))],
                 [jax.ShapeDtypeStruct((N_CONV_SEG, t, SEG), F32)], [xb, wt], {})


def _t5_bucket(d):
    max_exact = N_BUCKETS // 2
    df = jnp.maximum(d, 1).astype(F32)
    large = max_exact + (jnp.log(df / max_exact) / math.log(MAX_DISTANCE / max_exact)
                         * (N_BUCKETS - max_exact)).astype(jnp.int32)
    large = jnp.minimum(large, N_BUCKETS - 1)
    return jnp.where(d < max_exact, d, large)


def _prompt_bias_kernel(rb_ref, o_ref):
    h = pl.program_id(0)
    s = lax.broadcasted_iota(jnp.int32, (MOBA_BLOCK, MOBA_BLOCK), 0)
    t = lax.broadcasted_iota(jnp.int32, (MOBA_BLOCK, MOBA_BLOCK), 1)
    for k in range(2):
        dist = MOBA_BLOCK * k + t - s
        bucket = _t5_bucket(jnp.maximum(dist, 0))
        val = jnp.zeros((MOBA_BLOCK, MOBA_BLOCK), F32)
        for b in range(N_BUCKETS):
            val = jnp.where(bucket == b, rb_ref[b, h], val)
        o_ref[0, k] = jnp.where(dist >= 0, val * LOG2E, NEG)


def prompt_bias_tables(rel_bias):
    return pl.pallas_call(
        _prompt_bias_kernel,
        out_shape=jax.ShapeDtypeStruct((N_HEADS, 2, MOBA_BLOCK, MOBA_BLOCK), F32),
        grid=(N_HEADS,),
        in_specs=[pl.BlockSpec(memory_space=pltpu.SMEM)],
        out_specs=pl.BlockSpec((1, 2, MOBA_BLOCK, MOBA_BLOCK), lambda h: (h, 0, 0, 0)),
        compiler_params=_cparams(("parallel",)),
        name="prompt_bias_tables",
    )(rel_bias)


def _sample_bias_kernel(rb_ref, o_ref, *, t_new, past):
    rows = N_HEADS * t_new
    r = lax.broadcasted_iota(jnp.int32, (rows, LANE), 0)
    lane = lax.broadcasted_iota(jnp.int32, (rows, LANE), 1)
    t = r % t_new
    hh = r // t_new

    def lookup(dist):
        bucket = _t5_bucket(jnp.maximum(dist, 0))
        val = jnp.zeros((rows, LANE), F32)
        for b in range(N_BUCKETS):
            rb = jnp.zeros((rows, LANE), F32)
            for h in range(N_HEADS):
                rb = jnp.where(hh == h, rb_ref[b, h], rb)
            val = jnp.where(bucket == b, rb, val)
        return val

    o_ref[0] = lookup(LANE + t - lane)
    dist_new = t - lane
    o_ref[1] = jnp.where(dist_new >= 0, lookup(dist_new), NEG)
    o_ref[2] = lookup(jnp.full((rows, LANE), past, jnp.int32))


def sample_bias_tables(rel_bias, t_new, past):
    rows = N_HEADS * t_new
    return pl.pallas_call(
        functools.partial(_sample_bias_kernel, t_new=t_new, past=past),
        out_shape=jax.ShapeDtypeStruct((3, rows, LANE), F32),
        in_specs=[pl.BlockSpec(memory_space=pltpu.SMEM)],
        out_specs=pl.BlockSpec(memory_space=pltpu.VMEM),
        name="sample_bias_tables",
    )(rel_bias)


def _moba_select_kernel(rb_ref, k_ref, q_ref, o_ref, *, n_blk):
    h = pl.program_id(1)
    seq = q_ref.shape[3]
    kmean = jnp.sum(k_ref[0, 0, 0].reshape(n_blk, MOBA_BLOCK, HEAD_DIM), axis=1) * (1.0 / MOBA_BLOCK)
    gate = lax.dot_general(kmean, q_ref[0, 0, 0], NT_DIMS, precision=HIGHEST, preferred_element_type=F32)
    own = lax.broadcasted_iota(jnp.int32, (1, seq), 1) // MOBA_BLOCK
    far = rb_ref[N_BUCKETS - 1, h] * LOG2E
    o_ref[0, 0] = jnp.where(_top_blocks(gate, own, 0), far, NEG)


def moba_select(k, q, rel_bias, n_batch, seq):
    n_blk = seq // MOBA_BLOCK
    spec = lambda a: pl.BlockSpec((1, 1, 1, seq, HEAD_DIM), lambda b, h: (a[1], b, h, 0, 0))
    return pl.pallas_call(
        functools.partial(_moba_select_kernel, n_blk=n_blk),
        out_shape=jax.ShapeDtypeStruct((n_batch, N_HEADS, n_blk, seq), F32),
        grid=(n_batch, N_HEADS),
        in_specs=[pl.BlockSpec(memory_space=pltpu.SMEM), spec(k), spec(q)],
        out_specs=pl.BlockSpec((1, 1, n_blk, seq), lambda b, h: (b, h, 0, 0)),
        compiler_params=_cparams(("parallel", "parallel")),
        name="moba_select",
    )(rel_bias, k[0], q[0])


def _log_sigmoid(x):
    return -(jnp.maximum(-x, 0.0) + jnp.log1p(jnp.exp(-jnp.abs(x))))


def _split3(x):
    hi = x.astype(BF16).astype(F32)
    r1 = x - hi
    mid = r1.astype(BF16).astype(F32)
    lo = (r1 - mid).astype(BF16).astype(F32)
    return hi, mid, lo


def _fox_prep_kernel(f_ref, bf_ref, lf_ref, kx_ref, qx_ref, carry_ref, *, blk):
    i = pl.program_id(1)

    @pl.when(i == 0)
    def _():
        carry_ref[...] = jnp.zeros_like(carry_ref)

    lane = lax.broadcasted_iota(jnp.int32, (blk, LANE), 1)
    lf = jnp.where(lane < N_HEADS, _log_sigmoid(f_ref[0] + bf_ref[...]), 0.0)
    lf_ref[...] = lf
    row = lax.broadcasted_iota(jnp.int32, (blk, blk), 0)
    col = lax.broadcasted_iota(jnp.int32, (blk, blk), 1)
    tril = (row >= col).astype(BF16)
    cum_all = carry_ref[0:1, :]
    for piece in _split3(lf):
        cum_all = cum_all + jnp.dot(tril, piece.astype(BF16), preferred_element_type=F32)
    carry_ref[0:1, :] = cum_all[blk - 1:blk, :]
    for h in range(N_HEADS):
        cum = jnp.broadcast_to(cum_all[:, h:h + 1], (blk, LANE))
        hi, mid, lo = _split3(cum * LOG2E)
        kx_ref[0, h] = jnp.where(lane == 0, -hi, jnp.where(lane == 1, -mid, jnp.where(
            lane == 2, -lo, jnp.where(lane < 6, 1.0, 0.0)))).astype(BF16)
        qx_ref[0, h] = jnp.where(lane == 3, hi, jnp.where(lane == 4, mid, jnp.where(
            lane == 5, lo, jnp.where(lane < 3, 1.0, 0.0)))).astype(BF16)


def fox_prep(zc, bf_row, n_batch, seq, blk=256):
    n_i = seq // blk
    xshape = jax.ShapeDtypeStruct((n_batch, N_HEADS, seq, LANE), BF16)
    xspec = pl.BlockSpec((1, N_HEADS, blk, LANE), lambda b, i: (b, 0, i, 0))
    return pl.pallas_call(
        functools.partial(_fox_prep_kernel, blk=blk),
        out_shape=(jax.ShapeDtypeStruct((n_batch * seq, LANE), F32), xshape, xshape),
        grid=(n_batch, n_i),
        in_specs=[pl.BlockSpec((1, blk, LANE), lambda b, i: (F_SEG, b * n_i + i, F_LANEBLK)),
                  pl.BlockSpec((1, LANE), lambda b, i: (0, 0))],
        out_specs=(pl.BlockSpec((blk, LANE), lambda b, i: (b * n_i + i, 0)), xspec, xspec),
        scratch_shapes=[pltpu.VMEM((SUBLANE, LANE), F32)],
        compiler_params=_cparams(("parallel", "arbitrary")),
        name="fox_prep",
    )(zc, bf_row)


HEADS_PER_STEP = 6
HEAD_GROUP = 3


def _attn_tiles(qs, k_ref, kx_ref, v_ref, b0, nb, bias_fn, state):
    m_ref, l_ref, acc_ref = state
    start = pl.multiple_of(b0 * MOBA_BLOCK, MOBA_BLOCK)
    tk = nb * MOBA_BLOCK
    hp = len(qs)
    scores = {}

    def logits(heads):
        for i in heads:
            kt = k_ref[0, 0, i, pl.ds(start, tk), :]
            if kx_ref is not None:
                kt = jnp.concatenate([kt, kx_ref[0, i, pl.ds(start, tk), :]], axis=1)
            scores[i] = bias_fn(lax.dot_general(kt, qs[i], NT_DIMS, preferred_element_type=F32), i, b0)

    def update(heads):
        probs = {}
        for i in heads:
            m = m_ref[i]
            m_new = jnp.maximum(m, jnp.max(scores[i], axis=0, keepdims=True))
            alpha = jnp.exp2(m - m_new)
            p = jnp.exp2(scores.pop(i) - m_new)
            m_ref[i] = m_new
            l_ref[i] = alpha * l_ref[i] + jnp.sum(p, axis=0, keepdims=True)
            probs[i] = (p.astype(BF16), alpha)
        for i in heads:
            p, alpha = probs[i]
            vt = v_ref[0, 0, i, pl.ds(start, tk), :]
            pv = lax.dot_general(vt, p, TN_DIMS, preferred_element_type=F32)
            acc_ref[i] = alpha * acc_ref[i] + pv

    groups = [range(i, min(i + HEAD_GROUP, hp)) for i in range(0, hp, HEAD_GROUP)]
    logits(groups[0])
    for gi, heads in enumerate(groups):
        if gi + 1 < len(groups):
            logits(groups[gi + 1])
        update(heads)


def _attn_state_shapes(tq, hp):
    return [pltpu.VMEM((hp, 1, tq), F32), pltpu.VMEM((hp, 1, tq), F32), pltpu.VMEM((hp, HEAD_DIM, tq), F32)]


def _attn_init(state):
    m_ref, l_ref, acc_ref = state
    m_ref[...] = jnp.full(m_ref.shape, NEG, F32)
    l_ref[...] = jnp.zeros(l_ref.shape, F32)
    acc_ref[...] = jnp.zeros(acc_ref.shape, F32)


def _attn_finish(o_ref, state):
    m_ref, l_ref, acc_ref = state
    for i in range(acc_ref.shape[0]):
        o_ref[0, i] = (acc_ref[i] / l_ref[i]).T


PAST_BLOCKS = 4


def _attn_drive(j, tiles):
    jf = jnp.maximum(j - 1, 0)

    @pl.loop(0, jf // PAST_BLOCKS)
    def _(n):
        tiles(n * PAST_BLOCKS, PAST_BLOCKS, False)

    nb = PAST_BLOCKS // 2
    while nb >= 1:
        @pl.when(jf % (2 * nb) >= nb)
        def _(nb=nb):
            tiles((jf // (2 * nb)) * (2 * nb), nb, False)
        nb //= 2

    @pl.when(j > 0)
    def _():
        tiles(j - 1, 2, True)

    @pl.when(j == 0)
    def _():
        tiles(0, 1, True)


def _fox_prompt_kernel(q_ref, qx_ref, k_ref, kx_ref, v_ref, o_ref, *state, tq, hp):
    j = pl.program_id(2)
    qs = [jnp.concatenate([q_ref[0, 0, i], qx_ref[0, i]], axis=1) for i in range(hp)]

    def tiles(b0, nb, last):
        def bias(s, i, b0):
            if not last:
                return s
            key = lax.broadcasted_iota(jnp.int32, s.shape, 0) - (nb - 1) * tq
            qry = lax.broadcasted_iota(jnp.int32, s.shape, 1)
            return jnp.where(key <= qry, s, NEG)
        _attn_tiles(qs, k_ref, kx_ref, v_ref, b0, nb, bias, state)

    _attn_init(state)
    _attn_drive(j, tiles)
    _attn_finish(o_ref, state)


def _head_spec(seg, hp, rows, row_map):
    return pl.BlockSpec((1, 1, hp, rows, HEAD_DIM), lambda b, g, j: (seg, b, g, row_map(j), 0))


def fox_prompt(q, k, v, kx, qx, n_batch, seq, tq=256, hp=HEADS_PER_STEP):
    nq = seq // tq
    return pl.pallas_call(
        functools.partial(_fox_prompt_kernel, tq=tq, hp=hp),
        out_shape=jax.ShapeDtypeStruct((n_batch, N_HEADS, seq, HEAD_DIM), F32),
        grid=(n_batch, N_HEADS // hp, nq),
        in_specs=[_head_spec(q[1], hp, tq, lambda j: j),
                  pl.BlockSpec((1, hp, tq, LANE), lambda b, g, j: (b, g, j, 0)),
                  _head_spec(k[1], hp, seq, lambda j: 0),
                  pl.BlockSpec((1, hp, seq, LANE), lambda b, g, j: (b, g, 0, 0)),
                  _head_spec(v[1], hp, seq, lambda j: 0)],
        out_specs=pl.BlockSpec((1, hp, tq, HEAD_DIM), lambda b, g, j: (b, g, j, 0)),
        scratch_shapes=_attn_state_shapes(tq, hp),
        compiler_params=_cparams(("parallel", "parallel", "arbitrary")),
        name="fox_prompt",
    )(q[0], qx, k[0], kx, v[0])


def _top_blocks(gate, n_valid, axis):
    n_blk = gate.shape[axis]
    blk = lax.broadcasted_iota(jnp.int32, gate.shape, axis)
    blk_f = blk.astype(F32)
    g = jnp.where(blk < n_valid, gate, -jnp.inf)
    sel = jnp.zeros(gate.shape, jnp.bool_)
    for _ in range(MOBA_TOPK):
        mx = jnp.max(g, axis=axis, keepdims=True)
        cand = (g == mx) & (mx > -jnp.inf)
        first = jnp.min(jnp.where(cand, blk_f, float(n_blk)), axis=axis, keepdims=True)
        pick = blk_f == first
        sel = sel | pick
        g = jnp.where(pick, -jnp.inf, g)
    return sel


def _moba_prompt_kernel(rb_ref, q_ref, k_ref, v_ref, mb_ref, tab_ref, o_ref, *state, tq, hp):
    g = pl.program_id(1)
    j = pl.program_id(2)
    qs = [q_ref[0, 0, i] for i in range(hp)]
    fars = [rb_ref[N_BUCKETS - 1, g * hp + i] * LOG2E for i in range(hp)]

    def tiles(b0, nb, last):
        def bias(s, i, b0):
            rows = lambda r: s[r * tq:(r + 1) * tq]
            if not last:
                return jnp.concatenate([rows(r) + mb_ref[0, i, pl.ds(b0 + r, 1), :] for r in range(nb)], axis=0)
            own = rows(nb - 1) + tab_ref[i, 0]
            if nb == 1:
                return own
            prev = rows(0) + tab_ref[i, 1] + (mb_ref[0, i, pl.ds(b0, 1), :] - fars[i])
            return jnp.concatenate([prev, own], axis=0)
        _attn_tiles(qs, k_ref, None, v_ref, b0, nb, bias, state)

    _attn_init(state)
    _attn_drive(j, tiles)
    _attn_finish(o_ref, state)


def moba_prompt(q, k, v, mb, tab, rel_bias, n_batch, seq, hp=HEADS_PER_STEP):
    tq = MOBA_BLOCK
    nq = seq // tq
    return pl.pallas_call(
        functools.partial(_moba_prompt_kernel, tq=tq, hp=hp),
        out_shape=jax.ShapeDtypeStruct((n_batch, N_HEADS, seq, HEAD_DIM), F32),
        grid=(n_batch, N_HEADS // hp, nq),
        in_specs=[pl.BlockSpec(memory_space=pltpu.SMEM),
                  _head_spec(q[1], hp, tq, lambda j: j),
                  _head_spec(k[1], hp, seq, lambda j: 0),
                  _head_spec(v[1], hp, seq, lambda j: 0),
                  pl.BlockSpec((1, hp, nq, tq), lambda b, g, j: (b, g, 0, j)),
                  pl.BlockSpec((hp, 2, tq, tq), lambda b, g, j: (g, 0, 0, 0))],
        out_specs=pl.BlockSpec((1, hp, tq, HEAD_DIM), lambda b, g, j: (b, g, j, 0)),
        scratch_shapes=_attn_state_shapes(tq, hp),
        compiler_params=_cparams(("parallel", "parallel", "arbitrary")),
        name="moba_prompt",
    )(rel_bias, q[0], k[0], v[0], mb, tab)


def _pad_rows(x, rows):
    return jnp.concatenate([x, jnp.zeros((rows - x.shape[0], x.shape[1]), x.dtype)], axis=0)


def _lanes(a, b):
    return jnp.concatenate([a, b], axis=1)


def _rows(a, b):
    return jnp.concatenate([a, b], axis=0)


def _pair_queries(q_ref, g, scale):
    q0, q1 = q_ref[0, 0, 2 * g], q_ref[0, 0, 2 * g + 1]
    z = jnp.zeros_like(q0)
    return _rows(_lanes(q0, z), _lanes(z, q1)) * scale


def _pair_tile(pages, p0, g):
    t = lambda p: _lanes(pages[p][0, 0, 2 * g].astype(BF16), pages[p][0, 0, 2 * g + 1].astype(BF16))
    return _rows(t(p0), t(p0 + 1))


def _pair_new(x_ref, g, page):
    return _lanes(_pad_rows(x_ref[0, 0, 2 * g], page), _pad_rows(x_ref[0, 0, 2 * g + 1], page)).astype(BF16)


def _pair_bias(top, bottom, t_new):
    return _rows(jnp.broadcast_to(top, (t_new, top.shape[1])), jnp.broadcast_to(bottom, (t_new, bottom.shape[1])))


def _pair_probs(s_list, s_new):
    m = s_list[0]
    for s in s_list[1:]:
        m = jnp.maximum(m, s)
    m = jnp.max(jnp.maximum(jnp.maximum(m[:, :LANE], m[:, LANE:]), s_new), axis=1, keepdims=True)
    l = None
    p_list = []
    for s in s_list:
        p = jnp.exp(s - m)
        l = p if l is None else l + p
        p_list.append(p.astype(BF16))
    p = jnp.exp(s_new - m)
    l = jnp.sum(l[:, :LANE] + l[:, LANE:] + p, axis=1, keepdims=True)
    return p_list, p.astype(BF16), l


def _pair_pv(o_ref, g, probs, v_pages, v_ref, t_new, page):
    p_list, p_new, l = probs
    acc = jnp.dot(p_new, _pair_new(v_ref, g, page), preferred_element_type=F32)
    for n, p in enumerate(p_list):
        acc = acc + jnp.dot(p, _pair_tile(v_pages, 2 * n, g), preferred_element_type=F32)
    out = acc / l
    o_ref[0, 2 * g] = out[:t_new, :HEAD_DIM]
    o_ref[0, 2 * g + 1] = out[t_new:, HEAD_DIM:]


def _fox_sample_kernel(pt_ref, q_ref, k_ref, v_ref, f_ref, bf_ref, *rest, t_new, n_pages, page):
    k_pages = rest[:n_pages]
    v_pages = rest[n_pages:2 * n_pages]
    lf_pages = rest[2 * n_pages:3 * n_pages]
    o_ref, lf_ref = rest[3 * n_pages:]

    n_r = n_pages * SUBLANE
    lane = lax.broadcasted_iota(jnp.int32, (n_r, page), 1)
    srow = lax.broadcasted_iota(jnp.int32, (n_r, page), 0)
    lfc = jnp.concatenate([r[0, 0] for r in lf_pages], axis=0)
    incl = lfc
    step = 1
    while step < page:
        incl = incl + jnp.where(lane + step < page, pltpu.roll(incl, page - step, axis=1), 0.0)
        step *= 2
    tot = jnp.broadcast_to(jnp.sum(jnp.where(lane == 0, incl, 0.0), axis=1, keepdims=True), (n_r, page))
    run = tot
    step = SUBLANE
    while step < n_r:
        run = run + jnp.where(srow + step < n_r, pltpu.roll(run, n_r - step, axis=0), 0.0)
        step *= 2
    suf = (incl - lfc) + (run - tot)

    flane = lax.broadcasted_iota(jnp.int32, (t_new, LANE), 1)
    lf_new = jnp.where(flane < N_HEADS, _log_sigmoid(f_ref[0] + bf_ref[...]), 0.0)
    lf_ref[...] = lf_new
    cum = _pad_rows(lf_new, LANE).T
    lane2 = lax.broadcasted_iota(jnp.int32, (LANE, LANE), 1)
    step = 1
    while step < t_new:
        cum = cum + jnp.where(lane2 >= step, pltpu.roll(cum, step, axis=1), 0.0)
        step *= 2
    r = lax.broadcasted_iota(jnp.int32, (2 * t_new, page), 0) % t_new
    c = lax.broadcasted_iota(jnp.int32, (2 * t_new, page), 1)
    suf_row = lambda p, h: suf[p * SUBLANE + h:p * SUBLANE + h + 1, :]

    pairs = range(N_HEADS // 2)
    logits = []
    for g in pairs:
        h0, h1 = 2 * g, 2 * g + 1
        qp = _pair_queries(q_ref, g, SCALE).astype(BF16)
        s_list = []
        for p0 in range(0, n_pages, 2):
            s = lax.dot_general(qp, _pair_tile(k_pages, p0, g), NT_DIMS, preferred_element_type=F32)
            bias = _pair_bias(_lanes(suf_row(p0, h0), suf_row(p0 + 1, h0)),
                              _lanes(suf_row(p0, h1), suf_row(p0 + 1, h1)), t_new)
            s_list.append(s + bias)
        s_new = lax.dot_general(qp, _pair_new(k_ref, g, page), NT_DIMS, preferred_element_type=F32)
        s_new = jnp.where(c <= r, s_new - _pair_bias(cum[h0:h0 + 1, :], cum[h1:h1 + 1, :], t_new), NEG)
        logits.append((s_list, s_new))
    probs = [_pair_probs(*logits[g]) for g in pairs]
    for g in pairs:
        _pair_pv(o_ref, g, probs[g], v_pages, v_ref, t_new, page)


def _page_specs(n_pages, layer, shape, b0):
    def spec(p):
        return pl.BlockSpec((1, 1) + shape, lambda b, pt: (layer, pt[b0 + b, p]) + (0,) * len(shape))
    return [spec(p) for p in range(n_pages)]


def _new_spec(seg, t_new, b0):
    return pl.BlockSpec((1, 1, N_HEADS, t_new, HEAD_DIM), lambda b, pt: (seg, b0 + b, 0, 0, 0))


def fox_sample(q, k, v, zc, bf_row, pool_k, pool_v, pool_lft, page_table, layer, t_new, b0, n):
    n_pages = page_table.shape[1]
    page = pool_k.shape[3]
    in_specs = ([_new_spec(q[1], t_new, b0), _new_spec(k[1], t_new, b0), _new_spec(v[1], t_new, b0),
                 pl.BlockSpec((1, t_new, LANE), lambda b, pt: (F_SEG, b0 + b, F_LANEBLK)),
                 pl.BlockSpec((1, LANE), lambda b, pt: (0, 0))]
                + _page_specs(n_pages, layer, (N_HEADS, page, HEAD_DIM), b0)
                + _page_specs(n_pages, layer, (N_HEADS, page, HEAD_DIM), b0)
                + _page_specs(n_pages, layer, (SUBLANE, page), b0))
    return _Call(
        "fox_sample", functools.partial(_fox_sample_kernel, t_new=t_new, n_pages=n_pages, page=page), (n,), in_specs,
        [pl.BlockSpec((1, N_HEADS, t_new, HEAD_DIM), lambda b, pt: (b, 0, 0, 0)),
         pl.BlockSpec((t_new, LANE), lambda b, pt: (b, 0))],
        [jax.ShapeDtypeStruct((n, N_HEADS, t_new, HEAD_DIM), F32), jax.ShapeDtypeStruct((n * t_new, LANE), F32)],
        [q[0], k[0], v[0], zc, bf_row] + [pool_k] * n_pages + [pool_v] * n_pages + [pool_lft] * n_pages, {},
        page_table)


def _moba_sample_kernel(pt_ref, q_ref, k_ref, v_ref, tab_ref, *rest, t_new, n_pages, page):
    k_pages = rest[:n_pages]
    v_pages = rest[n_pages:2 * n_pages]
    o_ref = rest[2 * n_pages]
    assert MOBA_BLOCK == 2 * page
    n_blk = n_pages // 2

    def block_means(h):
        sums = [jnp.sum(k_pages[2 * n][0, 0, h] + k_pages[2 * n + 1][0, 0, h], axis=0, keepdims=True)
                for n in range(n_blk)]
        return _pad_rows(jnp.concatenate(sums, axis=0) * (1.0 / MOBA_BLOCK), LANE)

    pairs = range(N_HEADS // 2)
    logits = []
    for g in pairs:
        q32 = _pair_queries(q_ref, g, 1.0)
        qp = (q32 * SCALE).astype(BF16)
        kmean = _lanes(block_means(2 * g), block_means(2 * g + 1))
        gate = lax.dot_general(q32, kmean, NT_DIMS, precision=HIGHEST, preferred_element_type=F32)
        sel = _top_blocks(gate, n_blk, 1)
        mask = jnp.where(sel, 0.0, NEG)
        rows = slice(2 * g * t_new, (2 * g + 2) * t_new)
        far = tab_ref[2, rows, :]
        s_list = []
        for n in range(n_blk):
            s = lax.dot_general(qp, _pair_tile(k_pages, 2 * n, g), NT_DIMS, preferred_element_type=F32)
            bias = _lanes(far, tab_ref[0, rows, :] if n == n_blk - 1 else far)
            s_list.append(s + bias + mask[:, n:n + 1])
        s_new = lax.dot_general(qp, _pair_new(k_ref, g, page), NT_DIMS, preferred_element_type=F32)
        logits.append((s_list, s_new + tab_ref[1, rows, :]))
    probs = [_pair_probs(*logits[g]) for g in pairs]
    for g in pairs:
        _pair_pv(o_ref, g, probs[g], v_pages, v_ref, t_new, page)


def moba_sample(q, k, v, tab, pool_k, pool_v, page_table, layer, t_new, b0, n):
    n_pages = page_table.shape[1]
    page = pool_k.shape[3]
    rows = N_HEADS * t_new
    in_specs = ([_new_spec(q[1], t_new, b0), _new_spec(k[1], t_new, b0), _new_spec(v[1], t_new, b0),
                 pl.BlockSpec((3, rows, LANE), lambda b, pt: (0, 0, 0))]
                + _page_specs(n_pages, layer, (N_HEADS, page, HEAD_DIM), b0)
                + _page_specs(n_pages, layer, (N_HEADS, page, HEAD_DIM), b0))
    return _Call(
        "moba_sample", functools.partial(_moba_sample_kernel, t_new=t_new, n_pages=n_pages, page=page), (n,),
        in_specs, [pl.BlockSpec((1, N_HEADS, t_new, HEAD_DIM), lambda b, pt: (b, 0, 0, 0))],
        [jax.ShapeDtypeStruct((n, N_HEADS, t_new, HEAD_DIM), F32)],
        [q[0], k[0], v[0], tab] + [pool_k] * n_pages + [pool_v] * n_pages, {}, page_table)


def _silu(g):
    return g / (1.0 + jnp.exp(-g))


def _token_major(v, tm):
    return jnp.concatenate([v[:, h].reshape(tm, HEAD_DIM) for h in range(N_HEADS)], axis=1)


def _combine_kernel(*refs, tm, seq_tiles, seq_rows, alpha):
    (x_ref, oa_ref, of_ref, ga_ref, gf_ref, b0, b1, c0, c1, h0, h1, g0, g1) = refs[:13]
    if seq_rows is None:
        pc0, pc1, ph0, ph1 = refs[13:17]
        rest = refs[17:]
    else:
        hist1_ref, hist2_ref = refs[13:15]
        rest = refs[15:]
    wout_ref, cw_ref, lng_ref, lnb_ref, y_ref, yb_ref, u_ref = rest
    cat = lambda a, b: jnp.concatenate([a[0], b[0]], axis=1)
    u = cat(c0, c1) * cat(h0, h1)
    if seq_rows is None:
        keep = (pl.program_id(0) % seq_tiles != 0).astype(F32)
        uprev = cat(pc0, pc1) * cat(ph0, ph1) * keep
        ext = jnp.concatenate([uprev, u], axis=0)
        u1 = pltpu.roll(ext, 1, axis=0)[SUBLANE:]
        u2 = pltpu.roll(ext, 2, axis=0)[SUBLANE:]
        u_ref[0] = u[tm - SUBLANE:, :]
    else:
        r = lax.broadcasted_iota(jnp.int32, (tm, D_CONV), 0) % seq_rows
        u1 = jnp.where(r >= 1, pltpu.roll(u, 1, axis=0), hist1_ref[...])
        u2 = jnp.where(r >= 2, pltpu.roll(u, 2, axis=0), hist2_ref[...])
        u_ref[...] = u
    conv = cw_ref[0:1, :] * u2 + cw_ref[1:2, :] * u1 + cw_ref[2:3, :] * u
    oc = (cat(b0, b1) * conv * _silu(cat(g0, g1))).astype(BF16)
    oa = (_token_major(oa_ref[...], tm) * _silu(_token_major(ga_ref[0], tm))).astype(BF16)
    of = (_token_major(of_ref[...], tm) * _silu(_token_major(gf_ref[0], tm))).astype(BF16)
    proj = (jnp.dot(oa, wout_ref[0:D_ATT, :], preferred_element_type=F32)
            + jnp.dot(of, wout_ref[D_ATT:2 * D_ATT, :], preferred_element_type=F32)
            + jnp.dot(oc, wout_ref[2 * D_ATT:, :], preferred_element_type=F32))
    res = alpha * x_ref[...] + proj
    mu = jnp.mean(res, axis=-1, keepdims=True)
    cen = res - mu
    var = jnp.mean(cen * cen, axis=-1, keepdims=True)
    y = cen * lax.rsqrt(var + LN_EPS) * lng_ref[...] + lnb_ref[...]
    y_ref[...] = y
    yb_ref[...] = y.astype(BF16)


def combine(x, oa, of, zg, zc, wout, cw, lng, lnb, alpha, tm, hist=None):
    t = x.shape[0]
    n_seq, _, seq_len, _ = oa.shape
    assert t == n_seq * seq_len and t % tm == 0
    if seq_len >= tm:
        assert seq_len % tm == 0
        sb, sl, per_seq = 1, tm, seq_len // tm
        hmap = lambda i: (i // per_seq, 0, i % per_seq, 0)
    else:
        assert tm % seq_len == 0
        sb, sl, per_seq = tm // seq_len, seq_len, 1
        hmap = lambda i: (i, 0, 0, 0)
    hblk = (sb, N_HEADS, sl, HEAD_DIM)
    zblk = lambda sc: pl.BlockSpec((1, tm, CBLK), lambda i: (sc[0], i, sc[1]))
    pieces = [sc for idx in (0, 1, 2, 3) for sc in _conv_piece(idx)]
    in_specs = [pl.BlockSpec((tm, D_MODEL), lambda i: (i, 0)),
                pl.BlockSpec(hblk, hmap),
                pl.BlockSpec(hblk, hmap),
                pl.BlockSpec((1,) + hblk, lambda i: (0,) + hmap(i)),
                pl.BlockSpec((1,) + hblk, lambda i: (1,) + hmap(i))] + [zblk(sc) for sc in pieces]
    args = [x, oa, of, zg, zg] + [zc] * 8
    if hist is None:
        seq_rows = None
        tb = tm // SUBLANE
        prev = lambda sc: pl.BlockSpec((1, SUBLANE, CBLK), lambda i: (sc[0], jnp.maximum(i * tb - 1, 0), sc[1]))
        in_specs += [prev(sc) for sc in pieces[2:6]]
        args += [zc] * 4
        u_shape = jax.ShapeDtypeStruct((n_seq, SUBLANE, D_CONV), F32)
        u_spec = pl.BlockSpec((1, SUBLANE, D_CONV), lambda i: (i // per_seq, 0, 0))
    else:
        seq_rows = seq_len
        in_specs += [pl.BlockSpec((tm, D_CONV), lambda i: (i, 0))] * 2
        args += list(hist)
        u_shape = jax.ShapeDtypeStruct((t, D_CONV), F32)
        u_spec = pl.BlockSpec((tm, D_CONV), lambda i: (i, 0))
    in_specs += [pl.BlockSpec((D_MODEL, D_MODEL), lambda i: (0, 0)),
                 pl.BlockSpec((SUBLANE, D_CONV), lambda i: (0, 0)),
                 pl.BlockSpec((1, D_MODEL), lambda i: (0, 0)),
                 pl.BlockSpec((1, D_MODEL), lambda i: (0, 0))]
    args += [wout, cw, lng, lnb]
    return pl.pallas_call(
        functools.partial(_combine_kernel, tm=tm, seq_tiles=per_seq, seq_rows=seq_rows, alpha=alpha),
        out_shape=(jax.ShapeDtypeStruct((t, D_MODEL), F32), jax.ShapeDtypeStruct((t, D_MODEL), BF16), u_shape),
        grid=(t // tm,),
        in_specs=in_specs,
        out_specs=(pl.BlockSpec((tm, D_MODEL), lambda i: (i, 0)), pl.BlockSpec((tm, D_MODEL), lambda i: (i, 0)),
                   u_spec),
        compiler_params=_cparams(("arbitrary",)),
        name="combine_prompt" if hist is None else "combine_sample",
    )(*args)


def kernel(x_prompt, x_sample, cache_moba_k, cache_moba_v, cache_fox_k, cache_fox_v, cache_fox_logf,
           state_conv, page_table, rel_bias, w_in, b_f, conv_w, w_out, ln_g, ln_b):
    depth = w_in.shape[0]
    alpha = (2 * depth) ** 0.25
    n_batch, seq, _ = x_prompt.shape
    n_seq, t_new, _ = x_sample.shape
    n_phys, page = cache_moba_k.shape[1:3]
    past = page_table.shape[1] * page
    assert page == LANE and past % MOBA_BLOCK == 0 and t_new == SUBLANE and seq % MOBA_BLOCK == 0
    assert past >= MAX_DISTANCE and w_in.shape[2] == N_MAIN + N_HEADS + 4 * D_CONV

    xp = x_prompt.reshape(n_batch * seq, D_MODEL)
    xs = x_sample.reshape(n_seq * t_new, D_MODEL)
    xpb, xsb = xp.astype(BF16), xs.astype(BF16)
    pool = lambda c: jnp.swapaxes(c, 2, 3)
    pk_a, pv_a, pk_f, pv_f = pool(cache_moba_k), pool(cache_moba_v), pool(cache_fox_k), pool(cache_fox_v)
    pool_lft = jnp.pad(jnp.swapaxes(cache_fox_logf, 2, 3), ((0, 0), (0, 0), (0, SUBLANE - N_HEADS), (0, 0)))
    tab_p = prompt_bias_tables(rel_bias)
    tab_s = sample_bias_tables(rel_bias, t_new, past)
    tm_s = min(256, n_seq * t_new)

    outs = {k: [] for k in ("p_lf", "p_cv", "s_lf", "s_cv")}
    kv_segs = {"ka": SEG_KA, "va": SEG_VA, "kf": SEG_KF, "vf": SEG_VF}
    p_kv = dict.fromkeys(kv_segs)
    s_kv = dict.fromkeys(kv_segs)
    for layer in range(depth):
        wt = jnp.swapaxes(w_in[layer], 0, 1)
        n_f = N_MAIN + N_HEADS
        wt_h = wt[:N_MAIN].astype(BF16)
        wt_c = jnp.concatenate(
            [wt[n_f:], wt[N_MAIN:n_f],
             jnp.zeros((N_CONV_SEG * SEG - 4 * D_CONV - N_HEADS, D_MODEL), wt.dtype)], axis=0).astype(BF16)
        wo = w_out[layer].astype(BF16)
        bf_row = jnp.pad(b_f[layer], (0, LANE - N_HEADS)).reshape(1, LANE)
        cw = jnp.pad(conv_w[layer], ((0, SUBLANE - CONV_WIDTH), (0, 0)))
        lng, lnb = ln_g[layer].reshape(1, D_MODEL), ln_b[layer].reshape(1, D_MODEL)

        proj = functools.partial(inproj_heads, xsb, wt_h, n_seq=n_seq, seq_len=t_new, tm=n_seq * t_new)
        (zqs,) = run(proj((SEG_QA, SEG_QF)))
        (zgs,) = run(proj((SEG_GA, SEG_GF)))
        for name, seg in kv_segs.items():
            (s_kv[name],) = run(proj((seg,), stack=(depth, layer, s_kv[name])))
        (zcs,) = run(inproj_conv(xsb, wt_c, tm=n_seq * t_new))

        sample = {
            "fox": functools.partial(fox_sample, (zqs, 1), (s_kv["kf"], layer), (s_kv["vf"], layer), zcs, bf_row,
                                     pk_f, pv_f, pool_lft, page_table, layer, t_new),
            "moba": functools.partial(moba_sample, (zqs, 0), (s_kv["ka"], layer), (s_kv["va"], layer), tab_s,
                                      pk_a, pv_a, page_table, layer, t_new)}
        next_seq = {"fox": 0, "moba": 0}
        chunks = {"fox": [], "moba": []}

        def run_with_sample(call):
            steps = math.prod(call.grid)
            for kind in sample:
                if n_seq - next_seq[kind] >= steps:
                    outs_a, outs_b = run_zipped(call, sample[kind](next_seq[kind], steps))
                    next_seq[kind] += steps
                    chunks[kind].append(outs_b)
                    return outs_a
            return run(call)

        proj = functools.partial(inproj_heads, xpb, wt_h, n_seq=n_batch, seq_len=seq, tm=512)
        zq, zqb = run_with_sample(proj((SEG_QA, SEG_QF), bf16_scale=QSCALE))
        (zg,) = run_with_sample(proj((SEG_GA, SEG_GF)))
        kvb = {}
        for name, seg in kv_segs.items():
            p_kv[name], kvb[name] = run_with_sample(proj((seg,), bf16_scale=1.0, stack=(depth, layer, p_kv[name])))
        (zc,) = run_with_sample(inproj_conv(xpb, wt_c, tm=512))
        for kind in sample:
            if next_seq[kind] < n_seq:
                chunks[kind].append(run(sample[kind](next_seq[kind], n_seq - next_seq[kind])))
        ofs = jnp.concatenate([c[0] for c in chunks["fox"]], axis=0)
        lfs = jnp.concatenate([c[1] for c in chunks["fox"]], axis=0)
        oas = jnp.concatenate([c[0] for c in chunks["moba"]], axis=0)

        mb = moba_select((p_kv["ka"], layer), (zq, 0), rel_bias, n_batch, seq)
        oa = moba_prompt((zqb, 0), (kvb["ka"], 0), (kvb["va"], 0), mb, tab_p, rel_bias, n_batch, seq)
        lf, kx, qx = fox_prep(zc, bf_row, n_batch, seq)
        of = fox_prompt((zqb, 1), (kvb["kf"], 0), (kvb["vf"], 0), kx, qx, n_batch, seq)
        xp, xpb, ulast = combine(xp, oa, of, zg, zc, wo, cw, lng, lnb, alpha, tm=256)
        outs["p_lf"].append(lf[:, :N_HEADS].reshape(n_batch, seq, N_HEADS))
        outs["p_cv"].append(ulast[:, SUBLANE - (CONV_WIDTH - 1):, :])

        st = state_conv[layer]
        zero = jnp.zeros((n_seq, t_new, D_CONV), F32)
        hist1 = zero.at[:, 0].set(st[:, 1]).reshape(n_seq * t_new, D_CONV)
        hist2 = zero.at[:, 0].set(st[:, 0]).at[:, 1].set(st[:, 1]).reshape(n_seq * t_new, D_CONV)
        xs, xsb, us = combine(xs, oas, ofs, zgs, zcs, wo, cw, lng, lnb, alpha, tm=tm_s, hist=(hist1, hist2))
        outs["s_lf"].append(lfs[:, :N_HEADS].reshape(n_seq, t_new, N_HEADS))
        outs["s_cv"].append(us.reshape(n_seq, t_new, D_CONV)[:, t_new - (CONV_WIDTH - 1):, :])

    st = {k: jnp.stack(v) for k, v in outs.items()}
    tok_major = lambda a: jnp.swapaxes(a, 2, 3)
    return (xp.reshape(n_batch, seq, D_MODEL), xs.reshape(n_seq, t_new, D_MODEL),
            tok_major(p_kv["ka"]), tok_major(p_kv["va"]), tok_major(p_kv["kf"]), tok_major(p_kv["vf"]),
            st["p_lf"], st["p_cv"],
            tok_major(s_kv["ka"]), tok_major(s_kv["va"]), tok_major(s_kv["kf"]), tok_major(s_kv["vf"]),
            st["s_lf"], st["s_cv"])
```

```python
import functools
import math
from typing import Callable, NamedTuple

import jax
import jax.numpy as jnp
from jax import lax
from jax.experimental import pallas as pl
from jax.experimental.pallas import tpu as pltpu

F32 = jnp.float32
BF16 = jnp.bfloat16
HIGHEST = lax.Precision.HIGHEST

D_MODEL = 2048
HEAD_DIM = 128
N_HEADS = 6
D_ATT = N_HEADS * HEAD_DIM
D_CONV = 512
CONV_WIDTH = 3
MOBA_BLOCK = 256
MOBA_TOPK = 3
N_BUCKETS = 32
MAX_DISTANCE = 128
LN_EPS = 1e-5
SCALE = HEAD_DIM ** -0.5
LOG2E = math.log2(math.e)
QSCALE = SCALE * LOG2E

SEG = 768
SEG_QA, SEG_KA, SEG_VA, SEG_GA, SEG_QF, SEG_KF, SEG_VF, SEG_GF = range(8)
N_HEAD_SEG = 8
N_CONV_SEG = 3
N_MAIN = N_HEAD_SEG * SEG
CBLK = 256
LANE = 128
SUBLANE = 8
F_SEG, F_LANEBLK = (4 * D_CONV) // SEG, ((4 * D_CONV) % SEG) // LANE
NEG = -1e30
VMEM_LIMIT = 56 * 1024 * 1024

NT_DIMS = (((1,), (1,)), ((), ()))
TN_DIMS = (((0,), (0,)), ((), ()))


def _cparams(sem):
    return pltpu.CompilerParams(dimension_semantics=sem, vmem_limit_bytes=VMEM_LIMIT)


class _Call(NamedTuple):
    name: str
    body: Callable
    grid: tuple
    in_specs: list
    out_specs: list
    out_shape: list
    args: list
    aliases: dict
    page_table: object = None
    scratch: tuple = ()


def run(c):
    if c.page_table is None:
        body = lambda *refs: c.body(tuple(pl.program_id(d) for d in range(len(c.grid))), *refs)
        return pl.pallas_call(
            body, out_shape=tuple(c.out_shape), grid=c.grid, in_specs=c.in_specs, out_specs=tuple(c.out_specs),
            scratch_shapes=list(c.scratch), input_output_aliases=c.aliases,
            compiler_params=_cparams(("arbitrary",) * len(c.grid)), name=c.name,
        )(*c.args)
    assert not c.aliases and not c.scratch
    return pl.pallas_call(
        c.body, out_shape=tuple(c.out_shape),
        grid_spec=pltpu.PrefetchScalarGridSpec(num_scalar_prefetch=1, grid=c.grid, in_specs=c.in_specs,
                                               out_specs=tuple(c.out_specs)),
        compiler_params=_cparams(("arbitrary",) * len(c.grid)), name=c.name,
    )(c.page_table, *c.args)


def run_zipped(a, b):
    n = b.grid[0]
    assert a.page_table is None and b.page_table is not None and len(b.grid) == 1 and math.prod(a.grid) == n

    def a_idx(k):
        idx = []
        for size in reversed(a.grid):
            idx.append(k % size)
            k = k // size
        return tuple(reversed(idx))

    def respec(sp, fn):
        if sp.block_shape is None:
            return sp
        return pl.BlockSpec(sp.block_shape, fn(sp.index_map), memory_space=sp.memory_space,
                            pipeline_mode=sp.pipeline_mode)

    for_a = lambda m: (lambda k, pt: m(*a_idx(k)))
    for_b = lambda m: m
    na_in, nb_in, na_out = len(a.in_specs), len(b.in_specs), len(a.out_specs)

    nb_out = len(b.out_specs)

    def body(pt_ref, *refs):
        ins_a, refs = refs[:na_in], refs[na_in:]
        ins_b, refs = refs[:nb_in], refs[nb_in:]
        outs_a, refs = refs[:na_out], refs[na_out:]
        outs_b, scratch_a = refs[:nb_out], refs[nb_out:]
        a.body(a_idx(pl.program_id(0)), *ins_a, *outs_a, *scratch_a)
        b.body(pt_ref, *ins_b, *outs_b)

    outs = pl.pallas_call(
        body, out_shape=tuple(a.out_shape) + tuple(b.out_shape),
        grid_spec=pltpu.PrefetchScalarGridSpec(
            num_scalar_prefetch=1, grid=(n,),
            in_specs=[respec(sp, for_a) for sp in a.in_specs] + [respec(sp, for_b) for sp in b.in_specs],
            out_specs=tuple([respec(sp, for_a) for sp in a.out_specs] + [respec(sp, for_b) for sp in b.out_specs]),
            scratch_shapes=list(a.scratch)),
        input_output_aliases={1 + i: o for i, o in a.aliases.items()},
        compiler_params=_cparams(("arbitrary",)), name=a.name + "_zip_" + b.name,
    )(b.page_table, *a.args, *b.args)
    return outs[:na_out], outs[na_out:]


def _conv_piece(idx):
    out = []
    for half in range(2):
        col = idx * D_CONV + half * CBLK
        out.append((col // SEG, (col % SEG) // CBLK))
    return out


def _inproj_heads_kernel(idx, x_ref, w_ref, *refs, sb, sl, n_in, bf16_scale):
    o_ref = refs[n_in]
    res = lax.dot_general(x_ref[...], w_ref[...], NT_DIMS, preferred_element_type=F32)
    for h in range(N_HEADS):
        o_ref[0, :, h] = res[:, h * HEAD_DIM:(h + 1) * HEAD_DIM].reshape(sb, sl, HEAD_DIM)
    if bf16_scale is not None:
        scaled = (res * bf16_scale).astype(BF16)
        for h in range(N_HEADS):
            refs[n_in + 1][0, :, h] = scaled[:, h * HEAD_DIM:(h + 1) * HEAD_DIM].reshape(sb, sl, HEAD_DIM)


def inproj_heads(xb, wt, segs, n_seq, seq_len, tm, bf16_scale=None, stack=None):
    t = xb.shape[0]
    assert t == n_seq * seq_len and t % tm == 0 and wt.shape == (N_MAIN, D_MODEL)
    if seq_len >= tm:
        assert seq_len % tm == 0
        sb, sl, per_seq = 1, tm, seq_len // tm
        rows = lambda i: (i // per_seq, 0, i % per_seq, 0)
    else:
        assert tm % seq_len == 0
        sb, sl = tm // seq_len, seq_len
        rows = lambda i: (i, 0, 0, 0)
    blk = (1, sb, N_HEADS, sl, HEAD_DIM)
    tail = (n_seq, N_HEADS, seq_len, HEAD_DIM)

    def seg_of(j):
        s = segs[0]
        for idx in range(1, len(segs)):
            s = jnp.where(j == idx, segs[idx], s)
        return s

    in_specs = [pl.BlockSpec((tm, D_MODEL), lambda j, i: (i, 0)),
                pl.BlockSpec((SEG, D_MODEL), lambda j, i: (seg_of(j), 0))]
    args = [xb, wt]
    aliases = {}
    if stack is None:
        shapes = [jax.ShapeDtypeStruct((len(segs),) + tail, F32)]
        specs = [pl.BlockSpec(blk, lambda j, i: (j,) + rows(i))]
    else:
        depth, layer, prev = stack
        assert len(segs) == 1
        shapes = [jax.ShapeDtypeStruct((depth,) + tail, F32)]
        specs = [pl.BlockSpec(blk, lambda j, i: (layer,) + rows(i))]
        if prev is not None:
            in_specs.append(pl.BlockSpec(memory_space=pl.ANY))
            args.append(prev)
            aliases = {2: 0}
    if bf16_scale is not None:
        shapes.append(jax.ShapeDtypeStruct((len(segs),) + tail, BF16))
        specs.append(pl.BlockSpec(blk, lambda j, i: (j,) + rows(i)))
    body = functools.partial(_inproj_heads_kernel, sb=sb, sl=sl, n_in=len(args) - 2, bf16_scale=bf16_scale)
    return _Call("inproj_heads", body, (len(segs), t // tm), in_specs, specs, shapes, args, aliases)


def _inproj_conv_kernel(idx, x_ref, w_ref, o_ref):
    o_ref[0] = lax.dot_general(x_ref[...], w_ref[...], NT_DIMS, preferred_element_type=F32)


def inproj_conv(xb, wt, tm):
    t = xb.shape[0]
    assert t % tm == 0 and wt.shape == (N_CONV_SEG * SEG, D_MODEL)
    return _Call("inproj_conv", _inproj_conv_kernel, (N_CONV_SEG, t // tm),
                 [pl.BlockSpec((tm, D_MODEL), lambda j, i: (i, 0)),
                  pl.BlockSpec((SEG, D_MODEL), lambda j, i: (j, 0))],
                 [pl.BlockSpec((1, tm, SEG), lambda j, i: (j, i, 0))],
                 [jax.ShapeDtypeStruct((N_CONV_SEG, t, SEG), F32)], [xb, wt], {})


def _t5_bucket(d):
    max_exact = N_BUCKETS // 2
    df = jnp.maximum(d, 1).astype(F32)
    large = max_exact + (jnp.log(df / max_exact) / math.log(MAX_DISTANCE / max_exact)
                         * (N_BUCKETS - max_exact)).astype(jnp.int32)
    large = jnp.minimum(large, N_BUCKETS - 1)
    return jnp.where(d < max_exact, d, large)


def _prompt_bias_kernel(rb_ref, o_ref):
    h = pl.program_id(0)
    s = lax.broadcasted_iota(jnp.int32, (MOBA_BLOCK, MOBA_BLOCK), 0)
    t = lax.broadcasted_iota(jnp.int32, (MOBA_BLOCK, MOBA_BLOCK), 1)
    for k in range(2):
        dist = MOBA_BLOCK * k + t - s
        bucket = _t5_bucket(jnp.maximum(dist, 0))
        val = jnp.zeros((MOBA_BLOCK, MOBA_BLOCK), F32)
        for b in range(N_BUCKETS):
            val = jnp.where(bucket == b, rb_ref[b, h], val)
        o_ref[0, k] = jnp.where(dist >= 0, val * LOG2E, NEG)


def prompt_bias_tables(rel_bias):
    return pl.pallas_call(
        _prompt_bias_kernel,
        out_shape=jax.ShapeDtypeStruct((N_HEADS, 2, MOBA_BLOCK, MOBA_BLOCK), F32),
        grid=(N_HEADS,),
        in_specs=[pl.BlockSpec(memory_space=pltpu.SMEM)],
        out_specs=pl.BlockSpec((1, 2, MOBA_BLOCK, MOBA_BLOCK), lambda h: (h, 0, 0, 0)),
        compiler_params=_cparams(("parallel",)),
        name="prompt_bias_tables",
    )(rel_bias)


def _sample_bias_kernel(rb_ref, o_ref, *, t_new, past):
    rows = N_HEADS * t_new
    r = lax.broadcasted_iota(jnp.int32, (rows, LANE), 0)
    lane = lax.broadcasted_iota(jnp.int32, (rows, LANE), 1)
    t = r % t_new
    hh = r // t_new

    def lookup(dist):
        bucket = _t5_bucket(jnp.maximum(dist, 0))
        val = jnp.zeros((rows, LANE), F32)
        for b in range(N_BUCKETS):
            rb = jnp.zeros((rows, LANE), F32)
            for h in range(N_HEADS):
                rb = jnp.where(hh == h, rb_ref[b, h], rb)
            val = jnp.where(bucket == b, rb, val)
        return val

    o_ref[0] = lookup(LANE + t - lane)
    dist_new = t - lane
    o_ref[1] = jnp.where(dist_new >= 0, lookup(dist_new), NEG)
    o_ref[2] = lookup(jnp.full((rows, LANE), past, jnp.int32))


def sample_bias_tables(rel_bias, t_new, past):
    rows = N_HEADS * t_new
    return pl.pallas_call(
        functools.partial(_sample_bias_kernel, t_new=t_new, past=past),
        out_shape=jax.ShapeDtypeStruct((3, rows, LANE), F32),
        in_specs=[pl.BlockSpec(memory_space=pltpu.SMEM)],
        out_specs=pl.BlockSpec(memory_space=pltpu.VMEM),
        name="sample_bias_tables",
    )(rel_bias)


def _moba_select_kernel(rb_ref, k_ref, q_ref, o_ref, *, n_blk):
    h = pl.program_id(1)
    seq = q_ref.shape[3]
    kmean = jnp.sum(k_ref[0, 0, 0].reshape(n_blk, MOBA_BLOCK, HEAD_DIM), axis=1) * (1.0 / MOBA_BLOCK)
    gate = lax.dot_general(kmean, q_ref[0, 0, 0], NT_DIMS, precision=HIGHEST, preferred_element_type=F32)
    own = lax.broadcasted_iota(jnp.int32, (1, seq), 1) // MOBA_BLOCK
    far = rb_ref[N_BUCKETS - 1, h] * LOG2E
    o_ref[0, 0] = jnp.where(_top_blocks(gate, own, 0), far, NEG)


def moba_select(k, q, rel_bias, n_batch, seq):
    n_blk = seq // MOBA_BLOCK
    spec = lambda a: pl.BlockSpec((1, 1, 1, seq, HEAD_DIM), lambda b, h: (a[1], b, h, 0, 0))
    return pl.pallas_call(
        functools.partial(_moba_select_kernel, n_blk=n_blk),
        out_shape=jax.ShapeDtypeStruct((n_batch, N_HEADS, n_blk, seq), F32),
        grid=(n_batch, N_HEADS),
        in_specs=[pl.BlockSpec(memory_space=pltpu.SMEM), spec(k), spec(q)],
        out_specs=pl.BlockSpec((1, 1, n_blk, seq), lambda b, h: (b, h, 0, 0)),
        compiler_params=_cparams(("parallel", "parallel")),
        name="moba_select",
    )(rel_bias, k[0], q[0])


def _log_sigmoid(x):
    return -(jnp.maximum(-x, 0.0) + jnp.log1p(jnp.exp(-jnp.abs(x))))


def _split3(x):
    hi = x.astype(BF16).astype(F32)
    r1 = x - hi
    mid = r1.astype(BF16).astype(F32)
    lo = (r1 - mid).astype(BF16).astype(F32)
    return hi, mid, lo


def _fox_prep_kernel(f_ref, bf_ref, lf_ref, kx_ref, qx_ref, carry_ref, *, blk):
    i = pl.program_id(1)

    @pl.when(i == 0)
    def _():
        carry_ref[...] = jnp.zeros_like(carry_ref)

    lane = lax.broadcasted_iota(jnp.int32, (blk, LANE), 1)
    lf = jnp.where(lane < N_HEADS, _log_sigmoid(f_ref[0] + bf_ref[...]), 0.0)
    lf_ref[...] = lf
    row = lax.broadcasted_iota(jnp.int32, (blk, blk), 0)
    col = lax.broadcasted_iota(jnp.int32, (blk, blk), 1)
    tril = (row >= col).astype(BF16)
    cum_all = carry_ref[0:1, :]
    for piece in _split3(lf):
        cum_all = cum_all + jnp.dot(tril, piece.astype(BF16), preferred_element_type=F32)
    carry_ref[0:1, :] = cum_all[blk - 1:blk, :]
    for h in range(N_HEADS):
        cum = jnp.broadcast_to(cum_all[:, h:h + 1], (blk, LANE))
        hi, mid, lo = _split3(cum * LOG2E)
        kx_ref[0, h] = jnp.where(lane == 0, -hi, jnp.where(lane == 1, -mid, jnp.where(
            lane == 2, -lo, jnp.where(lane < 6, 1.0, 0.0)))).astype(BF16)
        qx_ref[0, h] = jnp.where(lane == 3, hi, jnp.where(lane == 4, mid, jnp.where(
            lane == 5, lo, jnp.where(lane < 3, 1.0, 0.0)))).astype(BF16)


def fox_prep(zc, bf_row, n_batch, seq, blk=256):
    n_i = seq // blk
    xshape = jax.ShapeDtypeStruct((n_batch, N_HEADS, seq, LANE), BF16)
    xspec = pl.BlockSpec((1, N_HEADS, blk, LANE), lambda b, i: (b, 0, i, 0))
    return pl.pallas_call(
        functools.partial(_fox_prep_kernel, blk=blk),
        out_shape=(jax.ShapeDtypeStruct((n_batch * seq, LANE), F32), xshape, xshape),
        grid=(n_batch, n_i),
        in_specs=[pl.BlockSpec((1, blk, LANE), lambda b, i: (F_SEG, b * n_i + i, F_LANEBLK)),
                  pl.BlockSpec((1, LANE), lambda b, i: (0, 0))],
        out_specs=(pl.BlockSpec((blk, LANE), lambda b, i: (b * n_i + i, 0)), xspec, xspec),
        scratch_shapes=[pltpu.VMEM((SUBLANE, LANE), F32)],
        compiler_params=_cparams(("parallel", "arbitrary")),
        name="fox_prep",
    )(zc, bf_row)


HEADS_PER_STEP = 6
HEAD_GROUP = 3


def _attn_tiles(qs, k_ref, kx_ref, v_ref, b0, nb, bias_fn, state):
    m_ref, l_ref, acc_ref = state
    start = pl.multiple_of(b0 * MOBA_BLOCK, MOBA_BLOCK)
    tk = nb * MOBA_BLOCK
    hp = len(qs)
    scores = {}

    def logits(heads):
        for i in heads:
            kt = k_ref[0, 0, i, pl.ds(start, tk), :]
            if kx_ref is not None:
                kt = jnp.concatenate([kt, kx_ref[0, i, pl.ds(start, tk), :]], axis=1)
            scores[i] = bias_fn(lax.dot_general(kt, qs[i], NT_DIMS, preferred_element_type=F32), i, b0)

    def update(heads):
        probs = {}
        for i in heads:
            m = m_ref[i]
            m_new = jnp.maximum(m, jnp.max(scores[i], axis=0, keepdims=True))
            alpha = jnp.exp2(m - m_new)
            p = jnp.exp2(scores.pop(i) - m_new)
            m_ref[i] = m_new
            l_ref[i] = alpha * l_ref[i] + jnp.sum(p, axis=0, keepdims=True)
            probs[i] = (p.astype(BF16), alpha)
        for i in heads:
            p, alpha = probs[i]
            vt = v_ref[0, 0, i, pl.ds(start, tk), :]
            pv = lax.dot_general(vt, p, TN_DIMS, preferred_element_type=F32)
            acc_ref[i] = alpha * acc_ref[i] + pv

    groups = [range(i, min(i + HEAD_GROUP, hp)) for i in range(0, hp, HEAD_GROUP)]
    logits(groups[0])
    for gi, heads in enumerate(groups):
        if gi + 1 < len(groups):
            logits(groups[gi + 1])
        update(heads)


def _attn_state_shapes(tq, hp):
    return [pltpu.VMEM((hp, 1, tq), F32), pltpu.VMEM((hp, 1, tq), F32), pltpu.VMEM((hp, HEAD_DIM, tq), F32)]


def _attn_init(state):
    m_ref, l_ref, acc_ref = state
    m_ref[...] = jnp.full(m_ref.shape, NEG, F32)
    l_ref[...] = jnp.zeros(l_ref.shape, F32)
    acc_ref[...] = jnp.zeros(acc_ref.shape, F32)


def _attn_finish(o_ref, state):
    m_ref, l_ref, acc_ref = state
    for i in range(acc_ref.shape[0]):
        o_ref[0, i] = (acc_ref[i] / l_ref[i]).T


PAST_BLOCKS = 4


def _attn_drive(j, tiles):
    jf = jnp.maximum(j - 1, 0)

    @pl.loop(0, jf // PAST_BLOCKS)
    def _(n):
        tiles(n * PAST_BLOCKS, PAST_BLOCKS, False)

    nb = PAST_BLOCKS // 2
    while nb >= 1:
        @pl.when(jf % (2 * nb) >= nb)
        def _(nb=nb):
            tiles((jf // (2 * nb)) * (2 * nb), nb, False)
        nb //= 2

    @pl.when(j > 0)
    def _():
        tiles(j - 1, 2, True)

    @pl.when(j == 0)
    def _():
        tiles(0, 1, True)


def _fox_prompt_kernel(idx, q_ref, qx_ref, k_ref, kx_ref, v_ref, o_ref, *state, tq, hp):
    j = idx[2]
    qs = [jnp.concatenate([q_ref[0, 0, i], qx_ref[0, i]], axis=1) for i in range(hp)]

    def tiles(b0, nb, last):
        def bias(s, i, b0):
            if not last:
                return s
            key = lax.broadcasted_iota(jnp.int32, s.shape, 0) - (nb - 1) * tq
            qry = lax.broadcasted_iota(jnp.int32, s.shape, 1)
            return jnp.where(key <= qry, s, NEG)
        _attn_tiles(qs, k_ref, kx_ref, v_ref, b0, nb, bias, state)

    _attn_init(state)
    _attn_drive(j, tiles)
    _attn_finish(o_ref, state)


def _head_spec(seg, hp, rows, row_map, **kw):
    return pl.BlockSpec((1, 1, hp, rows, HEAD_DIM), lambda b, g, j: (seg, b, g, row_map(j), 0), **kw)


_RESIDENT = dict(pipeline_mode=pl.Buffered(1))


def fox_prompt(q, k, v, kx, qx, n_batch, seq, tq=256, hp=HEADS_PER_STEP):
    nq = seq // tq
    return _Call(
        "fox_prompt", functools.partial(_fox_prompt_kernel, tq=tq, hp=hp), (n_batch, N_HEADS // hp, nq),
        [_head_spec(q[1], hp, tq, lambda j: j),
         pl.BlockSpec((1, hp, tq, LANE), lambda b, g, j: (b, g, j, 0)),
         _head_spec(k[1], hp, seq, lambda j: 0, **_RESIDENT),
         pl.BlockSpec((1, hp, seq, LANE), lambda b, g, j: (b, g, 0, 0), **_RESIDENT),
         _head_spec(v[1], hp, seq, lambda j: 0, **_RESIDENT)],
        [pl.BlockSpec((1, hp, tq, HEAD_DIM), lambda b, g, j: (b, g, j, 0))],
        [jax.ShapeDtypeStruct((n_batch, N_HEADS, seq, HEAD_DIM), F32)],
        [q[0], qx, k[0], kx, v[0]], {}, None, tuple(_attn_state_shapes(tq, hp)))


def _top_blocks(gate, n_valid, axis):
    n_blk = gate.shape[axis]
    blk = lax.broadcasted_iota(jnp.int32, gate.shape, axis)
    blk_f = blk.astype(F32)
    g = jnp.where(blk < n_valid, gate, -jnp.inf)
    sel = jnp.zeros(gate.shape, jnp.bool_)
    for _ in range(MOBA_TOPK):
        mx = jnp.max(g, axis=axis, keepdims=True)
        cand = (g == mx) & (mx > -jnp.inf)
        first = jnp.min(jnp.where(cand, blk_f, float(n_blk)), axis=axis, keepdims=True)
        pick = blk_f == first
        sel = sel | pick
        g = jnp.where(pick, -jnp.inf, g)
    return sel


def _moba_prompt_kernel(idx, rb_ref, q_ref, k_ref, v_ref, mb_ref, tab_ref, o_ref, *state, tq, hp):
    g, j = idx[1], idx[2]
    qs = [q_ref[0, 0, i] for i in range(hp)]
    fars = [rb_ref[N_BUCKETS - 1, g * hp + i] * LOG2E for i in range(hp)]

    def tiles(b0, nb, last):
        def bias(s, i, b0):
            rows = lambda r: s[r * tq:(r + 1) * tq]
            if not last:
                return jnp.concatenate([rows(r) + mb_ref[0, i, pl.ds(b0 + r, 1), :] for r in range(nb)], axis=0)
            own = rows(nb - 1) + tab_ref[i, 0]
            if nb == 1:
                return own
            prev = rows(0) + tab_ref[i, 1] + (mb_ref[0, i, pl.ds(b0, 1), :] - fars[i])
            return jnp.concatenate([prev, own], axis=0)
        _attn_tiles(qs, k_ref, None, v_ref, b0, nb, bias, state)

    _attn_init(state)
    _attn_drive(j, tiles)
    _attn_finish(o_ref, state)


def moba_prompt(q, k, v, mb, tab, rel_bias, n_batch, seq, hp=HEADS_PER_STEP):
    tq = MOBA_BLOCK
    nq = seq // tq
    return _Call(
        "moba_prompt", functools.partial(_moba_prompt_kernel, tq=tq, hp=hp), (n_batch, N_HEADS // hp, nq),
        [pl.BlockSpec(memory_space=pltpu.SMEM),
         _head_spec(q[1], hp, tq, lambda j: j),
         _head_spec(k[1], hp, seq, lambda j: 0, **_RESIDENT),
         _head_spec(v[1], hp, seq, lambda j: 0, **_RESIDENT),
         pl.BlockSpec((1, hp, nq, tq), lambda b, g, j: (b, g, 0, j)),
         pl.BlockSpec((hp, 2, tq, tq), lambda b, g, j: (g, 0, 0, 0), **_RESIDENT)],
        [pl.BlockSpec((1, hp, tq, HEAD_DIM), lambda b, g, j: (b, g, j, 0))],
        [jax.ShapeDtypeStruct((n_batch, N_HEADS, seq, HEAD_DIM), F32)],
        [rel_bias, q[0], k[0], v[0], mb, tab], {}, None, tuple(_attn_state_shapes(tq, hp)))


def _pad_rows(x, rows):
    return jnp.concatenate([x, jnp.zeros((rows - x.shape[0], x.shape[1]), x.dtype)], axis=0)


def _lanes(a, b):
    return jnp.concatenate([a, b], axis=1)


def _rows(a, b):
    return jnp.concatenate([a, b], axis=0)


def _pair_queries(q_ref, g, scale):
    q0, q1 = q_ref[0, 0, 2 * g], q_ref[0, 0, 2 * g + 1]
    z = jnp.zeros_like(q0)
    return _rows(_lanes(q0, z), _lanes(z, q1)) * scale


def _pair_tile(pages, p0, g):
    t = lambda p: _lanes(pages[p][0, 0, 2 * g].astype(BF16), pages[p][0, 0, 2 * g + 1].astype(BF16))
    return _rows(t(p0), t(p0 + 1))


def _pair_new(x_ref, g, page):
    return _lanes(_pad_rows(x_ref[0, 0, 2 * g], page), _pad_rows(x_ref[0, 0, 2 * g + 1], page)).astype(BF16)


def _pair_bias(top, bottom, t_new):
    return _rows(jnp.broadcast_to(top, (t_new, top.shape[1])), jnp.broadcast_to(bottom, (t_new, bottom.shape[1])))


def _pair_probs(s_list, s_new):
    m = s_list[0]
    for s in s_list[1:]:
        m = jnp.maximum(m, s)
    m = jnp.max(jnp.maximum(jnp.maximum(m[:, :LANE], m[:, LANE:]), s_new), axis=1, keepdims=True)
    l = None
    p_list = []
    for s in s_list:
        p = jnp.exp(s - m)
        l = p if l is None else l + p
        p_list.append(p.astype(BF16))
    p = jnp.exp(s_new - m)
    l = jnp.sum(l[:, :LANE] + l[:, LANE:] + p, axis=1, keepdims=True)
    return p_list, p.astype(BF16), l


def _pair_pv(o_ref, g, probs, v_pages, v_ref, t_new, page):
    p_list, p_new, l = probs
    acc = jnp.dot(p_new, _pair_new(v_ref, g, page), preferred_element_type=F32)
    for n, p in enumerate(p_list):
        acc = acc + jnp.dot(p, _pair_tile(v_pages, 2 * n, g), preferred_element_type=F32)
    out = acc / l
    o_ref[0, 2 * g] = out[:t_new, :HEAD_DIM]
    o_ref[0, 2 * g + 1] = out[t_new:, HEAD_DIM:]


def _fox_sample_kernel(pt_ref, q_ref, k_ref, v_ref, f_ref, bf_ref, *rest, t_new, n_pages, page):
    k_pages = rest[:n_pages]
    v_pages = rest[n_pages:2 * n_pages]
    lf_pages = rest[2 * n_pages:3 * n_pages]
    o_ref, lf_ref = rest[3 * n_pages:]

    n_r = n_pages * SUBLANE
    lane = lax.broadcasted_iota(jnp.int32, (n_r, page), 1)
    srow = lax.broadcasted_iota(jnp.int32, (n_r, page), 0)
    lfc = jnp.concatenate([r[0, 0] for r in lf_pages], axis=0)
    incl = lfc
    step = 1
    while step < page:
        incl = incl + jnp.where(lane + step < page, pltpu.roll(incl, page - step, axis=1), 0.0)
        step *= 2
    tot = jnp.broadcast_to(jnp.sum(jnp.where(lane == 0, incl, 0.0), axis=1, keepdims=True), (n_r, page))
    run = tot
    step = SUBLANE
    while step < n_r:
        run = run + jnp.where(srow + step < n_r, pltpu.roll(run, n_r - step, axis=0), 0.0)
        step *= 2
    suf = (incl - lfc) + (run - tot)

    flane = lax.broadcasted_iota(jnp.int32, (t_new, LANE), 1)
    lf_new = jnp.where(flane < N_HEADS, _log_sigmoid(f_ref[0] + bf_ref[...]), 0.0)
    lf_ref[...] = lf_new
    cum = _pad_rows(lf_new, LANE).T
    lane2 = lax.broadcasted_iota(jnp.int32, (LANE, LANE), 1)
    step = 1
    while step < t_new:
        cum = cum + jnp.where(lane2 >= step, pltpu.roll(cum, step, axis=1), 0.0)
        step *= 2
    r = lax.broadcasted_iota(jnp.int32, (2 * t_new, page), 0) % t_new
    c = lax.broadcasted_iota(jnp.int32, (2 * t_new, page), 1)
    suf_row = lambda p, h: suf[p * SUBLANE + h:p * SUBLANE + h + 1, :]

    pairs = range(N_HEADS // 2)
    logits = []
    for g in pairs:
        h0, h1 = 2 * g, 2 * g + 1
        qp = _pair_queries(q_ref, g, SCALE).astype(BF16)
        s_list = []
        for p0 in range(0, n_pages, 2):
            s = lax.dot_general(qp, _pair_tile(k_pages, p0, g), NT_DIMS, preferred_element_type=F32)
            bias = _pair_bias(_lanes(suf_row(p0, h0), suf_row(p0 + 1, h0)),
                              _lanes(suf_row(p0, h1), suf_row(p0 + 1, h1)), t_new)
            s_list.append(s + bias)
        s_new = lax.dot_general(qp, _pair_new(k_ref, g, page), NT_DIMS, preferred_element_type=F32)
        s_new = jnp.where(c <= r, s_new - _pair_bias(cum[h0:h0 + 1, :], cum[h1:h1 + 1, :], t_new), NEG)
        logits.append((s_list, s_new))
    probs = [_pair_probs(*logits[g]) for g in pairs]
    for g in pairs:
        _pair_pv(o_ref, g, probs[g], v_pages, v_ref, t_new, page)


def _page_specs(n_pages, layer, shape, b0):
    def spec(p):
        return pl.BlockSpec((1, 1) + shape, lambda b, pt: (layer, pt[b0 + b, p]) + (0,) * len(shape))
    return [spec(p) for p in range(n_pages)]


def _new_spec(seg, t_new, b0):
    return pl.BlockSpec((1, 1, N_HEADS, t_new, HEAD_DIM), lambda b, pt: (seg, b0 + b, 0, 0, 0))


def fox_sample(q, k, v, zc, bf_row, pool_k, pool_v, pool_lft, page_table, layer, t_new, b0, n):
    n_pages = page_table.shape[1]
    page = pool_k.shape[3]
    in_specs = ([_new_spec(q[1], t_new, b0), _new_spec(k[1], t_new, b0), _new_spec(v[1], t_new, b0),
                 pl.BlockSpec((1, t_new, LANE), lambda b, pt: (F_SEG, b0 + b, F_LANEBLK)),
                 pl.BlockSpec((1, LANE), lambda b, pt: (0, 0))]
                + _page_specs(n_pages, layer, (N_HEADS, page, HEAD_DIM), b0)
                + _page_specs(n_pages, layer, (N_HEADS, page, HEAD_DIM), b0)
                + _page_specs(n_pages, layer, (SUBLANE, page), b0))
    return _Call(
        "fox_sample", functools.partial(_fox_sample_kernel, t_new=t_new, n_pages=n_pages, page=page), (n,), in_specs,
        [pl.BlockSpec((1, N_HEADS, t_new, HEAD_DIM), lambda b, pt: (b, 0, 0, 0)),
         pl.BlockSpec((t_new, LANE), lambda b, pt: (b, 0))],
        [jax.ShapeDtypeStruct((n, N_HEADS, t_new, HEAD_DIM), F32), jax.ShapeDtypeStruct((n * t_new, LANE), F32)],
        [q[0], k[0], v[0], zc, bf_row] + [pool_k] * n_pages + [pool_v] * n_pages + [pool_lft] * n_pages, {},
        page_table)


def _moba_sample_kernel(pt_ref, q_ref, k_ref, v_ref, tab_ref, *rest, t_new, n_pages, page):
    k_pages = rest[:n_pages]
    v_pages = rest[n_pages:2 * n_pages]
    o_ref = rest[2 * n_pages]
    assert MOBA_BLOCK == 2 * page
    n_blk = n_pages // 2

    def block_means(h):
        sums = [jnp.sum(k_pages[2 * n][0, 0, h] + k_pages[2 * n + 1][0, 0, h], axis=0, keepdims=True)
                for n in range(n_blk)]
        return _pad_rows(jnp.concatenate(sums, axis=0) * (1.0 / MOBA_BLOCK), LANE)

    pairs = range(N_HEADS // 2)
    logits = []
    for g in pairs:
        q32 = _pair_queries(q_ref, g, 1.0)
        qp = (q32 * SCALE).astype(BF16)
        kmean = _lanes(block_means(2 * g), block_means(2 * g + 1))
        gate = lax.dot_general(q32, kmean, NT_DIMS, precision=HIGHEST, preferred_element_type=F32)
        sel = _top_blocks(gate, n_blk, 1)
        mask = jnp.where(sel, 0.0, NEG)
        rows = slice(2 * g * t_new, (2 * g + 2) * t_new)
        far = tab_ref[2, rows, :]
        s_list = []
        for n in range(n_blk):
            s = lax.dot_general(qp, _pair_tile(k_pages, 2 * n, g), NT_DIMS, preferred_element_type=F32)
            bias = _lanes(far, tab_ref[0, rows, :] if n == n_blk - 1 else far)
            s_list.append(s + bias + mask[:, n:n + 1])
        s_new = lax.dot_general(qp, _pair_new(k_ref, g, page), NT_DIMS, preferred_element_type=F32)
        logits.append((s_list, s_new + tab_ref[1, rows, :]))
    probs = [_pair_probs(*logits[g]) for g in pairs]
    for g in pairs:
        _pair_pv(o_ref, g, probs[g], v_pages, v_ref, t_new, page)


def moba_sample(q, k, v, tab, pool_k, pool_v, page_table, layer, t_new, b0, n):
    n_pages = page_table.shape[1]
    page = pool_k.shape[3]
    rows = N_HEADS * t_new
    in_specs = ([_new_spec(q[1], t_new, b0), _new_spec(k[1], t_new, b0), _new_spec(v[1], t_new, b0),
                 pl.BlockSpec((3, rows, LANE), lambda b, pt: (0, 0, 0))]
                + _page_specs(n_pages, layer, (N_HEADS, page, HEAD_DIM), b0)
                + _page_specs(n_pages, layer, (N_HEADS, page, HEAD_DIM), b0))
    return _Call(
        "moba_sample", functools.partial(_moba_sample_kernel, t_new=t_new, n_pages=n_pages, page=page), (n,),
        in_specs, [pl.BlockSpec((1, N_HEADS, t_new, HEAD_DIM), lambda b, pt: (b, 0, 0, 0))],
        [jax.ShapeDtypeStruct((n, N_HEADS, t_new, HEAD_DIM), F32)],
        [q[0], k[0], v[0], tab] + [pool_k] * n_pages + [pool_v] * n_pages, {}, page_table)


def _silu(g):
    return g / (1.0 + jnp.exp(-g))


def _token_major(v, tm):
    return jnp.concatenate([v[:, h].reshape(tm, HEAD_DIM) for h in range(N_HEADS)], axis=1)


def _combine_kernel(*refs, tm, seq_tiles, seq_rows, alpha):
    (x_ref, oa_ref, of_ref, ga_ref, gf_ref, b0, b1, c0, c1, h0, h1, g0, g1) = refs[:13]
    if seq_rows is None:
        pc0, pc1, ph0, ph1 = refs[13:17]
        rest = refs[17:]
    else:
        hist1_ref, hist2_ref = refs[13:15]
        rest = refs[15:]
    wout_ref, cw_ref, lng_ref, lnb_ref, y_ref, yb_ref, u_ref = rest
    cat = lambda a, b: jnp.concatenate([a[0], b[0]], axis=1)
    u = cat(c0, c1) * cat(h0, h1)
    if seq_rows is None:
        keep = (pl.program_id(0) % seq_tiles != 0).astype(F32)
        uprev = cat(pc0, pc1) * cat(ph0, ph1) * keep
        ext = jnp.concatenate([uprev, u], axis=0)
        u1 = pltpu.roll(ext, 1, axis=0)[SUBLANE:]
        u2 = pltpu.roll(ext, 2, axis=0)[SUBLANE:]
        u_ref[0] = u[tm - SUBLANE:, :]
    else:
        r = lax.broadcasted_iota(jnp.int32, (tm, D_CONV), 0) % seq_rows
        u1 = jnp.where(r >= 1, pltpu.roll(u, 1, axis=0), hist1_ref[...])
        u2 = jnp.where(r >= 2, pltpu.roll(u, 2, axis=0), hist2_ref[...])
        u_ref[...] = u
    conv = cw_ref[0:1, :] * u2 + cw_ref[1:2, :] * u1 + cw_ref[2:3, :] * u
    oc = (cat(b0, b1) * conv * _silu(cat(g0, g1))).astype(BF16)
    oa = (_token_major(oa_ref[...], tm) * _silu(_token_major(ga_ref[0], tm))).astype(BF16)
    of = (_token_major(of_ref[...], tm) * _silu(_token_major(gf_ref[0], tm))).astype(BF16)
    proj = (jnp.dot(oa, wout_ref[0:D_ATT, :], preferred_element_type=F32)
            + jnp.dot(of, wout_ref[D_ATT:2 * D_ATT, :], preferred_element_type=F32)
            + jnp.dot(oc, wout_ref[2 * D_ATT:, :], preferred_element_type=F32))
    res = alpha * x_ref[...] + proj
    mu = jnp.mean(res, axis=-1, keepdims=True)
    cen = res - mu
    var = jnp.mean(cen * cen, axis=-1, keepdims=True)
    y = cen * lax.rsqrt(var + LN_EPS) * lng_ref[...] + lnb_ref[...]
    y_ref[...] = y
    yb_ref[...] = y.astype(BF16)


def combine(x, oa, of, zg, zc, wout, cw, lng, lnb, alpha, tm, hist=None):
    t = x.shape[0]
    n_seq, _, seq_len, _ = oa.shape
    assert t == n_seq * seq_len and t % tm == 0
    if seq_len >= tm:
        assert seq_len % tm == 0
        sb, sl, per_seq = 1, tm, seq_len // tm
        hmap = lambda i: (i // per_seq, 0, i % per_seq, 0)
    else:
        assert tm % seq_len == 0
        sb, sl, per_seq = tm // seq_len, seq_len, 1
        hmap = lambda i: (i, 0, 0, 0)
    hblk = (sb, N_HEADS, sl, HEAD_DIM)
    zblk = lambda sc: pl.BlockSpec((1, tm, CBLK), lambda i: (sc[0], i, sc[1]))
    pieces = [sc for idx in (0, 1, 2, 3) for sc in _conv_piece(idx)]
    in_specs = [pl.BlockSpec((tm, D_MODEL), lambda i: (i, 0)),
                pl.BlockSpec(hblk, hmap),
                pl.BlockSpec(hblk, hmap),
                pl.BlockSpec((1,) + hblk, lambda i: (0,) + hmap(i)),
                pl.BlockSpec((1,) + hblk, lambda i: (1,) + hmap(i))] + [zblk(sc) for sc in pieces]
    args = [x, oa, of, zg, zg] + [zc] * 8
    if hist is None:
        seq_rows = None
        tb = tm // SUBLANE
        prev = lambda sc: pl.BlockSpec((1, SUBLANE, CBLK), lambda i: (sc[0], jnp.maximum(i * tb - 1, 0), sc[1]))
        in_specs += [prev(sc) for sc in pieces[2:6]]
        args += [zc] * 4
        u_shape = jax.ShapeDtypeStruct((n_seq, SUBLANE, D_CONV), F32)
        u_spec = pl.BlockSpec((1, SUBLANE, D_CONV), lambda i: (i // per_seq, 0, 0))
    else:
        seq_rows = seq_len
        in_specs += [pl.BlockSpec((tm, D_CONV), lambda i: (i, 0))] * 2
        args += list(hist)
        u_shape = jax.ShapeDtypeStruct((t, D_CONV), F32)
        u_spec = pl.BlockSpec((tm, D_CONV), lambda i: (i, 0))
    in_specs += [pl.BlockSpec((D_MODEL, D_MODEL), lambda i: (0, 0)),
                 pl.BlockSpec((SUBLANE, D_CONV), lambda i: (0, 0)),
                 pl.BlockSpec((1, D_MODEL), lambda i: (0, 0)),
                 pl.BlockSpec((1, D_MODEL), lambda i: (0, 0))]
    args += [wout, cw, lng, lnb]
    return pl.pallas_call(
        functools.partial(_combine_kernel, tm=tm, seq_tiles=per_seq, seq_rows=seq_rows, alpha=alpha),
        out_shape=(jax.ShapeDtypeStruct((t, D_MODEL), F32), jax.ShapeDtypeStruct((t, D_MODEL), BF16), u_shape),
        grid=(t // tm,),
        in_specs=in_specs,
        out_specs=(pl.BlockSpec((tm, D_MODEL), lambda i: (i, 0)), pl.BlockSpec((tm, D_MODEL), lambda i: (i, 0)),
                   u_spec),
        compiler_params=_cparams(("arbitrary",)),
        name="combine_prompt" if hist is None else "combine_sample",
    )(*args)


def kernel(x_prompt, x_sample, cache_moba_k, cache_moba_v, cache_fox_k, cache_fox_v, cache_fox_logf,
           state_conv, page_table, rel_bias, w_in, b_f, conv_w, w_out, ln_g, ln_b):
    depth = w_in.shape[0]
    alpha = (2 * depth) ** 0.25
    n_batch, seq, _ = x_prompt.shape
    n_seq, t_new, _ = x_sample.shape
    n_phys, page = cache_moba_k.shape[1:3]
    past = page_table.shape[1] * page
    assert page == LANE and past % MOBA_BLOCK == 0 and t_new == SUBLANE and seq % MOBA_BLOCK == 0
    assert past >= MAX_DISTANCE and w_in.shape[2] == N_MAIN + N_HEADS + 4 * D_CONV

    xp = x_prompt.reshape(n_batch * seq, D_MODEL)
    xs = x_sample.reshape(n_seq * t_new, D_MODEL)
    xpb, xsb = xp.astype(BF16), xs.astype(BF16)
    pool = lambda c: jnp.swapaxes(c, 2, 3)
    pk_a, pv_a, pk_f, pv_f = pool(cache_moba_k), pool(cache_moba_v), pool(cache_fox_k), pool(cache_fox_v)
    pool_lft = jnp.pad(jnp.swapaxes(cache_fox_logf, 2, 3), ((0, 0), (0, 0), (0, SUBLANE - N_HEADS), (0, 0)))
    tab_p = prompt_bias_tables(rel_bias)
    tab_s = sample_bias_tables(rel_bias, t_new, past)
    tm_s = min(256, n_seq * t_new)

    outs = {k: [] for k in ("p_lf", "p_cv", "s_lf", "s_cv")}
    kv_segs = {"ka": SEG_KA, "va": SEG_VA, "kf": SEG_KF, "vf": SEG_VF}
    p_kv = dict.fromkeys(kv_segs)
    s_kv = dict.fromkeys(kv_segs)
    for layer in range(depth):
        wt = jnp.swapaxes(w_in[layer], 0, 1)
        n_f = N_MAIN + N_HEADS
        wt_h = wt[:N_MAIN].astype(BF16)
        wt_c = jnp.concatenate(
            [wt[n_f:], wt[N_MAIN:n_f],
             jnp.zeros((N_CONV_SEG * SEG - 4 * D_CONV - N_HEADS, D_MODEL), wt.dtype)], axis=0).astype(BF16)
        wo = w_out[layer].astype(BF16)
        bf_row = jnp.pad(b_f[layer], (0, LANE - N_HEADS)).reshape(1, LANE)
        cw = jnp.pad(conv_w[layer], ((0, SUBLANE - CONV_WIDTH), (0, 0)))
        lng, lnb = ln_g[layer].reshape(1, D_MODEL), ln_b[layer].reshape(1, D_MODEL)

        proj = functools.partial(inproj_heads, xsb, wt_h, n_seq=n_seq, seq_len=t_new, tm=n_seq * t_new)
        (zqs,) = run(proj((SEG_QA, SEG_QF)))
        (zgs,) = run(proj((SEG_GA, SEG_GF)))
        for name, seg in kv_segs.items():
            (s_kv[name],) = run(proj((seg,), stack=(depth, layer, s_kv[name])))
        (zcs,) = run(inproj_conv(xsb, wt_c, tm=n_seq * t_new))

        sample = {
            "fox": functools.partial(fox_sample, (zqs, 1), (s_kv["kf"], layer), (s_kv["vf"], layer), zcs, bf_row,
                                     pk_f, pv_f, pool_lft, page_table, layer, t_new),
            "moba": functools.partial(moba_sample, (zqs, 0), (s_kv["ka"], layer), (s_kv["va"], layer), tab_s,
                                      pk_a, pv_a, page_table, layer, t_new)}
        next_seq = {"fox": 0, "moba": 0}
        chunks = {"fox": [], "moba": []}

        def run_with_sample(call):
            steps = math.prod(call.grid)
            for kind in sample:
                if n_seq - next_seq[kind] >= steps:
                    outs_a, outs_b = run_zipped(call, sample[kind](next_seq[kind], steps))
                    next_seq[kind] += steps
                    chunks[kind].append(outs_b)
                    return outs_a
            return run(call)

        proj = functools.partial(inproj_heads, xpb, wt_h, n_seq=n_batch, seq_len=seq, tm=512)
        zq, zqb = run_with_sample(proj((SEG_QA, SEG_QF), bf16_scale=QSCALE))
        (zg,) = run_with_sample(proj((SEG_GA, SEG_GF)))
        kvb = {}
        for name, seg in kv_segs.items():
            p_kv[name], kvb[name] = run_with_sample(proj((seg,), bf16_scale=1.0, stack=(depth, layer, p_kv[name])))
        (zc,) = run_with_sample(inproj_conv(xpb, wt_c, tm=512))

        mb = moba_select((p_kv["ka"], layer), (zq, 0), rel_bias, n_batch, seq)
        (oa,) = run_with_sample(
            moba_prompt((zqb, 0), (kvb["ka"], 0), (kvb["va"], 0), mb, tab_p, rel_bias, n_batch, seq))
        lf, kx, qx = fox_prep(zc, bf_row, n_batch, seq)
        (of,) = run_with_sample(fox_prompt((zqb, 1), (kvb["kf"], 0), (kvb["vf"], 0), kx, qx, n_batch, seq))
        for kind in sample:
            if next_seq[kind] < n_seq:
                chunks[kind].append(run(sample[kind](next_seq[kind], n_seq - next_seq[kind])))
        ofs = jnp.concatenate([c[0] for c in chunks["fox"]], axis=0)
        lfs = jnp.concatenate([c[1] for c in chunks["fox"]], axis=0)
        oas = jnp.concatenate([c[0] for c in chunks["moba"]], axis=0)
        xp, xpb, ulast = combine(xp, oa, of, zg, zc, wo, cw, lng, lnb, alpha, tm=256)
        outs["p_lf"].append(lf[:, :N_HEADS].reshape(n_batch, seq, N_HEADS))
        outs["p_cv"].append(ulast[:, SUBLANE - (CONV_WIDTH - 1):, :])

        st = state_conv[layer]
        zero = jnp.zeros((n_seq, t_new, D_CONV), F32)
        hist1 = zero.at[:, 0].set(st[:, 1]).reshape(n_seq * t_new, D_CONV)
        hist2 = zero.at[:, 0].set(st[:, 0]).at[:, 1].set(st[:, 1]).reshape(n_seq * t_new, D_CONV)
        xs, xsb, us = combine(xs, oas, ofs, zgs, zcs, wo, cw, lng, lnb, alpha, tm=tm_s, hist=(hist1, hist2))
        outs["s_lf"].append(lfs[:, :N_HEADS].reshape(n_seq, t_new, N_HEADS))
        outs["s_cv"].append(us.reshape(n_seq, t_new, D_CONV)[:, t_new - (CONV_WIDTH - 1):, :])

    st = {k: jnp.stack(v) for k, v in outs.items()}
    tok_major = lambda a: jnp.swapaxes(a, 2, 3)
    return (xp.reshape(n_batch, seq, D_MODEL), xs.reshape(n_seq, t_new, D_MODEL),
            tok_major(p_kv["ka"]), tok_major(p_kv["va"]), tok_major(p_kv["kf"]), tok_major(p_kv["vf"]),
            st["p_lf"], st["p_cv"],
            tok_major(s_kv["ka"]), tok_major(s_kv["va"]), tok_major(s_kv["kf"]), tok_major(s_kv["vf"]),
            st["s_lf"], st["s_cv"])
```

```python
import functools
import math
from typing import Callable, NamedTuple

import jax
import jax.numpy as jnp
from jax import lax
from jax.experimental import pallas as pl
from jax.experimental.pallas import tpu as pltpu

F32 = jnp.float32
BF16 = jnp.bfloat16
HIGHEST = lax.Precision.HIGHEST

D_MODEL = 2048
HEAD_DIM = 128
N_HEADS = 6
D_ATT = N_HEADS * HEAD_DIM
D_CONV = 512
CONV_WIDTH = 3
MOBA_BLOCK = 256
MOBA_TOPK = 3
N_BUCKETS = 32
MAX_DISTANCE = 128
LN_EPS = 1e-5
SCALE = HEAD_DIM ** -0.5
LOG2E = math.log2(math.e)
QSCALE = SCALE * LOG2E

SEG = 768
SEG_QA, SEG_KA, SEG_VA, SEG_GA, SEG_QF, SEG_KF, SEG_VF, SEG_GF = range(8)
N_HEAD_SEG = 8
N_CONV_SEG = 3
N_MAIN = N_HEAD_SEG * SEG
CBLK = 256
LANE = 128
SUBLANE = 8
F_SEG, F_LANEBLK = (4 * D_CONV) // SEG, ((4 * D_CONV) % SEG) // LANE
NEG = -1e30
VMEM_LIMIT = 56 * 1024 * 1024

NT_DIMS = (((1,), (1,)), ((), ()))
TN_DIMS = (((0,), (0,)), ((), ()))


def _cparams(sem):
    return pltpu.CompilerParams(dimension_semantics=sem, vmem_limit_bytes=VMEM_LIMIT)


class _Call(NamedTuple):
    name: str
    body: Callable
    grid: tuple
    in_specs: list
    out_specs: list
    out_shape: list
    args: list
    aliases: dict
    page_table: object = None
    scratch: tuple = ()


def run(c):
    if c.page_table is None:
        body = lambda *refs: c.body(tuple(pl.program_id(d) for d in range(len(c.grid))), *refs)
        return pl.pallas_call(
            body, out_shape=tuple(c.out_shape), grid=c.grid, in_specs=c.in_specs, out_specs=tuple(c.out_specs),
            scratch_shapes=list(c.scratch), input_output_aliases=c.aliases,
            compiler_params=_cparams(("arbitrary",) * len(c.grid)), name=c.name,
        )(*c.args)
    assert not c.aliases and not c.scratch
    return pl.pallas_call(
        c.body, out_shape=tuple(c.out_shape),
        grid_spec=pltpu.PrefetchScalarGridSpec(num_scalar_prefetch=1, grid=c.grid, in_specs=c.in_specs,
                                               out_specs=tuple(c.out_specs)),
        compiler_params=_cparams(("arbitrary",) * len(c.grid)), name=c.name,
    )(c.page_table, *c.args)


def run_zipped(a, b):
    n = b.grid[0]
    assert a.page_table is None and b.page_table is not None and len(b.grid) == 1 and math.prod(a.grid) == n

    def a_idx(k):
        idx = []
        for size in reversed(a.grid):
            idx.append(k % size)
            k = k // size
        return tuple(reversed(idx))

    def respec(sp, fn):
        if sp.block_shape is None:
            return sp
        return pl.BlockSpec(sp.block_shape, fn(sp.index_map), memory_space=sp.memory_space,
                            pipeline_mode=sp.pipeline_mode)

    for_a = lambda m: (lambda k, pt: m(*a_idx(k)))
    for_b = lambda m: m
    na_in, nb_in, na_out = len(a.in_specs), len(b.in_specs), len(a.out_specs)

    nb_out = len(b.out_specs)

    def body(pt_ref, *refs):
        ins_a, refs = refs[:na_in], refs[na_in:]
        ins_b, refs = refs[:nb_in], refs[nb_in:]
        outs_a, refs = refs[:na_out], refs[na_out:]
        outs_b, scratch_a = refs[:nb_out], refs[nb_out:]
        a.body(a_idx(pl.program_id(0)), *ins_a, *outs_a, *scratch_a)
        b.body(pt_ref, *ins_b, *outs_b)

    outs = pl.pallas_call(
        body, out_shape=tuple(a.out_shape) + tuple(b.out_shape),
        grid_spec=pltpu.PrefetchScalarGridSpec(
            num_scalar_prefetch=1, grid=(n,),
            in_specs=[respec(sp, for_a) for sp in a.in_specs] + [respec(sp, for_b) for sp in b.in_specs],
            out_specs=tuple([respec(sp, for_a) for sp in a.out_specs] + [respec(sp, for_b) for sp in b.out_specs]),
            scratch_shapes=list(a.scratch)),
        input_output_aliases={1 + i: o for i, o in a.aliases.items()},
        compiler_params=_cparams(("arbitrary",)), name=a.name + "_zip_" + b.name,
    )(b.page_table, *a.args, *b.args)
    return outs[:na_out], outs[na_out:]


def _conv_piece(idx):
    out = []
    for half in range(2):
        col = idx * D_CONV + half * CBLK
        out.append((col // SEG, (col % SEG) // CBLK))
    return out


def _inproj_heads_kernel(idx, x_ref, w_ref, *refs, sb, sl, n_in, bf16_scale):
    o_ref = refs[n_in]
    res = lax.dot_general(x_ref[...], w_ref[...], NT_DIMS, preferred_element_type=F32)
    for h in range(N_HEADS):
        o_ref[0, :, h] = res[:, h * HEAD_DIM:(h + 1) * HEAD_DIM].reshape(sb, sl, HEAD_DIM)
    if bf16_scale is not None:
        scaled = (res * bf16_scale).astype(BF16)
        for h in range(N_HEADS):
            refs[n_in + 1][0, :, h] = scaled[:, h * HEAD_DIM:(h + 1) * HEAD_DIM].reshape(sb, sl, HEAD_DIM)


def inproj_heads(xb, wt, segs, n_seq, seq_len, tm, bf16_scale=None, stack=None):
    t = xb.shape[0]
    assert t == n_seq * seq_len and t % tm == 0 and wt.shape[0] >= N_MAIN and wt.shape[1] == D_MODEL
    if seq_len >= tm:
        assert seq_len % tm == 0
        sb, sl, per_seq = 1, tm, seq_len // tm
        rows = lambda i: (i // per_seq, 0, i % per_seq, 0)
    else:
        assert tm % seq_len == 0
        sb, sl = tm // seq_len, seq_len
        rows = lambda i: (i, 0, 0, 0)
    blk = (1, sb, N_HEADS, sl, HEAD_DIM)
    tail = (n_seq, N_HEADS, seq_len, HEAD_DIM)

    def seg_of(j):
        s = segs[0]
        for idx in range(1, len(segs)):
            s = jnp.where(j == idx, segs[idx], s)
        return s

    in_specs = [pl.BlockSpec((tm, D_MODEL), lambda j, i: (i, 0)),
                pl.BlockSpec((SEG, D_MODEL), lambda j, i: (seg_of(j), 0))]
    args = [xb, wt]
    aliases = {}
    if stack is None:
        shapes = [jax.ShapeDtypeStruct((len(segs),) + tail, F32)]
        specs = [pl.BlockSpec(blk, lambda j, i: (j,) + rows(i))]
    else:
        depth, layer, prev = stack
        assert len(segs) == 1
        shapes = [jax.ShapeDtypeStruct((depth,) + tail, F32)]
        specs = [pl.BlockSpec(blk, lambda j, i: (layer,) + rows(i))]
        if prev is not None:
            in_specs.append(pl.BlockSpec(memory_space=pl.ANY))
            args.append(prev)
            aliases = {2: 0}
    if bf16_scale is not None:
        shapes.append(jax.ShapeDtypeStruct((len(segs),) + tail, BF16))
        specs.append(pl.BlockSpec(blk, lambda j, i: (j,) + rows(i)))
    body = functools.partial(_inproj_heads_kernel, sb=sb, sl=sl, n_in=len(args) - 2, bf16_scale=bf16_scale)
    return _Call("inproj_heads", body, (len(segs), t // tm), in_specs, specs, shapes, args, aliases)


def _inproj_conv_kernel(idx, x_ref, w_ref, o_ref):
    o_ref[0] = lax.dot_general(x_ref[...], w_ref[...], NT_DIMS, preferred_element_type=F32)


def inproj_conv(xb, wt, tm):
    t = xb.shape[0]
    assert t % tm == 0 and wt.shape == (N_CONV_SEG * SEG, D_MODEL)
    return _Call("inproj_conv", _inproj_conv_kernel, (N_CONV_SEG, t // tm),
                 [pl.BlockSpec((tm, D_MODEL), lambda j, i: (i, 0)),
                  pl.BlockSpec((SEG, D_MODEL), lambda j, i: (j, 0))],
                 [pl.BlockSpec((1, tm, SEG), lambda j, i: (j, i, 0))],
                 [jax.ShapeDtypeStruct((N_CONV_SEG, t, SEG), F32)], [xb, wt], {})


def _t5_bucket(d):
    max_exact = N_BUCKETS // 2
    df = jnp.maximum(d, 1).astype(F32)
    large = max_exact + (jnp.log(df / max_exact) / math.log(MAX_DISTANCE / max_exact)
                         * (N_BUCKETS - max_exact)).astype(jnp.int32)
    large = jnp.minimum(large, N_BUCKETS - 1)
    return jnp.where(d < max_exact, d, large)


def _prompt_bias_kernel(rb_ref, o_ref):
    h = pl.program_id(0)
    s = lax.broadcasted_iota(jnp.int32, (MOBA_BLOCK, MOBA_BLOCK), 0)
    t = lax.broadcasted_iota(jnp.int32, (MOBA_BLOCK, MOBA_BLOCK), 1)
    for k in range(2):
        dist = MOBA_BLOCK * k + t - s
        bucket = _t5_bucket(jnp.maximum(dist, 0))
        val = jnp.zeros((MOBA_BLOCK, MOBA_BLOCK), F32)
        for b in range(N_BUCKETS):
            val = jnp.where(bucket == b, rb_ref[b, h], val)
        o_ref[0, k] = jnp.where(dist >= 0, val * LOG2E, NEG)


def prompt_bias_tables(rel_bias):
    return pl.pallas_call(
        _prompt_bias_kernel,
        out_shape=jax.ShapeDtypeStruct((N_HEADS, 2, MOBA_BLOCK, MOBA_BLOCK), F32),
        grid=(N_HEADS,),
        in_specs=[pl.BlockSpec(memory_space=pltpu.SMEM)],
        out_specs=pl.BlockSpec((1, 2, MOBA_BLOCK, MOBA_BLOCK), lambda h: (h, 0, 0, 0)),
        compiler_params=_cparams(("parallel",)),
        name="prompt_bias_tables",
    )(rel_bias)


def _sample_bias_kernel(rb_ref, o_ref, *, t_new, past):
    rows = N_HEADS * t_new
    r = lax.broadcasted_iota(jnp.int32, (rows, LANE), 0)
    lane = lax.broadcasted_iota(jnp.int32, (rows, LANE), 1)
    t = r % t_new
    hh = r // t_new

    def lookup(dist):
        bucket = _t5_bucket(jnp.maximum(dist, 0))
        val = jnp.zeros((rows, LANE), F32)
        for b in range(N_BUCKETS):
            rb = jnp.zeros((rows, LANE), F32)
            for h in range(N_HEADS):
                rb = jnp.where(hh == h, rb_ref[b, h], rb)
            val = jnp.where(bucket == b, rb, val)
        return val

    o_ref[0] = lookup(LANE + t - lane)
    dist_new = t - lane
    o_ref[1] = jnp.where(dist_new >= 0, lookup(dist_new), NEG)
    o_ref[2] = lookup(jnp.full((rows, LANE), past, jnp.int32))


def sample_bias_tables(rel_bias, t_new, past):
    rows = N_HEADS * t_new
    return pl.pallas_call(
        functools.partial(_sample_bias_kernel, t_new=t_new, past=past),
        out_shape=jax.ShapeDtypeStruct((3, rows, LANE), F32),
        in_specs=[pl.BlockSpec(memory_space=pltpu.SMEM)],
        out_specs=pl.BlockSpec(memory_space=pltpu.VMEM),
        name="sample_bias_tables",
    )(rel_bias)


def _moba_select_kernel(rb_ref, k_ref, q_ref, o_ref, *, n_blk):
    h = pl.program_id(1)
    seq = q_ref.shape[3]
    kmean = jnp.sum(k_ref[0, 0, 0].reshape(n_blk, MOBA_BLOCK, HEAD_DIM), axis=1) * (1.0 / MOBA_BLOCK)
    gate = lax.dot_general(kmean, q_ref[0, 0, 0], NT_DIMS, precision=HIGHEST, preferred_element_type=F32)
    own = lax.broadcasted_iota(jnp.int32, (1, seq), 1) // MOBA_BLOCK
    far = rb_ref[N_BUCKETS - 1, h] * LOG2E
    o_ref[0, 0] = jnp.where(_top_blocks(gate, own, 0), far, NEG)


def moba_select(k, q, rel_bias, n_batch, seq):
    n_blk = seq // MOBA_BLOCK
    spec = lambda a: pl.BlockSpec((1, 1, 1, seq, HEAD_DIM), lambda b, h: (a[1], b, h, 0, 0))
    return pl.pallas_call(
        functools.partial(_moba_select_kernel, n_blk=n_blk),
        out_shape=jax.ShapeDtypeStruct((n_batch, N_HEADS, n_blk, seq), F32),
        grid=(n_batch, N_HEADS),
        in_specs=[pl.BlockSpec(memory_space=pltpu.SMEM), spec(k), spec(q)],
        out_specs=pl.BlockSpec((1, 1, n_blk, seq), lambda b, h: (b, h, 0, 0)),
        compiler_params=_cparams(("parallel", "parallel")),
        name="moba_select",
    )(rel_bias, k[0], q[0])


def _log_sigmoid(x):
    return -(jnp.maximum(-x, 0.0) + jnp.log1p(jnp.exp(-jnp.abs(x))))


def _split3(x):
    hi = x.astype(BF16).astype(F32)
    r1 = x - hi
    mid = r1.astype(BF16).astype(F32)
    lo = (r1 - mid).astype(BF16).astype(F32)
    return hi, mid, lo


def _fox_prep_kernel(f_ref, bf_ref, lf_ref, kx_ref, qx_ref, carry_ref, *, blk):
    i = pl.program_id(1)

    @pl.when(i == 0)
    def _():
        carry_ref[...] = jnp.zeros_like(carry_ref)

    lane = lax.broadcasted_iota(jnp.int32, (blk, LANE), 1)
    lf = jnp.where(lane < N_HEADS, _log_sigmoid(f_ref[0] + bf_ref[...]), 0.0)
    lf_ref[...] = lf
    row = lax.broadcasted_iota(jnp.int32, (blk, blk), 0)
    col = lax.broadcasted_iota(jnp.int32, (blk, blk), 1)
    tril = (row >= col).astype(BF16)
    cum_all = carry_ref[0:1, :]
    for piece in _split3(lf):
        cum_all = cum_all + jnp.dot(tril, piece.astype(BF16), preferred_element_type=F32)
    carry_ref[0:1, :] = cum_all[blk - 1:blk, :]
    for h in range(N_HEADS):
        cum = jnp.broadcast_to(cum_all[:, h:h + 1], (blk, LANE))
        hi, mid, lo = _split3(cum * LOG2E)
        kx_ref[0, h] = jnp.where(lane == 0, -hi, jnp.where(lane == 1, -mid, jnp.where(
            lane == 2, -lo, jnp.where(lane < 6, 1.0, 0.0)))).astype(BF16)
        qx_ref[0, h] = jnp.where(lane == 3, hi, jnp.where(lane == 4, mid, jnp.where(
            lane == 5, lo, jnp.where(lane < 3, 1.0, 0.0)))).astype(BF16)


def fox_prep(zc, bf_row, n_batch, seq, blk=256):
    n_i = seq // blk
    xshape = jax.ShapeDtypeStruct((n_batch, N_HEADS, seq, LANE), BF16)
    xspec = pl.BlockSpec((1, N_HEADS, blk, LANE), lambda b, i: (b, 0, i, 0))
    return pl.pallas_call(
        functools.partial(_fox_prep_kernel, blk=blk),
        out_shape=(jax.ShapeDtypeStruct((n_batch * seq, LANE), F32), xshape, xshape),
        grid=(n_batch, n_i),
        in_specs=[pl.BlockSpec((1, blk, LANE), lambda b, i: (F_SEG, b * n_i + i, F_LANEBLK)),
                  pl.BlockSpec((1, LANE), lambda b, i: (0, 0))],
        out_specs=(pl.BlockSpec((blk, LANE), lambda b, i: (b * n_i + i, 0)), xspec, xspec),
        scratch_shapes=[pltpu.VMEM((SUBLANE, LANE), F32)],
        compiler_params=_cparams(("parallel", "arbitrary")),
        name="fox_prep",
    )(zc, bf_row)


HEADS_PER_STEP = 6
HEAD_GROUP = 3


def _attn_tiles(qs, k_ref, kx_ref, v_ref, b0, nb, bias_fn, state):
    m_ref, l_ref, acc_ref = state
    start = pl.multiple_of(b0 * MOBA_BLOCK, MOBA_BLOCK)
    tk = nb * MOBA_BLOCK
    hp = len(qs)
    scores = {}

    def logits(heads):
        for i in heads:
            kt = k_ref[0, 0, i, pl.ds(start, tk), :]
            if kx_ref is not None:
                kt = jnp.concatenate([kt, kx_ref[0, i, pl.ds(start, tk), :]], axis=1)
            scores[i] = bias_fn(lax.dot_general(kt, qs[i], NT_DIMS, preferred_element_type=F32), i, b0)

    def update(heads):
        probs = {}
        for i in heads:
            m = m_ref[i]
            m_new = jnp.maximum(m, jnp.max(scores[i], axis=0, keepdims=True))
            alpha = jnp.exp2(m - m_new)
            p = jnp.exp2(scores.pop(i) - m_new)
            m_ref[i] = m_new
            l_ref[i] = alpha * l_ref[i] + jnp.sum(p, axis=0, keepdims=True)
            probs[i] = (p.astype(BF16), alpha)
        for i in heads:
            p, alpha = probs[i]
            vt = v_ref[0, 0, i, pl.ds(start, tk), :]
            pv = lax.dot_general(vt, p, TN_DIMS, preferred_element_type=F32)
            acc_ref[i] = alpha * acc_ref[i] + pv

    groups = [range(i, min(i + HEAD_GROUP, hp)) for i in range(0, hp, HEAD_GROUP)]
    logits(groups[0])
    for gi, heads in enumerate(groups):
        if gi + 1 < len(groups):
            logits(groups[gi + 1])
        update(heads)


def _attn_state_shapes(tq, hp):
    return [pltpu.VMEM((hp, 1, tq), F32), pltpu.VMEM((hp, 1, tq), F32), pltpu.VMEM((hp, HEAD_DIM, tq), F32)]


def _attn_init(state):
    m_ref, l_ref, acc_ref = state
    m_ref[...] = jnp.full(m_ref.shape, NEG, F32)
    l_ref[...] = jnp.zeros(l_ref.shape, F32)
    acc_ref[...] = jnp.zeros(acc_ref.shape, F32)


def _attn_finish(o_ref, state):
    m_ref, l_ref, acc_ref = state
    for i in range(acc_ref.shape[0]):
        o_ref[0, i] = (acc_ref[i] / l_ref[i]).T.astype(o_ref.dtype)


PAST_BLOCKS = 4


def _attn_drive(j, tiles):
    jf = jnp.maximum(j - 1, 0)

    @pl.loop(0, jf // PAST_BLOCKS)
    def _(n):
        tiles(n * PAST_BLOCKS, PAST_BLOCKS, False)

    nb = PAST_BLOCKS // 2
    while nb >= 1:
        @pl.when(jf % (2 * nb) >= nb)
        def _(nb=nb):
            tiles((jf // (2 * nb)) * (2 * nb), nb, False)
        nb //= 2

    @pl.when(j > 0)
    def _():
        tiles(j - 1, 2, True)

    @pl.when(j == 0)
    def _():
        tiles(0, 1, True)


def _fox_prompt_kernel(idx, q_ref, qx_ref, k_ref, kx_ref, v_ref, o_ref, *state, tq, hp):
    j = idx[2]
    qs = [jnp.concatenate([q_ref[0, 0, i], qx_ref[0, i]], axis=1) for i in range(hp)]

    def tiles(b0, nb, last):
        def bias(s, i, b0):
            if not last:
                return s
            key = lax.broadcasted_iota(jnp.int32, s.shape, 0) - (nb - 1) * tq
            qry = lax.broadcasted_iota(jnp.int32, s.shape, 1)
            return jnp.where(key <= qry, s, NEG)
        _attn_tiles(qs, k_ref, kx_ref, v_ref, b0, nb, bias, state)

    _attn_init(state)
    _attn_drive(j, tiles)
    _attn_finish(o_ref, state)


def _head_spec(seg, hp, rows, row_map, **kw):
    return pl.BlockSpec((1, 1, hp, rows, HEAD_DIM), lambda b, g, j: (seg, b, g, row_map(j), 0), **kw)


_RESIDENT = dict(pipeline_mode=pl.Buffered(1))


def fox_prompt(q, k, v, kx, qx, n_batch, seq, tq=256, hp=HEADS_PER_STEP):
    nq = seq // tq
    return _Call(
        "fox_prompt", functools.partial(_fox_prompt_kernel, tq=tq, hp=hp), (n_batch, N_HEADS // hp, nq),
        [_head_spec(q[1], hp, tq, lambda j: j),
         pl.BlockSpec((1, hp, tq, LANE), lambda b, g, j: (b, g, j, 0)),
         _head_spec(k[1], hp, seq, lambda j: 0, **_RESIDENT),
         pl.BlockSpec((1, hp, seq, LANE), lambda b, g, j: (b, g, 0, 0), **_RESIDENT),
         _head_spec(v[1], hp, seq, lambda j: 0, **_RESIDENT)],
        [pl.BlockSpec((1, hp, tq, HEAD_DIM), lambda b, g, j: (b, g, j, 0))],
        [jax.ShapeDtypeStruct((n_batch, N_HEADS, seq, HEAD_DIM), BF16)],
        [q[0], qx, k[0], kx, v[0]], {}, None, tuple(_attn_state_shapes(tq, hp)))


def _top_blocks(gate, n_valid, axis):
    n_blk = gate.shape[axis]
    blk = lax.broadcasted_iota(jnp.int32, gate.shape, axis)
    blk_f = blk.astype(F32)
    g = jnp.where(blk < n_valid, gate, -jnp.inf)
    sel = jnp.zeros(gate.shape, jnp.bool_)
    for _ in range(MOBA_TOPK):
        mx = jnp.max(g, axis=axis, keepdims=True)
        cand = (g == mx) & (mx > -jnp.inf)
        first = jnp.min(jnp.where(cand, blk_f, float(n_blk)), axis=axis, keepdims=True)
        pick = blk_f == first
        sel = sel | pick
        g = jnp.where(pick, -jnp.inf, g)
    return sel


def _moba_prompt_kernel(idx, rb_ref, q_ref, k_ref, v_ref, mb_ref, tab_ref, o_ref, *state, tq, hp):
    g, j = idx[1], idx[2]
    qs = [q_ref[0, 0, i] for i in range(hp)]
    fars = [rb_ref[N_BUCKETS - 1, g * hp + i] * LOG2E for i in range(hp)]

    def tiles(b0, nb, last):
        def bias(s, i, b0):
            rows = lambda r: s[r * tq:(r + 1) * tq]
            if not last:
                return jnp.concatenate([rows(r) + mb_ref[0, i, pl.ds(b0 + r, 1), :] for r in range(nb)], axis=0)
            own = rows(nb - 1) + tab_ref[i, 0]
            if nb == 1:
                return own
            prev = rows(0) + tab_ref[i, 1] + (mb_ref[0, i, pl.ds(b0, 1), :] - fars[i])
            return jnp.concatenate([prev, own], axis=0)
        _attn_tiles(qs, k_ref, None, v_ref, b0, nb, bias, state)

    _attn_init(state)
    _attn_drive(j, tiles)
    _attn_finish(o_ref, state)


def moba_prompt(q, k, v, mb, tab, rel_bias, n_batch, seq, hp=HEADS_PER_STEP):
    tq = MOBA_BLOCK
    nq = seq // tq
    return _Call(
        "moba_prompt", functools.partial(_moba_prompt_kernel, tq=tq, hp=hp), (n_batch, N_HEADS // hp, nq),
        [pl.BlockSpec(memory_space=pltpu.SMEM),
         _head_spec(q[1], hp, tq, lambda j: j),
         _head_spec(k[1], hp, seq, lambda j: 0, **_RESIDENT),
         _head_spec(v[1], hp, seq, lambda j: 0, **_RESIDENT),
         pl.BlockSpec((1, hp, nq, tq), lambda b, g, j: (b, g, 0, j)),
         pl.BlockSpec((hp, 2, tq, tq), lambda b, g, j: (g, 0, 0, 0), **_RESIDENT)],
        [pl.BlockSpec((1, hp, tq, HEAD_DIM), lambda b, g, j: (b, g, j, 0))],
        [jax.ShapeDtypeStruct((n_batch, N_HEADS, seq, HEAD_DIM), BF16)],
        [rel_bias, q[0], k[0], v[0], mb, tab], {}, None, tuple(_attn_state_shapes(tq, hp)))


def _pad_rows(x, rows):
    return jnp.concatenate([x, jnp.zeros((rows - x.shape[0], x.shape[1]), x.dtype)], axis=0)


def _lanes(a, b):
    return jnp.concatenate([a, b], axis=1)


def _rows(a, b):
    return jnp.concatenate([a, b], axis=0)


def _pair_queries(q_ref, g, scale):
    q0, q1 = q_ref[0, 0, 2 * g], q_ref[0, 0, 2 * g + 1]
    z = jnp.zeros_like(q0)
    return _rows(_lanes(q0, z), _lanes(z, q1)) * scale


def _pair_tile(pages, p0, g):
    t = lambda p: _lanes(pages[p][0, 0, 2 * g].astype(BF16), pages[p][0, 0, 2 * g + 1].astype(BF16))
    return _rows(t(p0), t(p0 + 1))


def _pair_new(x_ref, g, page):
    return _lanes(_pad_rows(x_ref[0, 0, 2 * g], page), _pad_rows(x_ref[0, 0, 2 * g + 1], page)).astype(BF16)


def _pair_bias(top, bottom, t_new):
    return _rows(jnp.broadcast_to(top, (t_new, top.shape[1])), jnp.broadcast_to(bottom, (t_new, bottom.shape[1])))


def _pair_probs(s_list, s_new):
    m = s_list[0]
    for s in s_list[1:]:
        m = jnp.maximum(m, s)
    m = jnp.max(jnp.maximum(jnp.maximum(m[:, :LANE], m[:, LANE:]), s_new), axis=1, keepdims=True)
    l = None
    p_list = []
    for s in s_list:
        p = jnp.exp(s - m)
        l = p if l is None else l + p
        p_list.append(p.astype(BF16))
    p = jnp.exp(s_new - m)
    l = jnp.sum(l[:, :LANE] + l[:, LANE:] + p, axis=1, keepdims=True)
    return p_list, p.astype(BF16), l


def _pair_pv(o_ref, g, probs, v_pages, v_ref, t_new, page):
    p_list, p_new, l = probs
    acc = jnp.dot(p_new, _pair_new(v_ref, g, page), preferred_element_type=F32)
    for n, p in enumerate(p_list):
        acc = acc + jnp.dot(p, _pair_tile(v_pages, 2 * n, g), preferred_element_type=F32)
    out = acc / l
    o_ref[0, 2 * g] = out[:t_new, :HEAD_DIM]
    o_ref[0, 2 * g + 1] = out[t_new:, HEAD_DIM:]


def _fox_sample_kernel(pt_ref, q_ref, k_ref, v_ref, f_ref, bf_ref, *rest, t_new, n_pages, page):
    k_pages = rest[:n_pages]
    v_pages = rest[n_pages:2 * n_pages]
    lf_pages = rest[2 * n_pages:3 * n_pages]
    o_ref, lf_ref = rest[3 * n_pages:]

    n_r = n_pages * SUBLANE
    lane = lax.broadcasted_iota(jnp.int32, (n_r, page), 1)
    srow = lax.broadcasted_iota(jnp.int32, (n_r, page), 0)
    lfc = jnp.concatenate([r[0, 0] for r in lf_pages], axis=0)
    incl = lfc
    step = 1
    while step < page:
        incl = incl + jnp.where(lane + step < page, pltpu.roll(incl, page - step, axis=1), 0.0)
        step *= 2
    tot = jnp.broadcast_to(jnp.sum(jnp.where(lane == 0, incl, 0.0), axis=1, keepdims=True), (n_r, page))
    run = tot
    step = SUBLANE
    while step < n_r:
        run = run + jnp.where(srow + step < n_r, pltpu.roll(run, n_r - step, axis=0), 0.0)
        step *= 2
    suf = (incl - lfc) + (run - tot)

    flane = lax.broadcasted_iota(jnp.int32, (t_new, LANE), 1)
    lf_new = jnp.where(flane < N_HEADS, _log_sigmoid(f_ref[0] + bf_ref[...]), 0.0)
    lf_ref[...] = lf_new
    cum = _pad_rows(lf_new, LANE).T
    lane2 = lax.broadcasted_iota(jnp.int32, (LANE, LANE), 1)
    step = 1
    while step < t_new:
        cum = cum + jnp.where(lane2 >= step, pltpu.roll(cum, step, axis=1), 0.0)
        step *= 2
    r = lax.broadcasted_iota(jnp.int32, (2 * t_new, page), 0) % t_new
    c = lax.broadcasted_iota(jnp.int32, (2 * t_new, page), 1)
    suf_row = lambda p, h: suf[p * SUBLANE + h:p * SUBLANE + h + 1, :]

    pairs = range(N_HEADS // 2)
    logits = []
    for g in pairs:
        h0, h1 = 2 * g, 2 * g + 1
        qp = _pair_queries(q_ref, g, SCALE).astype(BF16)
        s_list = []
        for p0 in range(0, n_pages, 2):
            s = lax.dot_general(qp, _pair_tile(k_pages, p0, g), NT_DIMS, preferred_element_type=F32)
            bias = _pair_bias(_lanes(suf_row(p0, h0), suf_row(p0 + 1, h0)),
                              _lanes(suf_row(p0, h1), suf_row(p0 + 1, h1)), t_new)
            s_list.append(s + bias)
        s_new = lax.dot_general(qp, _pair_new(k_ref, g, page), NT_DIMS, preferred_element_type=F32)
        s_new = jnp.where(c <= r, s_new - _pair_bias(cum[h0:h0 + 1, :], cum[h1:h1 + 1, :], t_new), NEG)
        logits.append((s_list, s_new))
    probs = [_pair_probs(*logits[g]) for g in pairs]
    for g in pairs:
        _pair_pv(o_ref, g, probs[g], v_pages, v_ref, t_new, page)


def _page_specs(n_pages, layer, shape, b0):
    def spec(p):
        return pl.BlockSpec((1, 1) + shape, lambda b, pt: (layer, pt[b0 + b, p]) + (0,) * len(shape))
    return [spec(p) for p in range(n_pages)]


def _new_spec(seg, t_new, b0):
    return pl.BlockSpec((1, 1, N_HEADS, t_new, HEAD_DIM), lambda b, pt: (seg, b0 + b, 0, 0, 0))


def fox_sample(q, k, v, zc, bf_row, pool_k, pool_v, pool_lft, page_table, layer, t_new, b0, n):
    n_pages = page_table.shape[1]
    page = pool_k.shape[3]
    in_specs = ([_new_spec(q[1], t_new, b0), _new_spec(k[1], t_new, b0), _new_spec(v[1], t_new, b0),
                 pl.BlockSpec((1, t_new, LANE), lambda b, pt: (F_SEG, b0 + b, F_LANEBLK)),
                 pl.BlockSpec((1, LANE), lambda b, pt: (0, 0))]
                + _page_specs(n_pages, layer, (N_HEADS, page, HEAD_DIM), b0)
                + _page_specs(n_pages, layer, (N_HEADS, page, HEAD_DIM), b0)
                + _page_specs(n_pages, layer, (SUBLANE, page), b0))
    return _Call(
        "fox_sample", functools.partial(_fox_sample_kernel, t_new=t_new, n_pages=n_pages, page=page), (n,), in_specs,
        [pl.BlockSpec((1, N_HEADS, t_new, HEAD_DIM), lambda b, pt: (b, 0, 0, 0)),
         pl.BlockSpec((t_new, LANE), lambda b, pt: (b, 0))],
        [jax.ShapeDtypeStruct((n, N_HEADS, t_new, HEAD_DIM), F32), jax.ShapeDtypeStruct((n * t_new, LANE), F32)],
        [q[0], k[0], v[0], zc, bf_row] + [pool_k] * n_pages + [pool_v] * n_pages + [pool_lft] * n_pages, {},
        page_table)


def _moba_sample_kernel(pt_ref, q_ref, k_ref, v_ref, tab_ref, *rest, t_new, n_pages, page):
    k_pages = rest[:n_pages]
    v_pages = rest[n_pages:2 * n_pages]
    o_ref = rest[2 * n_pages]
    assert MOBA_BLOCK == 2 * page
    n_blk = n_pages // 2

    def block_means(h):
        sums = [jnp.sum(k_pages[2 * n][0, 0, h] + k_pages[2 * n + 1][0, 0, h], axis=0, keepdims=True)
                for n in range(n_blk)]
        return _pad_rows(jnp.concatenate(sums, axis=0) * (1.0 / MOBA_BLOCK), LANE)

    pairs = range(N_HEADS // 2)
    logits = []
    for g in pairs:
        q32 = _pair_queries(q_ref, g, 1.0)
        qp = (q32 * SCALE).astype(BF16)
        kmean = _lanes(block_means(2 * g), block_means(2 * g + 1))
        gate = lax.dot_general(q32, kmean, NT_DIMS, precision=HIGHEST, preferred_element_type=F32)
        sel = _top_blocks(gate, n_blk, 1)
        mask = jnp.where(sel, 0.0, NEG)
        rows = slice(2 * g * t_new, (2 * g + 2) * t_new)
        far = tab_ref[2, rows, :]
        s_list = []
        for n in range(n_blk):
            s = lax.dot_general(qp, _pair_tile(k_pages, 2 * n, g), NT_DIMS, preferred_element_type=F32)
            bias = _lanes(far, tab_ref[0, rows, :] if n == n_blk - 1 else far)
            s_list.append(s + bias + mask[:, n:n + 1])
        s_new = lax.dot_general(qp, _pair_new(k_ref, g, page), NT_DIMS, preferred_element_type=F32)
        logits.append((s_list, s_new + tab_ref[1, rows, :]))
    probs = [_pair_probs(*logits[g]) for g in pairs]
    for g in pairs:
        _pair_pv(o_ref, g, probs[g], v_pages, v_ref, t_new, page)


def moba_sample(q, k, v, tab, pool_k, pool_v, page_table, layer, t_new, b0, n):
    n_pages = page_table.shape[1]
    page = pool_k.shape[3]
    rows = N_HEADS * t_new
    in_specs = ([_new_spec(q[1], t_new, b0), _new_spec(k[1], t_new, b0), _new_spec(v[1], t_new, b0),
                 pl.BlockSpec((3, rows, LANE), lambda b, pt: (0, 0, 0))]
                + _page_specs(n_pages, layer, (N_HEADS, page, HEAD_DIM), b0)
                + _page_specs(n_pages, layer, (N_HEADS, page, HEAD_DIM), b0))
    return _Call(
        "moba_sample", functools.partial(_moba_sample_kernel, t_new=t_new, n_pages=n_pages, page=page), (n,),
        in_specs, [pl.BlockSpec((1, N_HEADS, t_new, HEAD_DIM), lambda b, pt: (b, 0, 0, 0))],
        [jax.ShapeDtypeStruct((n, N_HEADS, t_new, HEAD_DIM), F32)],
        [q[0], k[0], v[0], tab] + [pool_k] * n_pages + [pool_v] * n_pages, {}, page_table)


def _silu(g):
    return g / (1.0 + jnp.exp(-g))


def _token_major(v, tm):
    return jnp.concatenate([v[:, h].reshape(tm, HEAD_DIM) for h in range(N_HEADS)], axis=1)


def _combine_kernel(*refs, tm, seq_tiles, seq_rows, alpha):
    (x_ref, oa_ref, of_ref, ga_ref, gf_ref, b0, b1, c0, c1, h0, h1, g0, g1) = refs[:13]
    if seq_rows is None:
        pc0, pc1, ph0, ph1 = refs[13:17]
        rest = refs[17:]
    else:
        hist1_ref, hist2_ref = refs[13:15]
        rest = refs[15:]
    wout_ref, cw_ref, lng_ref, lnb_ref, y_ref, yb_ref, u_ref = rest
    cat = lambda a, b: jnp.concatenate([a[0], b[0]], axis=1)
    u = cat(c0, c1) * cat(h0, h1)
    if seq_rows is None:
        keep = (pl.program_id(0) % seq_tiles != 0).astype(F32)
        uprev = cat(pc0, pc1) * cat(ph0, ph1) * keep
        ext = jnp.concatenate([uprev, u], axis=0)
        u1 = pltpu.roll(ext, 1, axis=0)[SUBLANE:]
        u2 = pltpu.roll(ext, 2, axis=0)[SUBLANE:]
        u_ref[0] = u[tm - SUBLANE:, :]
    else:
        r = lax.broadcasted_iota(jnp.int32, (tm, D_CONV), 0) % seq_rows
        u1 = jnp.where(r >= 1, pltpu.roll(u, 1, axis=0), hist1_ref[...])
        u2 = jnp.where(r >= 2, pltpu.roll(u, 2, axis=0), hist2_ref[...])
        u_ref[...] = u
    conv = cw_ref[0:1, :] * u2 + cw_ref[1:2, :] * u1 + cw_ref[2:3, :] * u
    oc = (cat(b0, b1) * conv * _silu(cat(g0, g1))).astype(BF16)
    oa = (_token_major(oa_ref[...].astype(F32), tm) * _silu(_token_major(ga_ref[0], tm))).astype(BF16)
    of = (_token_major(of_ref[...].astype(F32), tm) * _silu(_token_major(gf_ref[0], tm))).astype(BF16)
    proj = (jnp.dot(oa, wout_ref[0:D_ATT, :], preferred_element_type=F32)
            + jnp.dot(of, wout_ref[D_ATT:2 * D_ATT, :], preferred_element_type=F32)
            + jnp.dot(oc, wout_ref[2 * D_ATT:, :], preferred_element_type=F32))
    res = alpha * x_ref[...] + proj
    mu = jnp.mean(res, axis=-1, keepdims=True)
    cen = res - mu
    var = jnp.mean(cen * cen, axis=-1, keepdims=True)
    y = cen * lax.rsqrt(var + LN_EPS) * lng_ref[...] + lnb_ref[...]
    y_ref[...] = y
    yb_ref[...] = y.astype(BF16)


def combine(x, oa, of, zg, zc, wout, cw, lng, lnb, alpha, tm, hist=None, gates=(0, 1)):
    t = x.shape[0]
    n_seq, _, seq_len, _ = oa.shape
    assert t == n_seq * seq_len and t % tm == 0
    if seq_len >= tm:
        assert seq_len % tm == 0
        sb, sl, per_seq = 1, tm, seq_len // tm
        hmap = lambda i: (i // per_seq, 0, i % per_seq, 0)
    else:
        assert tm % seq_len == 0
        sb, sl, per_seq = tm // seq_len, seq_len, 1
        hmap = lambda i: (i, 0, 0, 0)
    hblk = (sb, N_HEADS, sl, HEAD_DIM)
    zblk = lambda sc: pl.BlockSpec((1, tm, CBLK), lambda i: (sc[0], i, sc[1]))
    pieces = [sc for idx in (0, 1, 2, 3) for sc in _conv_piece(idx)]
    in_specs = [pl.BlockSpec((tm, D_MODEL), lambda i: (i, 0)),
                pl.BlockSpec(hblk, hmap),
                pl.BlockSpec(hblk, hmap),
                pl.BlockSpec((1,) + hblk, lambda i: (gates[0],) + hmap(i)),
                pl.BlockSpec((1,) + hblk, lambda i: (gates[1],) + hmap(i))] + [zblk(sc) for sc in pieces]
    args = [x, oa, of, zg, zg] + [zc] * 8
    if hist is None:
        seq_rows = None
        tb = tm // SUBLANE
        prev = lambda sc: pl.BlockSpec((1, SUBLANE, CBLK), lambda i: (sc[0], jnp.maximum(i * tb - 1, 0), sc[1]))
        in_specs += [prev(sc) for sc in pieces[2:6]]
        args += [zc] * 4
        u_shape = jax.ShapeDtypeStruct((n_seq, SUBLANE, D_CONV), F32)
        u_spec = pl.BlockSpec((1, SUBLANE, D_CONV), lambda i: (i // per_seq, 0, 0))
    else:
        seq_rows = seq_len
        in_specs += [pl.BlockSpec((tm, D_CONV), lambda i: (i, 0))] * 2
        args += list(hist)
        u_shape = jax.ShapeDtypeStruct((t, D_CONV), F32)
        u_spec = pl.BlockSpec((tm, D_CONV), lambda i: (i, 0))
    in_specs += [pl.BlockSpec((D_MODEL, D_MODEL), lambda i: (0, 0)),
                 pl.BlockSpec((SUBLANE, D_CONV), lambda i: (0, 0)),
                 pl.BlockSpec((1, D_MODEL), lambda i: (0, 0)),
                 pl.BlockSpec((1, D_MODEL), lambda i: (0, 0))]
    args += [wout, cw, lng, lnb]
    return pl.pallas_call(
        functools.partial(_combine_kernel, tm=tm, seq_tiles=per_seq, seq_rows=seq_rows, alpha=alpha),
        out_shape=(jax.ShapeDtypeStruct((t, D_MODEL), F32), jax.ShapeDtypeStruct((t, D_MODEL), BF16), u_shape),
        grid=(t // tm,),
        in_specs=in_specs,
        out_specs=(pl.BlockSpec((tm, D_MODEL), lambda i: (i, 0)), pl.BlockSpec((tm, D_MODEL), lambda i: (i, 0)),
                   u_spec),
        compiler_params=_cparams(("arbitrary",)),
        name="combine_prompt" if hist is None else "combine_sample",
    )(*args)


def kernel(x_prompt, x_sample, cache_moba_k, cache_moba_v, cache_fox_k, cache_fox_v, cache_fox_logf,
           state_conv, page_table, rel_bias, w_in, b_f, conv_w, w_out, ln_g, ln_b):
    depth = w_in.shape[0]
    alpha = (2 * depth) ** 0.25
    n_batch, seq, _ = x_prompt.shape
    n_seq, t_new, _ = x_sample.shape
    n_phys, page = cache_moba_k.shape[1:3]
    past = page_table.shape[1] * page
    assert page == LANE and past % MOBA_BLOCK == 0 and t_new == SUBLANE and seq % MOBA_BLOCK == 0
    assert past >= MAX_DISTANCE and w_in.shape[2] == N_MAIN + N_HEADS + 4 * D_CONV

    xp = x_prompt.reshape(n_batch * seq, D_MODEL)
    xs = x_sample.reshape(n_seq * t_new, D_MODEL)
    xpb, xsb = xp.astype(BF16), xs.astype(BF16)
    pool = lambda c: jnp.swapaxes(c, 2, 3)
    pk_a, pv_a, pk_f, pv_f = pool(cache_moba_k), pool(cache_moba_v), pool(cache_fox_k), pool(cache_fox_v)
    pool_lft = jnp.pad(jnp.swapaxes(cache_fox_logf, 2, 3), ((0, 0), (0, 0), (0, SUBLANE - N_HEADS), (0, 0)))
    tab_p = prompt_bias_tables(rel_bias)
    tab_s = sample_bias_tables(rel_bias, t_new, past)
    tm_s = min(256, n_seq * t_new)

    outs = {k: [] for k in ("p_lf", "p_cv", "s_lf", "s_cv")}
    kv_segs = {"ka": SEG_KA, "va": SEG_VA, "kf": SEG_KF, "vf": SEG_VF}
    p_kv = dict.fromkeys(kv_segs)
    s_kv = {name: [] for name in kv_segs}
    for layer in range(depth):
        wt_h = jnp.swapaxes(w_in[layer], 0, 1).astype(BF16)
        n_f = N_MAIN + N_HEADS
        wt_c = jnp.concatenate(
            [wt_h[n_f:], wt_h[N_MAIN:n_f],
             jnp.zeros((N_CONV_SEG * SEG - 4 * D_CONV - N_HEADS, D_MODEL), BF16)], axis=0)
        wo = w_out[layer].astype(BF16)
        bf_row = jnp.pad(b_f[layer], (0, LANE - N_HEADS)).reshape(1, LANE)
        cw = jnp.pad(conv_w[layer], ((0, SUBLANE - CONV_WIDTH), (0, 0)))
        lng, lnb = ln_g[layer].reshape(1, D_MODEL), ln_b[layer].reshape(1, D_MODEL)

        (zhs,) = run(inproj_heads(xsb, wt_h, tuple(range(N_HEAD_SEG)), n_seq=n_seq, seq_len=t_new, tm=n_seq * t_new))
        (zcs,) = run(inproj_conv(xsb, wt_c, tm=n_seq * t_new))
        for name, seg in kv_segs.items():
            s_kv[name].append(zhs[seg])

        sample = {
            "fox": functools.partial(fox_sample, (zhs, SEG_QF), (zhs, SEG_KF), (zhs, SEG_VF), zcs, bf_row,
                                     pk_f, pv_f, pool_lft, page_table, layer, t_new),
            "moba": functools.partial(moba_sample, (zhs, SEG_QA), (zhs, SEG_KA), (zhs, SEG_VA), tab_s,
                                      pk_a, pv_a, page_table, layer, t_new)}
        next_seq = {"fox": 0, "moba": 0}
        chunks = {"fox": [], "moba": []}

        def run_with_sample(call):
            steps = math.prod(call.grid)
            for kind in sample:
                if n_seq - next_seq[kind] >= steps:
                    outs_a, outs_b = run_zipped(call, sample[kind](next_seq[kind], steps))
                    next_seq[kind] += steps
                    chunks[kind].append(outs_b)
                    return outs_a
            return run(call)

        proj = functools.partial(inproj_heads, xpb, wt_h, n_seq=n_batch, seq_len=seq, tm=512)
        zq, zqb = run_with_sample(proj((SEG_QA, SEG_QF), bf16_scale=QSCALE))
        (zg,) = run_with_sample(proj((SEG_GA, SEG_GF)))
        kvb = {}
        for name, seg in kv_segs.items():
            p_kv[name], kvb[name] = run_with_sample(proj((seg,), bf16_scale=1.0, stack=(depth, layer, p_kv[name])))
        (zc,) = run_with_sample(inproj_conv(xpb, wt_c, tm=512))

        mb = moba_select((p_kv["ka"], layer), (zq, 0), rel_bias, n_batch, seq)
        (oa,) = run_with_sample(
            moba_prompt((zqb, 0), (kvb["ka"], 0), (kvb["va"], 0), mb, tab_p, rel_bias, n_batch, seq))
        lf, kx, qx = fox_prep(zc, bf_row, n_batch, seq)
        (of,) = run_with_sample(fox_prompt((zqb, 1), (kvb["kf"], 0), (kvb["vf"], 0), kx, qx, n_batch, seq))
        for kind in sample:
            if next_seq[kind] < n_seq:
                chunks[kind].append(run(sample[kind](next_seq[kind], n_seq - next_seq[kind])))
        ofs = jnp.concatenate([c[0] for c in chunks["fox"]], axis=0)
        lfs = jnp.concatenate([c[1] for c in chunks["fox"]], axis=0)
        oas = jnp.concatenate([c[0] for c in chunks["moba"]], axis=0)
        xp, xpb, ulast = combine(xp, oa, of, zg, zc, wo, cw, lng, lnb, alpha, tm=256)
        outs["p_lf"].append(lf[:, :N_HEADS].reshape(n_batch, seq, N_HEADS))
        outs["p_cv"].append(ulast[:, SUBLANE - (CONV_WIDTH - 1):, :])

        st = state_conv[layer]
        zero = jnp.zeros((n_seq, t_new, D_CONV), F32)
        hist1 = zero.at[:, 0].set(st[:, 1]).reshape(n_seq * t_new, D_CONV)
        hist2 = zero.at[:, 0].set(st[:, 0]).at[:, 1].set(st[:, 1]).reshape(n_seq * t_new, D_CONV)
        xs, xsb, us = combine(xs, oas, ofs, zhs, zcs, wo, cw, lng, lnb, alpha, tm=tm_s, hist=(hist1, hist2),
                              gates=(SEG_GA, SEG_GF))
        outs["s_lf"].append(lfs[:, :N_HEADS].reshape(n_seq, t_new, N_HEADS))
        outs["s_cv"].append(us.reshape(n_seq, t_new, D_CONV)[:, t_new - (CONV_WIDTH - 1):, :])

    st = {k: jnp.stack(v) for k, v in outs.items()}
    tok_major = lambda a: jnp.swapaxes(a, 2, 3)
    return (xp.reshape(n_batch, seq, D_MODEL), xs.reshape(n_seq, t_new, D_MODEL),
            tok_major(p_kv["ka"]), tok_major(p_kv["va"]), tok_major(p_kv["kf"]), tok_major(p_kv["vf"]),
            st["p_lf"], st["p_cv"],
            *(tok_major(jnp.stack(s_kv[name])) for name in ("ka", "va", "kf", "vf")),
            st["s_lf"], st["s_cv"])
```

```python
import functools
import math
from typing import Callable, NamedTuple

import jax
import jax.numpy as jnp
from jax import lax
from jax.experimental import pallas as pl
from jax.experimental.pallas import tpu as pltpu

F32 = jnp.float32
BF16 = jnp.bfloat16
HIGHEST = lax.Precision.HIGHEST

D_MODEL = 2048
HEAD_DIM = 128
N_HEADS = 6
D_ATT = N_HEADS * HEAD_DIM
D_CONV = 512
CONV_WIDTH = 3
MOBA_BLOCK = 256
MOBA_TOPK = 3
N_BUCKETS = 32
MAX_DISTANCE = 128
LN_EPS = 1e-5
SCALE = HEAD_DIM ** -0.5
LOG2E = math.log2(math.e)
QSCALE = SCALE * LOG2E

SEG = 768
SEG_QA, SEG_KA, SEG_VA, SEG_GA, SEG_QF, SEG_KF, SEG_VF, SEG_GF = range(8)
N_HEAD_SEG = 8
N_CONV_SEG = 3
N_MAIN = N_HEAD_SEG * SEG
CBLK = 256
LANE = 128
SUBLANE = 8
F_SEG, F_LANEBLK = (4 * D_CONV) // SEG, ((4 * D_CONV) % SEG) // LANE
NEG = -1e30
VMEM_LIMIT = 56 * 1024 * 1024

NT_DIMS = (((1,), (1,)), ((), ()))
TN_DIMS = (((0,), (0,)), ((), ()))


def _cparams(sem):
    return pltpu.CompilerParams(dimension_semantics=sem, vmem_limit_bytes=VMEM_LIMIT)


class _Call(NamedTuple):
    name: str
    body: Callable
    grid: tuple
    in_specs: list
    out_specs: list
    out_shape: list
    args: list
    aliases: dict
    page_table: object = None
    scratch: tuple = ()


def run(c):
    if c.page_table is None:
        body = lambda *refs: c.body(tuple(pl.program_id(d) for d in range(len(c.grid))), *refs)
        return pl.pallas_call(
            body, out_shape=tuple(c.out_shape), grid=c.grid, in_specs=c.in_specs, out_specs=tuple(c.out_specs),
            scratch_shapes=list(c.scratch), input_output_aliases=c.aliases,
            compiler_params=_cparams(("arbitrary",) * len(c.grid)), name=c.name,
        )(*c.args)
    assert not c.aliases and not c.scratch
    return pl.pallas_call(
        c.body, out_shape=tuple(c.out_shape),
        grid_spec=pltpu.PrefetchScalarGridSpec(num_scalar_prefetch=1, grid=c.grid, in_specs=c.in_specs,
                                               out_specs=tuple(c.out_specs)),
        compiler_params=_cparams(("arbitrary",) * len(c.grid)), name=c.name,
    )(c.page_table, *c.args)


def run_zipped(a, b):
    n, n_b = math.prod(a.grid), b.grid[0]
    assert a.page_table is None and b.page_table is not None and len(b.grid) == 1 and n_b <= n

    def a_idx(k):
        idx = []
        for size in reversed(a.grid):
            idx.append(k % size)
            k = k // size
        return tuple(reversed(idx))

    def respec(sp, fn):
        if sp.block_shape is None:
            return sp
        return pl.BlockSpec(sp.block_shape, fn(sp.index_map), memory_space=sp.memory_space,
                            pipeline_mode=sp.pipeline_mode)

    for_a = lambda m: (lambda k, pt: m(*a_idx(k)))
    for_b = lambda m: (lambda k, pt: m(jnp.minimum(k, n_b - 1), pt))
    na_in, nb_in, na_out = len(a.in_specs), len(b.in_specs), len(a.out_specs)
    nb_out = len(b.out_specs)

    def body(pt_ref, *refs):
        ins_a, refs = refs[:na_in], refs[na_in:]
        ins_b, refs = refs[:nb_in], refs[nb_in:]
        outs_a, refs = refs[:na_out], refs[na_out:]
        outs_b, scratch_a = refs[:nb_out], refs[nb_out:]
        a.body(a_idx(pl.program_id(0)), *ins_a, *outs_a, *scratch_a)
        if n_b == n:
            b.body(pt_ref, *ins_b, *outs_b)
        else:
            pl.when(pl.program_id(0) < n_b)(lambda: b.body(pt_ref, *ins_b, *outs_b))

    outs = pl.pallas_call(
        body, out_shape=tuple(a.out_shape) + tuple(b.out_shape),
        grid_spec=pltpu.PrefetchScalarGridSpec(
            num_scalar_prefetch=1, grid=(n,),
            in_specs=[respec(sp, for_a) for sp in a.in_specs] + [respec(sp, for_b) for sp in b.in_specs],
            out_specs=tuple([respec(sp, for_a) for sp in a.out_specs] + [respec(sp, for_b) for sp in b.out_specs]),
            scratch_shapes=list(a.scratch)),
        input_output_aliases={1 + i: o for i, o in a.aliases.items()},
        compiler_params=_cparams(("arbitrary",)), name=a.name + "_zip_" + b.name,
    )(b.page_table, *a.args, *b.args)
    return outs[:na_out], outs[na_out:]


def _conv_piece(idx):
    out = []
    for half in range(2):
        col = idx * D_CONV + half * CBLK
        out.append((col // SEG, (col % SEG) // CBLK))
    return out


def _inproj_heads_kernel(idx, x_ref, w_ref, *refs, sb, sl, n_in, bf16_scale):
    o_ref = refs[n_in]
    res = lax.dot_general(x_ref[...], w_ref[...], NT_DIMS, preferred_element_type=F32)
    for h in range(N_HEADS):
        o_ref[0, :, h] = res[:, h * HEAD_DIM:(h + 1) * HEAD_DIM].reshape(sb, sl, HEAD_DIM)
    if bf16_scale is not None:
        scaled = (res * bf16_scale).astype(BF16)
        for h in range(N_HEADS):
            refs[n_in + 1][0, :, h] = scaled[:, h * HEAD_DIM:(h + 1) * HEAD_DIM].reshape(sb, sl, HEAD_DIM)


def inproj_heads(xb, wt, segs, n_seq, seq_len, tm, bf16_scale=None, stack=None):
    t = xb.shape[0]
    assert t == n_seq * seq_len and t % tm == 0 and wt.shape[0] >= N_MAIN and wt.shape[1] == D_MODEL
    if seq_len >= tm:
        assert seq_len % tm == 0
        sb, sl, per_seq = 1, tm, seq_len // tm
        rows = lambda i: (i // per_seq, 0, i % per_seq, 0)
    else:
        assert tm % seq_len == 0
        sb, sl = tm // seq_len, seq_len
        rows = lambda i: (i, 0, 0, 0)
    blk = (1, sb, N_HEADS, sl, HEAD_DIM)
    tail = (n_seq, N_HEADS, seq_len, HEAD_DIM)

    def seg_of(j):
        s = segs[0]
        for idx in range(1, len(segs)):
            s = jnp.where(j == idx, segs[idx], s)
        return s

    in_specs = [pl.BlockSpec((tm, D_MODEL), lambda j, i: (i, 0)),
                pl.BlockSpec((SEG, D_MODEL), lambda j, i: (seg_of(j), 0))]
    args = [xb, wt]
    aliases = {}
    if stack is None:
        shapes = [jax.ShapeDtypeStruct((len(segs),) + tail, F32)]
        specs = [pl.BlockSpec(blk, lambda j, i: (j,) + rows(i))]
    else:
        depth, layer, prev = stack
        assert len(segs) == 1
        shapes = [jax.ShapeDtypeStruct((depth,) + tail, F32)]
        specs = [pl.BlockSpec(blk, lambda j, i: (layer,) + rows(i))]
        if prev is not None:
            in_specs.append(pl.BlockSpec(memory_space=pl.ANY))
            args.append(prev)
            aliases = {2: 0}
    if bf16_scale is not None:
        shapes.append(jax.ShapeDtypeStruct((len(segs),) + tail, BF16))
        specs.append(pl.BlockSpec(blk, lambda j, i: (j,) + rows(i)))
    body = functools.partial(_inproj_heads_kernel, sb=sb, sl=sl, n_in=len(args) - 2, bf16_scale=bf16_scale)
    return _Call("inproj_heads", body, (len(segs), t // tm), in_specs, specs, shapes, args, aliases)


def _inproj_conv_kernel(idx, x_ref, w_ref, o_ref):
    o_ref[0] = lax.dot_general(x_ref[...], w_ref[...], NT_DIMS, preferred_element_type=F32)


def inproj_conv(xb, wt, tm):
    t = xb.shape[0]
    assert t % tm == 0 and wt.shape == (N_CONV_SEG * SEG, D_MODEL)
    return _Call("inproj_conv", _inproj_conv_kernel, (N_CONV_SEG, t // tm),
                 [pl.BlockSpec((tm, D_MODEL), lambda j, i: (i, 0)),
                  pl.BlockSpec((SEG, D_MODEL), lambda j, i: (j, 0))],
                 [pl.BlockSpec((1, tm, SEG), lambda j, i: (j, i, 0))],
                 [jax.ShapeDtypeStruct((N_CONV_SEG, t, SEG), F32)], [xb, wt], {})


def _t5_bucket(d):
    max_exact = N_BUCKETS // 2
    df = jnp.maximum(d, 1).astype(F32)
    large = max_exact + (jnp.log(df / max_exact) / math.log(MAX_DISTANCE / max_exact)
                         * (N_BUCKETS - max_exact)).astype(jnp.int32)
    large = jnp.minimum(large, N_BUCKETS - 1)
    return jnp.where(d < max_exact, d, large)


def _prompt_bias_kernel(rb_ref, o_ref):
    h = pl.program_id(0)
    s = lax.broadcasted_iota(jnp.int32, (MOBA_BLOCK, MOBA_BLOCK), 0)
    t = lax.broadcasted_iota(jnp.int32, (MOBA_BLOCK, MOBA_BLOCK), 1)
    for k in range(2):
        dist = MOBA_BLOCK * k + t - s
        bucket = _t5_bucket(jnp.maximum(dist, 0))
        val = jnp.zeros((MOBA_BLOCK, MOBA_BLOCK), F32)
        for b in range(N_BUCKETS):
            val = jnp.where(bucket == b, rb_ref[b, h], val)
        o_ref[0, k] = jnp.where(dist >= 0, val * LOG2E, NEG)


def prompt_bias_tables(rel_bias):
    return pl.pallas_call(
        _prompt_bias_kernel,
        out_shape=jax.ShapeDtypeStruct((N_HEADS, 2, MOBA_BLOCK, MOBA_BLOCK), F32),
        grid=(N_HEADS,),
        in_specs=[pl.BlockSpec(memory_space=pltpu.SMEM)],
        out_specs=pl.BlockSpec((1, 2, MOBA_BLOCK, MOBA_BLOCK), lambda h: (h, 0, 0, 0)),
        compiler_params=_cparams(("parallel",)),
        name="prompt_bias_tables",
    )(rel_bias)


def _sample_bias_kernel(rb_ref, o_ref, *, t_new, past):
    rows = N_HEADS * t_new
    r = lax.broadcasted_iota(jnp.int32, (rows, LANE), 0)
    lane = lax.broadcasted_iota(jnp.int32, (rows, LANE), 1)
    t = r % t_new
    hh = r // t_new

    def lookup(dist):
        bucket = _t5_bucket(jnp.maximum(dist, 0))
        val = jnp.zeros((rows, LANE), F32)
        for b in range(N_BUCKETS):
            rb = jnp.zeros((rows, LANE), F32)
            for h in range(N_HEADS):
                rb = jnp.where(hh == h, rb_ref[b, h], rb)
            val = jnp.where(bucket == b, rb, val)
        return val

    o_ref[0] = lookup(LANE + t - lane)
    dist_new = t - lane
    o_ref[1] = jnp.where(dist_new >= 0, lookup(dist_new), NEG)
    o_ref[2] = lookup(jnp.full((rows, LANE), past, jnp.int32))


def sample_bias_tables(rel_bias, t_new, past):
    rows = N_HEADS * t_new
    return pl.pallas_call(
        functools.partial(_sample_bias_kernel, t_new=t_new, past=past),
        out_shape=jax.ShapeDtypeStruct((3, rows, LANE), F32),
        in_specs=[pl.BlockSpec(memory_space=pltpu.SMEM)],
        out_specs=pl.BlockSpec(memory_space=pltpu.VMEM),
        name="sample_bias_tables",
    )(rel_bias)


def _moba_select_kernel(idx, rb_ref, k_ref, q_ref, o_ref, *, n_blk):
    h = idx[1]
    seq = q_ref.shape[3]
    kmean = jnp.sum(k_ref[0, 0, 0].reshape(n_blk, MOBA_BLOCK, HEAD_DIM), axis=1) * (1.0 / MOBA_BLOCK)
    gate = lax.dot_general(kmean, q_ref[0, 0, 0], NT_DIMS, precision=HIGHEST, preferred_element_type=F32)
    own = lax.broadcasted_iota(jnp.int32, (1, seq), 1) // MOBA_BLOCK
    far = rb_ref[N_BUCKETS - 1, h] * LOG2E
    o_ref[0, 0] = jnp.where(_top_blocks(gate, own, 0), far, NEG)


def moba_select(k, q, rel_bias, n_batch, seq):
    n_blk = seq // MOBA_BLOCK
    spec = lambda a: pl.BlockSpec((1, 1, 1, seq, HEAD_DIM), lambda b, h: (a[1], b, h, 0, 0))
    return _Call("moba_select", functools.partial(_moba_select_kernel, n_blk=n_blk), (n_batch, N_HEADS),
                 [pl.BlockSpec(memory_space=pltpu.SMEM), spec(k), spec(q)],
                 [pl.BlockSpec((1, 1, n_blk, seq), lambda b, h: (b, h, 0, 0))],
                 [jax.ShapeDtypeStruct((n_batch, N_HEADS, n_blk, seq), F32)], [rel_bias, k[0], q[0]], {})


def _log_sigmoid(x):
    return -(jnp.maximum(-x, 0.0) + jnp.log1p(jnp.exp(-jnp.abs(x))))


def _split3(x):
    hi = x.astype(BF16).astype(F32)
    r1 = x - hi
    mid = r1.astype(BF16).astype(F32)
    lo = (r1 - mid).astype(BF16).astype(F32)
    return hi, mid, lo


def _fox_prep_kernel(idx, f_ref, bf_ref, lf_ref, kx_ref, qx_ref, carry_ref, *, blk):
    i = idx[1]

    @pl.when(i == 0)
    def _():
        carry_ref[...] = jnp.zeros_like(carry_ref)

    lane = lax.broadcasted_iota(jnp.int32, (blk, LANE), 1)
    lf = jnp.where(lane < N_HEADS, _log_sigmoid(f_ref[0] + bf_ref[...]), 0.0)
    lf_ref[...] = lf
    row = lax.broadcasted_iota(jnp.int32, (blk, blk), 0)
    col = lax.broadcasted_iota(jnp.int32, (blk, blk), 1)
    tril = (row >= col).astype(BF16)
    cum_all = carry_ref[0:1, :]
    for piece in _split3(lf):
        cum_all = cum_all + jnp.dot(tril, piece.astype(BF16), preferred_element_type=F32)
    carry_ref[0:1, :] = cum_all[blk - 1:blk, :]
    for h in range(N_HEADS):
        cum = jnp.broadcast_to(cum_all[:, h:h + 1], (blk, LANE))
        hi, mid, lo = _split3(cum * LOG2E)
        kx_ref[0, h] = jnp.where(lane == 0, -hi, jnp.where(lane == 1, -mid, jnp.where(
            lane == 2, -lo, jnp.where(lane < 6, 1.0, 0.0)))).astype(BF16)
        qx_ref[0, h] = jnp.where(lane == 3, hi, jnp.where(lane == 4, mid, jnp.where(
            lane == 5, lo, jnp.where(lane < 3, 1.0, 0.0)))).astype(BF16)


def fox_prep(zc, bf_row, n_batch, seq, blk=256):
    n_i = seq // blk
    xshape = jax.ShapeDtypeStruct((n_batch, N_HEADS, seq, LANE), BF16)
    xspec = pl.BlockSpec((1, N_HEADS, blk, LANE), lambda b, i: (b, 0, i, 0))
    return _Call("fox_prep", functools.partial(_fox_prep_kernel, blk=blk), (n_batch, n_i),
                 [pl.BlockSpec((1, blk, LANE), lambda b, i: (F_SEG, b * n_i + i, F_LANEBLK)),
                  pl.BlockSpec((1, LANE), lambda b, i: (0, 0))],
                 [pl.BlockSpec((blk, LANE), lambda b, i: (b * n_i + i, 0)), xspec, xspec],
                 [jax.ShapeDtypeStruct((n_batch * seq, LANE), F32), xshape, xshape], [zc, bf_row], {}, None,
                 (pltpu.VMEM((SUBLANE, LANE), F32),))


HEADS_PER_STEP = 6
HEAD_GROUP = 3


def _attn_tiles(qs, k_ref, kx_ref, v_ref, b0, nb, bias_fn, state):
    m_ref, l_ref, acc_ref = state
    start = pl.multiple_of(b0 * MOBA_BLOCK, MOBA_BLOCK)
    tk = nb * MOBA_BLOCK
    hp = len(qs)
    scores = {}

    def logits(heads):
        for i in heads:
            kt = k_ref[0, 0, i, pl.ds(start, tk), :]
            if kx_ref is not None:
                kt = jnp.concatenate([kt, kx_ref[0, i, pl.ds(start, tk), :]], axis=1)
            scores[i] = bias_fn(lax.dot_general(kt, qs[i], NT_DIMS, preferred_element_type=F32), i, b0)

    def update(heads):
        probs = {}
        for i in heads:
            m = m_ref[i]
            m_new = jnp.maximum(m, jnp.max(scores[i], axis=0, keepdims=True))
            alpha = jnp.exp2(m - m_new)
            p = jnp.exp2(scores.pop(i) - m_new)
            m_ref[i] = m_new
            l_ref[i] = alpha * l_ref[i] + jnp.sum(p, axis=0, keepdims=True)
            probs[i] = (p.astype(BF16), alpha)
        for i in heads:
            p, alpha = probs[i]
            vt = v_ref[0, 0, i, pl.ds(start, tk), :]
            pv = lax.dot_general(vt, p, TN_DIMS, preferred_element_type=F32)
            acc_ref[i] = alpha * acc_ref[i] + pv

    groups = [range(i, min(i + HEAD_GROUP, hp)) for i in range(0, hp, HEAD_GROUP)]
    logits(groups[0])
    for gi, heads in enumerate(groups):
        if gi + 1 < len(groups):
            logits(groups[gi + 1])
        update(heads)


def _attn_state_shapes(tq, hp):
    return [pltpu.VMEM((hp, 1, tq), F32), pltpu.VMEM((hp, 1, tq), F32), pltpu.VMEM((hp, HEAD_DIM, tq), F32)]


def _attn_init(state):
    m_ref, l_ref, acc_ref = state
    m_ref[...] = jnp.full(m_ref.shape, NEG, F32)
    l_ref[...] = jnp.zeros(l_ref.shape, F32)
    acc_ref[...] = jnp.zeros(acc_ref.shape, F32)


def _attn_finish(o_ref, state):
    m_ref, l_ref, acc_ref = state
    for i in range(acc_ref.shape[0]):
        o_ref[0, i] = (acc_ref[i] / l_ref[i]).T.astype(o_ref.dtype)


PAST_BLOCKS = 4


def _attn_drive(j, tiles):
    jf = jnp.maximum(j - 1, 0)

    @pl.loop(0, jf // PAST_BLOCKS)
    def _(n):
        tiles(n * PAST_BLOCKS, PAST_BLOCKS, False)

    nb = PAST_BLOCKS // 2
    while nb >= 1:
        @pl.when(jf % (2 * nb) >= nb)
        def _(nb=nb):
            tiles((jf // (2 * nb)) * (2 * nb), nb, False)
        nb //= 2

    @pl.when(j > 0)
    def _():
        tiles(j - 1, 2, True)

    @pl.when(j == 0)
    def _():
        tiles(0, 1, True)


def _fox_prompt_kernel(idx, q_ref, qx_ref, k_ref, kx_ref, v_ref, o_ref, *state, tq, hp):
    j = idx[2]
    qs = [jnp.concatenate([q_ref[0, 0, i], qx_ref[0, i]], axis=1) for i in range(hp)]

    def tiles(b0, nb, last):
        def bias(s, i, b0):
            if not last:
                return s
            key = lax.broadcasted_iota(jnp.int32, s.shape, 0) - (nb - 1) * tq
            qry = lax.broadcasted_iota(jnp.int32, s.shape, 1)
            return jnp.where(key <= qry, s, NEG)
        _attn_tiles(qs, k_ref, kx_ref, v_ref, b0, nb, bias, state)

    _attn_init(state)
    _attn_drive(j, tiles)
    _attn_finish(o_ref, state)


def _head_spec(seg, hp, rows, row_map, **kw):
    return pl.BlockSpec((1, 1, hp, rows, HEAD_DIM), lambda b, g, j: (seg, b, g, row_map(j), 0), **kw)


_RESIDENT = dict(pipeline_mode=pl.Buffered(1))


def fox_prompt(q, k, v, kx, qx, n_batch, seq, tq=256, hp=HEADS_PER_STEP):
    nq = seq // tq
    return _Call(
        "fox_prompt", functools.partial(_fox_prompt_kernel, tq=tq, hp=hp), (n_batch, N_HEADS // hp, nq),
        [_head_spec(q[1], hp, tq, lambda j: j),
         pl.BlockSpec((1, hp, tq, LANE), lambda b, g, j: (b, g, j, 0)),
         _head_spec(k[1], hp, seq, lambda j: 0, **_RESIDENT),
         pl.BlockSpec((1, hp, seq, LANE), lambda b, g, j: (b, g, 0, 0), **_RESIDENT),
         _head_spec(v[1], hp, seq, lambda j: 0, **_RESIDENT)],
        [pl.BlockSpec((1, hp, tq, HEAD_DIM), lambda b, g, j: (b, g, j, 0))],
        [jax.ShapeDtypeStruct((n_batch, N_HEADS, seq, HEAD_DIM), BF16)],
        [q[0], qx, k[0], kx, v[0]], {}, None, tuple(_attn_state_shapes(tq, hp)))


def _top_blocks(gate, n_valid, axis):
    n_blk = gate.shape[axis]
    blk = lax.broadcasted_iota(jnp.int32, gate.shape, axis)
    blk_f = blk.astype(F32)
    g = jnp.where(blk < n_valid, gate, -jnp.inf)
    sel = jnp.zeros(gate.shape, jnp.bool_)
    for _ in range(MOBA_TOPK):
        mx = jnp.max(g, axis=axis, keepdims=True)
        cand = (g == mx) & (mx > -jnp.inf)
        first = jnp.min(jnp.where(cand, blk_f, float(n_blk)), axis=axis, keepdims=True)
        pick = blk_f == first
        sel = sel | pick
        g = jnp.where(pick, -jnp.inf, g)
    return sel


def _moba_prompt_kernel(idx, rb_ref, q_ref, k_ref, v_ref, mb_ref, tab_ref, o_ref, *state, tq, hp):
    g, j = idx[1], idx[2]
    qs = [q_ref[0, 0, i] for i in range(hp)]
    fars = [rb_ref[N_BUCKETS - 1, g * hp + i] * LOG2E for i in range(hp)]

    def tiles(b0, nb, last):
        def bias(s, i, b0):
            rows = lambda r: s[r * tq:(r + 1) * tq]
            if not last:
                return jnp.concatenate([rows(r) + mb_ref[0, i, pl.ds(b0 + r, 1), :] for r in range(nb)], axis=0)
            own = rows(nb - 1) + tab_ref[i, 0]
            if nb == 1:
                return own
            prev = rows(0) + tab_ref[i, 1] + (mb_ref[0, i, pl.ds(b0, 1), :] - fars[i])
            return jnp.concatenate([prev, own], axis=0)
        _attn_tiles(qs, k_ref, None, v_ref, b0, nb, bias, state)

    _attn_init(state)
    _attn_drive(j, tiles)
    _attn_finish(o_ref, state)


def moba_prompt(q, k, v, mb, tab, rel_bias, n_batch, seq, hp=HEADS_PER_STEP):
    tq = MOBA_BLOCK
    nq = seq // tq
    return _Call(
        "moba_prompt", functools.partial(_moba_prompt_kernel, tq=tq, hp=hp), (n_batch, N_HEADS // hp, nq),
        [pl.BlockSpec(memory_space=pltpu.SMEM),
         _head_spec(q[1], hp, tq, lambda j: j),
         _head_spec(k[1], hp, seq, lambda j: 0, **_RESIDENT),
         _head_spec(v[1], hp, seq, lambda j: 0, **_RESIDENT),
         pl.BlockSpec((1, hp, nq, tq), lambda b, g, j: (b, g, 0, j)),
         pl.BlockSpec((hp, 2, tq, tq), lambda b, g, j: (g, 0, 0, 0), **_RESIDENT)],
        [pl.BlockSpec((1, hp, tq, HEAD_DIM), lambda b, g, j: (b, g, j, 0))],
        [jax.ShapeDtypeStruct((n_batch, N_HEADS, seq, HEAD_DIM), BF16)],
        [rel_bias, q[0], k[0], v[0], mb, tab], {}, None, tuple(_attn_state_shapes(tq, hp)))


def _pad_rows(x, rows):
    return jnp.concatenate([x, jnp.zeros((rows - x.shape[0], x.shape[1]), x.dtype)], axis=0)


def _lanes(a, b):
    return jnp.concatenate([a, b], axis=1)


def _rows(a, b):
    return jnp.concatenate([a, b], axis=0)


def _pair_queries(q_ref, g, scale):
    q0, q1 = q_ref[0, 0, 2 * g], q_ref[0, 0, 2 * g + 1]
    z = jnp.zeros_like(q0)
    return _rows(_lanes(q0, z), _lanes(z, q1)) * scale


def _pair_tile(pages, p0, g):
    t = lambda p: _lanes(pages[p][0, 0, 2 * g].astype(BF16), pages[p][0, 0, 2 * g + 1].astype(BF16))
    return _rows(t(p0), t(p0 + 1))


def _pair_new(x_ref, g, page):
    return _lanes(_pad_rows(x_ref[0, 0, 2 * g], page), _pad_rows(x_ref[0, 0, 2 * g + 1], page)).astype(BF16)


def _pair_bias(top, bottom, t_new):
    return _rows(jnp.broadcast_to(top, (t_new, top.shape[1])), jnp.broadcast_to(bottom, (t_new, bottom.shape[1])))


def _pair_probs(s_list, s_new):
    m = s_list[0]
    for s in s_list[1:]:
        m = jnp.maximum(m, s)
    m = jnp.max(jnp.maximum(jnp.maximum(m[:, :LANE], m[:, LANE:]), s_new), axis=1, keepdims=True)
    l = None
    p_list = []
    for s in s_list:
        p = jnp.exp(s - m)
        l = p if l is None else l + p
        p_list.append(p.astype(BF16))
    p = jnp.exp(s_new - m)
    l = jnp.sum(l[:, :LANE] + l[:, LANE:] + p, axis=1, keepdims=True)
    return p_list, p.astype(BF16), l


def _pair_pv(o_ref, g, probs, v_pages, v_ref, t_new, page):
    p_list, p_new, l = probs
    acc = jnp.dot(p_new, _pair_new(v_ref, g, page), preferred_element_type=F32)
    for n, p in enumerate(p_list):
        acc = acc + jnp.dot(p, _pair_tile(v_pages, 2 * n, g), preferred_element_type=F32)
    out = acc / l
    o_ref[0, 2 * g] = out[:t_new, :HEAD_DIM]
    o_ref[0, 2 * g + 1] = out[t_new:, HEAD_DIM:]


def _fox_sample_kernel(pt_ref, q_ref, k_ref, v_ref, f_ref, bf_ref, *rest, t_new, n_pages, page):
    k_pages = rest[:n_pages]
    v_pages = rest[n_pages:2 * n_pages]
    lf_pages = rest[2 * n_pages:3 * n_pages]
    o_ref, lf_ref = rest[3 * n_pages:]

    n_r = n_pages * SUBLANE
    lane = lax.broadcasted_iota(jnp.int32, (n_r, page), 1)
    srow = lax.broadcasted_iota(jnp.int32, (n_r, page), 0)
    lfc = jnp.concatenate([r[0, 0] for r in lf_pages], axis=0)
    incl = lfc
    step = 1
    while step < page:
        incl = incl + jnp.where(lane + step < page, pltpu.roll(incl, page - step, axis=1), 0.0)
        step *= 2
    tot = jnp.broadcast_to(jnp.sum(jnp.where(lane == 0, incl, 0.0), axis=1, keepdims=True), (n_r, page))
    run = tot
    step = SUBLANE
    while step < n_r:
        run = run + jnp.where(srow + step < n_r, pltpu.roll(run, n_r - step, axis=0), 0.0)
        step *= 2
    suf = (incl - lfc) + (run - tot)

    flane = lax.broadcasted_iota(jnp.int32, (t_new, LANE), 1)
    lf_new = jnp.where(flane < N_HEADS, _log_sigmoid(f_ref[0] + bf_ref[...]), 0.0)
    lf_ref[...] = lf_new
    cum = _pad_rows(lf_new, LANE).T
    lane2 = lax.broadcasted_iota(jnp.int32, (LANE, LANE), 1)
    step = 1
    while step < t_new:
        cum = cum + jnp.where(lane2 >= step, pltpu.roll(cum, step, axis=1), 0.0)
        step *= 2
    r = lax.broadcasted_iota(jnp.int32, (2 * t_new, page), 0) % t_new
    c = lax.broadcasted_iota(jnp.int32, (2 * t_new, page), 1)
    suf_row = lambda p, h: suf[p * SUBLANE + h:p * SUBLANE + h + 1, :]

    pairs = range(N_HEADS // 2)
    logits = []
    for g in pairs:
        h0, h1 = 2 * g, 2 * g + 1
        qp = _pair_queries(q_ref, g, SCALE).astype(BF16)
        s_list = []
        for p0 in range(0, n_pages, 2):
            s = lax.dot_general(qp, _pair_tile(k_pages, p0, g), NT_DIMS, preferred_element_type=F32)
            bias = _pair_bias(_lanes(suf_row(p0, h0), suf_row(p0 + 1, h0)),
                              _lanes(suf_row(p0, h1), suf_row(p0 + 1, h1)), t_new)
            s_list.append(s + bias)
        s_new = lax.dot_general(qp, _pair_new(k_ref, g, page), NT_DIMS, preferred_element_type=F32)
        s_new = jnp.where(c <= r, s_new - _pair_bias(cum[h0:h0 + 1, :], cum[h1:h1 + 1, :], t_new), NEG)
        logits.append((s_list, s_new))
    probs = [_pair_probs(*logits[g]) for g in pairs]
    for g in pairs:
        _pair_pv(o_ref, g, probs[g], v_pages, v_ref, t_new, page)


def _page_specs(n_pages, layer, shape, b0):
    def spec(p):
        return pl.BlockSpec((1, 1) + shape, lambda b, pt: (layer, pt[b0 + b, p]) + (0,) * len(shape))
    return [spec(p) for p in range(n_pages)]


def _new_spec(seg, t_new, b0):
    return pl.BlockSpec((1, 1, N_HEADS, t_new, HEAD_DIM), lambda b, pt: (seg, b0 + b, 0, 0, 0))


def fox_sample(q, k, v, zc, bf_row, pool_k, pool_v, pool_lft, page_table, layer, t_new, b0, n):
    n_pages = page_table.shape[1]
    page = pool_k.shape[3]
    in_specs = ([_new_spec(q[1], t_new, b0), _new_spec(k[1], t_new, b0), _new_spec(v[1], t_new, b0),
                 pl.BlockSpec((1, t_new, LANE), lambda b, pt: (F_SEG, b0 + b, F_LANEBLK)),
                 pl.BlockSpec((1, LANE), lambda b, pt: (0, 0))]
                + _page_specs(n_pages, layer, (N_HEADS, page, HEAD_DIM), b0)
                + _page_specs(n_pages, layer, (N_HEADS, page, HEAD_DIM), b0)
                + _page_specs(n_pages, layer, (SUBLANE, page), b0))
    return _Call(
        "fox_sample", functools.partial(_fox_sample_kernel, t_new=t_new, n_pages=n_pages, page=page), (n,), in_specs,
        [pl.BlockSpec((1, N_HEADS, t_new, HEAD_DIM), lambda b, pt: (b, 0, 0, 0)),
         pl.BlockSpec((t_new, LANE), lambda b, pt: (b, 0))],
        [jax.ShapeDtypeStruct((n, N_HEADS, t_new, HEAD_DIM), F32), jax.ShapeDtypeStruct((n * t_new, LANE), F32)],
        [q[0], k[0], v[0], zc, bf_row] + [pool_k] * n_pages + [pool_v] * n_pages + [pool_lft] * n_pages, {},
        page_table)


def _moba_sample_kernel(pt_ref, q_ref, k_ref, v_ref, tab_ref, *rest, t_new, n_pages, page):
    k_pages = rest[:n_pages]
    v_pages = rest[n_pages:2 * n_pages]
    o_ref = rest[2 * n_pages]
    assert MOBA_BLOCK == 2 * page
    n_blk = n_pages // 2

    def block_means(h):
        sums = [jnp.sum(k_pages[2 * n][0, 0, h] + k_pages[2 * n + 1][0, 0, h], axis=0, keepdims=True)
                for n in range(n_blk)]
        return _pad_rows(jnp.concatenate(sums, axis=0) * (1.0 / MOBA_BLOCK), LANE)

    pairs = range(N_HEADS // 2)
    logits = []
    for g in pairs:
        q32 = _pair_queries(q_ref, g, 1.0)
        qp = (q32 * SCALE).astype(BF16)
        kmean = _lanes(block_means(2 * g), block_means(2 * g + 1))
        gate = lax.dot_general(q32, kmean, NT_DIMS, precision=HIGHEST, preferred_element_type=F32)
        sel = _top_blocks(gate, n_blk, 1)
        mask = jnp.where(sel, 0.0, NEG)
        rows = slice(2 * g * t_new, (2 * g + 2) * t_new)
        far = tab_ref[2, rows, :]
        s_list = []
        for n in range(n_blk):
            s = lax.dot_general(qp, _pair_tile(k_pages, 2 * n, g), NT_DIMS, preferred_element_type=F32)
            bias = _lanes(far, tab_ref[0, rows, :] if n == n_blk - 1 else far)
            s_list.append(s + bias + mask[:, n:n + 1])
        s_new = lax.dot_general(qp, _pair_new(k_ref, g, page), NT_DIMS, preferred_element_type=F32)
        logits.append((s_list, s_new + tab_ref[1, rows, :]))
    probs = [_pair_probs(*logits[g]) for g in pairs]
    for g in pairs:
        _pair_pv(o_ref, g, probs[g], v_pages, v_ref, t_new, page)


def moba_sample(q, k, v, tab, pool_k, pool_v, page_table, layer, t_new, b0, n):
    n_pages = page_table.shape[1]
    page = pool_k.shape[3]
    rows = N_HEADS * t_new
    in_specs = ([_new_spec(q[1], t_new, b0), _new_spec(k[1], t_new, b0), _new_spec(v[1], t_new, b0),
                 pl.BlockSpec((3, rows, LANE), lambda b, pt: (0, 0, 0))]
                + _page_specs(n_pages, layer, (N_HEADS, page, HEAD_DIM), b0)
                + _page_specs(n_pages, layer, (N_HEADS, page, HEAD_DIM), b0))
    return _Call(
        "moba_sample", functools.partial(_moba_sample_kernel, t_new=t_new, n_pages=n_pages, page=page), (n,),
        in_specs, [pl.BlockSpec((1, N_HEADS, t_new, HEAD_DIM), lambda b, pt: (b, 0, 0, 0))],
        [jax.ShapeDtypeStruct((n, N_HEADS, t_new, HEAD_DIM), F32)],
        [q[0], k[0], v[0], tab] + [pool_k] * n_pages + [pool_v] * n_pages, {}, page_table)


def _silu(g):
    return g / (1.0 + jnp.exp(-g))


def _token_major(v, tm):
    return jnp.concatenate([v[:, h].reshape(tm, HEAD_DIM) for h in range(N_HEADS)], axis=1)


def _combine_kernel(*refs, tm, seq_tiles, seq_rows, alpha):
    (x_ref, oa_ref, of_ref, ga_ref, gf_ref, b0, b1, c0, c1, h0, h1, g0, g1) = refs[:13]
    if seq_rows is None:
        pc0, pc1, ph0, ph1 = refs[13:17]
        rest = refs[17:]
    else:
        hist1_ref, hist2_ref = refs[13:15]
        rest = refs[15:]
    wout_ref, cw_ref, lng_ref, lnb_ref, y_ref, yb_ref, u_ref = rest
    cat = lambda a, b: jnp.concatenate([a[0], b[0]], axis=1)
    u = cat(c0, c1) * cat(h0, h1)
    if seq_rows is None:
        keep = (pl.program_id(0) % seq_tiles != 0).astype(F32)
        uprev = cat(pc0, pc1) * cat(ph0, ph1) * keep
        ext = jnp.concatenate([uprev, u], axis=0)
        u1 = pltpu.roll(ext, 1, axis=0)[SUBLANE:]
        u2 = pltpu.roll(ext, 2, axis=0)[SUBLANE:]
        u_ref[0] = u[tm - SUBLANE:, :]
    else:
        r = lax.broadcasted_iota(jnp.int32, (tm, D_CONV), 0) % seq_rows
        u1 = jnp.where(r >= 1, pltpu.roll(u, 1, axis=0), hist1_ref[...])
        u2 = jnp.where(r >= 2, pltpu.roll(u, 2, axis=0), hist2_ref[...])
        u_ref[...] = u
    conv = cw_ref[0:1, :] * u2 + cw_ref[1:2, :] * u1 + cw_ref[2:3, :] * u
    oc = (cat(b0, b1) * conv * _silu(cat(g0, g1))).astype(BF16)
    oa = (_token_major(oa_ref[...].astype(F32), tm) * _silu(_token_major(ga_ref[0], tm))).astype(BF16)
    of = (_token_major(of_ref[...].astype(F32), tm) * _silu(_token_major(gf_ref[0], tm))).astype(BF16)
    proj = (jnp.dot(oa, wout_ref[0:D_ATT, :], preferred_element_type=F32)
            + jnp.dot(of, wout_ref[D_ATT:2 * D_ATT, :], preferred_element_type=F32)
            + jnp.dot(oc, wout_ref[2 * D_ATT:, :], preferred_element_type=F32))
    res = alpha * x_ref[...] + proj
    mu = jnp.mean(res, axis=-1, keepdims=True)
    cen = res - mu
    var = jnp.mean(cen * cen, axis=-1, keepdims=True)
    y = cen * lax.rsqrt(var + LN_EPS) * lng_ref[...] + lnb_ref[...]
    y_ref[...] = y
    yb_ref[...] = y.astype(BF16)


def combine(x, oa, of, zg, zc, wout, cw, lng, lnb, alpha, tm, hist=None, gates=(0, 1)):
    t = x.shape[0]
    n_seq, _, seq_len, _ = oa.shape
    assert t == n_seq * seq_len and t % tm == 0
    if seq_len >= tm:
        assert seq_len % tm == 0
        sb, sl, per_seq = 1, tm, seq_len // tm
        hmap = lambda i: (i // per_seq, 0, i % per_seq, 0)
    else:
        assert tm % seq_len == 0
        sb, sl, per_seq = tm // seq_len, seq_len, 1
        hmap = lambda i: (i, 0, 0, 0)
    hblk = (sb, N_HEADS, sl, HEAD_DIM)
    zblk = lambda sc: pl.BlockSpec((1, tm, CBLK), lambda i: (sc[0], i, sc[1]))
    pieces = [sc for idx in (0, 1, 2, 3) for sc in _conv_piece(idx)]
    in_specs = [pl.BlockSpec((tm, D_MODEL), lambda i: (i, 0)),
                pl.BlockSpec(hblk, hmap),
                pl.BlockSpec(hblk, hmap),
                pl.BlockSpec((1,) + hblk, lambda i: (gates[0],) + hmap(i)),
                pl.BlockSpec((1,) + hblk, lambda i: (gates[1],) + hmap(i))] + [zblk(sc) for sc in pieces]
    args = [x, oa, of, zg, zg] + [zc] * 8
    if hist is None:
        seq_rows = None
        tb = tm // SUBLANE
        prev = lambda sc: pl.BlockSpec((1, SUBLANE, CBLK), lambda i: (sc[0], jnp.maximum(i * tb - 1, 0), sc[1]))
        in_specs += [prev(sc) for sc in pieces[2:6]]
        args += [zc] * 4
        u_shape = jax.ShapeDtypeStruct((n_seq, SUBLANE, D_CONV), F32)
        u_spec = pl.BlockSpec((1, SUBLANE, D_CONV), lambda i: (i // per_seq, 0, 0))
    else:
        seq_rows = seq_len
        in_specs += [pl.BlockSpec((tm, D_CONV), lambda i: (i, 0))] * 2
        args += list(hist)
        u_shape = jax.ShapeDtypeStruct((t, D_CONV), F32)
        u_spec = pl.BlockSpec((tm, D_CONV), lambda i: (i, 0))
    in_specs += [pl.BlockSpec((D_MODEL, D_MODEL), lambda i: (0, 0)),
                 pl.BlockSpec((SUBLANE, D_CONV), lambda i: (0, 0)),
                 pl.BlockSpec((1, D_MODEL), lambda i: (0, 0)),
                 pl.BlockSpec((1, D_MODEL), lambda i: (0, 0))]
    args += [wout, cw, lng, lnb]
    return pl.pallas_call(
        functools.partial(_combine_kernel, tm=tm, seq_tiles=per_seq, seq_rows=seq_rows, alpha=alpha),
        out_shape=(jax.ShapeDtypeStruct((t, D_MODEL), F32), jax.ShapeDtypeStruct((t, D_MODEL), BF16), u_shape),
        grid=(t // tm,),
        in_specs=in_specs,
        out_specs=(pl.BlockSpec((tm, D_MODEL), lambda i: (i, 0)), pl.BlockSpec((tm, D_MODEL), lambda i: (i, 0)),
                   u_spec),
        compiler_params=_cparams(("arbitrary",)),
        name="combine_prompt" if hist is None else "combine_sample",
    )(*args)


def kernel(x_prompt, x_sample, cache_moba_k, cache_moba_v, cache_fox_k, cache_fox_v, cache_fox_logf,
           state_conv, page_table, rel_bias, w_in, b_f, conv_w, w_out, ln_g, ln_b):
    depth = w_in.shape[0]
    alpha = (2 * depth) ** 0.25
    n_batch, seq, _ = x_prompt.shape
    n_seq, t_new, _ = x_sample.shape
    n_phys, page = cache_moba_k.shape[1:3]
    past = page_table.shape[1] * page
    assert page == LANE and past % MOBA_BLOCK == 0 and t_new == SUBLANE and seq % MOBA_BLOCK == 0
    assert past >= MAX_DISTANCE and w_in.shape[2] == N_MAIN + N_HEADS + 4 * D_CONV

    xp = x_prompt.reshape(n_batch * seq, D_MODEL)
    xs = x_sample.reshape(n_seq * t_new, D_MODEL)
    xpb, xsb = xp.astype(BF16), xs.astype(BF16)
    pool = lambda c: jnp.swapaxes(c, 2, 3)
    pk_a, pv_a, pk_f, pv_f = pool(cache_moba_k), pool(cache_moba_v), pool(cache_fox_k), pool(cache_fox_v)
    pool_lft = jnp.pad(jnp.swapaxes(cache_fox_logf, 2, 3), ((0, 0), (0, 0), (0, SUBLANE - N_HEADS), (0, 0)))
    tab_p = prompt_bias_tables(rel_bias)
    tab_s = sample_bias_tables(rel_bias, t_new, past)
    tm_s = min(256, n_seq * t_new)

    outs = {k: [] for k in ("p_lf", "p_cv", "s_lf", "s_cv")}
    kv_segs = {"ka": SEG_KA, "va": SEG_VA, "kf": SEG_KF, "vf": SEG_VF}
    p_kv = dict.fromkeys(kv_segs)
    s_kv = {name: [] for name in kv_segs}
    for layer in range(depth):
        wt_h = jnp.swapaxes(w_in[layer], 0, 1).astype(BF16)
        n_f = N_MAIN + N_HEADS
        wt_c = jnp.concatenate(
            [wt_h[n_f:], wt_h[N_MAIN:n_f],
             jnp.zeros((N_CONV_SEG * SEG - 4 * D_CONV - N_HEADS, D_MODEL), BF16)], axis=0)
        wo = w_out[layer].astype(BF16)
        bf_row = jnp.pad(b_f[layer], (0, LANE - N_HEADS)).reshape(1, LANE)
        cw = jnp.pad(conv_w[layer], ((0, SUBLANE - CONV_WIDTH), (0, 0)))
        lng, lnb = ln_g[layer].reshape(1, D_MODEL), ln_b[layer].reshape(1, D_MODEL)

        (zhs,) = run(inproj_heads(xsb, wt_h, tuple(range(N_HEAD_SEG)), n_seq=n_seq, seq_len=t_new, tm=n_seq * t_new))
        (zcs,) = run(inproj_conv(xsb, wt_c, tm=n_seq * t_new))
        for name, seg in kv_segs.items():
            s_kv[name].append(zhs[seg])

        sample = {
            "fox": functools.partial(fox_sample, (zhs, SEG_QF), (zhs, SEG_KF), (zhs, SEG_VF), zcs, bf_row,
                                     pk_f, pv_f, pool_lft, page_table, layer, t_new),
            "moba": functools.partial(moba_sample, (zhs, SEG_QA), (zhs, SEG_KA), (zhs, SEG_VA), tab_s,
                                      pk_a, pv_a, page_table, layer, t_new)}
        next_seq = {"fox": 0, "moba": 0}
        chunks = {"fox": [], "moba": []}

        def run_with_sample(call):
            steps = math.prod(call.grid)
            for kind in sample:
                n_b = min(steps, n_seq - next_seq[kind])
                if n_b > 0:
                    outs_a, outs_b = run_zipped(call, sample[kind](next_seq[kind], n_b))
                    next_seq[kind] += n_b
                    chunks[kind].append(outs_b)
                    return outs_a
            return run(call)

        proj = functools.partial(inproj_heads, xpb, wt_h, n_seq=n_batch, seq_len=seq, tm=512)
        zq, zqb = run_with_sample(proj((SEG_QA, SEG_QF), bf16_scale=QSCALE))
        (zg,) = run_with_sample(proj((SEG_GA, SEG_GF)))
        kvb = {}
        for name, seg in kv_segs.items():
            p_kv[name], kvb[name] = run_with_sample(proj((seg,), bf16_scale=1.0, stack=(depth, layer, p_kv[name])))
        (zc,) = run_with_sample(inproj_conv(xpb, wt_c, tm=512))

        (mb,) = run_with_sample(moba_select((p_kv["ka"], layer), (zq, 0), rel_bias, n_batch, seq))
        (oa,) = run_with_sample(
            moba_prompt((zqb, 0), (kvb["ka"], 0), (kvb["va"], 0), mb, tab_p, rel_bias, n_batch, seq))
        lf, kx, qx = run_with_sample(fox_prep(zc, bf_row, n_batch, seq))
        (of,) = run_with_sample(fox_prompt((zqb, 1), (kvb["kf"], 0), (kvb["vf"], 0), kx, qx, n_batch, seq))
        for kind in sample:
            if next_seq[kind] < n_seq:
                chunks[kind].append(run(sample[kind](next_seq[kind], n_seq - next_seq[kind])))
        ofs = jnp.concatenate([c[0] for c in chunks["fox"]], axis=0)
        lfs = jnp.concatenate([c[1] for c in chunks["fox"]], axis=0)
        oas = jnp.concatenate([c[0] for c in chunks["moba"]], axis=0)
        xp, xpb, ulast = combine(xp, oa, of, zg, zc, wo, cw, lng, lnb, alpha, tm=256)
        outs["p_lf"].append(lf[:, :N_HEADS].reshape(n_batch, seq, N_HEADS))
        outs["p_cv"].append(ulast[:, SUBLANE - (CONV_WIDTH - 1):, :])

        st = state_conv[layer]
        zero = jnp.zeros((n_seq, t_new, D_CONV), F32)
        hist1 = zero.at[:, 0].set(st[:, 1]).reshape(n_seq * t_new, D_CONV)
        hist2 = zero.at[:, 0].set(st[:, 0]).at[:, 1].set(st[:, 1]).reshape(n_seq * t_new, D_CONV)
        xs, xsb, us = combine(xs, oas, ofs, zhs, zcs, wo, cw, lng, lnb, alpha, tm=tm_s, hist=(hist1, hist2),
                              gates=(SEG_GA, SEG_GF))
        outs["s_lf"].append(lfs[:, :N_HEADS].reshape(n_seq, t_new, N_HEADS))
        outs["s_cv"].append(us.reshape(n_seq, t_new, D_CONV)[:, t_new - (CONV_WIDTH - 1):, :])

    st = {k: jnp.stack(v) for k, v in outs.items()}
    tok_major = lambda a: jnp.swapaxes(a, 2, 3)
    return (xp.reshape(n_batch, seq, D_MODEL), xs.reshape(n_seq, t_new, D_MODEL),
            tok_major(p_kv["ka"]), tok_major(p_kv["va"]), tok_major(p_kv["kf"]), tok_major(p_kv["vf"]),
            st["p_lf"], st["p_cv"],
            *(tok_major(jnp.stack(s_kv[name])) for name in ("ka", "va", "kf", "vf")),
            st["s_lf"], st["s_cv"])
```

```python
import functools
import math
from typing import Callable, NamedTuple

import jax
import jax.numpy as jnp
from jax import lax
from jax.experimental import pallas as pl
from jax.experimental.pallas import tpu as pltpu

F32 = jnp.float32
BF16 = jnp.bfloat16
HIGHEST = lax.Precision.HIGHEST

D_MODEL = 2048
HEAD_DIM = 128
N_HEADS = 6
D_ATT = N_HEADS * HEAD_DIM
D_CONV = 512
CONV_WIDTH = 3
MOBA_BLOCK = 256
MOBA_TOPK = 3
N_BUCKETS = 32
MAX_DISTANCE = 128
LN_EPS = 1e-5
SCALE = HEAD_DIM ** -0.5
LOG2E = math.log2(math.e)
QSCALE = SCALE * LOG2E

SEG = 768
SEG_QA, SEG_KA, SEG_VA, SEG_GA, SEG_QF, SEG_KF, SEG_VF, SEG_GF = range(8)
N_HEAD_SEG = 8
N_CONV_SEG = 3
N_MAIN = N_HEAD_SEG * SEG
CBLK = 256
LANE = 128
SUBLANE = 8
F_SEG, F_LANEBLK = (4 * D_CONV) // SEG, ((4 * D_CONV) % SEG) // LANE
NEG = -1e30
VMEM_LIMIT = 56 * 1024 * 1024

NT_DIMS = (((1,), (1,)), ((), ()))
TN_DIMS = (((0,), (0,)), ((), ()))


def _cparams(sem):
    return pltpu.CompilerParams(dimension_semantics=sem, vmem_limit_bytes=VMEM_LIMIT)


class _Call(NamedTuple):
    name: str
    body: Callable
    grid: tuple
    in_specs: list
    out_specs: list
    out_shape: list
    args: list
    aliases: dict
    page_table: object = None
    scratch: tuple = ()


def run(c):
    if c.page_table is None:
        body = lambda *refs: c.body(tuple(pl.program_id(d) for d in range(len(c.grid))), *refs)
        return pl.pallas_call(
            body, out_shape=tuple(c.out_shape), grid=c.grid, in_specs=c.in_specs, out_specs=tuple(c.out_specs),
            scratch_shapes=list(c.scratch), input_output_aliases=c.aliases,
            compiler_params=_cparams(("arbitrary",) * len(c.grid)), name=c.name,
        )(*c.args)
    assert not c.aliases and not c.scratch
    return pl.pallas_call(
        c.body, out_shape=tuple(c.out_shape),
        grid_spec=pltpu.PrefetchScalarGridSpec(num_scalar_prefetch=1, grid=c.grid, in_specs=c.in_specs,
                                               out_specs=tuple(c.out_specs)),
        compiler_params=_cparams(("arbitrary",) * len(c.grid)), name=c.name,
    )(c.page_table, *c.args)


def run_zipped(a, b):
    n, n_b = math.prod(a.grid), b.grid[0]
    assert a.page_table is None and b.page_table is not None and len(b.grid) == 1 and n_b <= n

    def a_idx(k):
        idx = []
        for size in reversed(a.grid):
            idx.append(k % size)
            k = k // size
        return tuple(reversed(idx))

    def respec(sp, fn):
        if sp.block_shape is None:
            return sp
        return pl.BlockSpec(sp.block_shape, fn(sp.index_map), memory_space=sp.memory_space,
                            pipeline_mode=sp.pipeline_mode)

    for_a = lambda m: (lambda k, pt: m(*a_idx(k)))
    for_b = lambda m: (lambda k, pt: m(jnp.minimum(k, n_b - 1), pt))
    na_in, nb_in, na_out = len(a.in_specs), len(b.in_specs), len(a.out_specs)
    nb_out = len(b.out_specs)

    def body(pt_ref, *refs):
        ins_a, refs = refs[:na_in], refs[na_in:]
        ins_b, refs = refs[:nb_in], refs[nb_in:]
        outs_a, refs = refs[:na_out], refs[na_out:]
        outs_b, scratch_a = refs[:nb_out], refs[nb_out:]
        a.body(a_idx(pl.program_id(0)), *ins_a, *outs_a, *scratch_a)
        if n_b == n:
            b.body(pt_ref, *ins_b, *outs_b)
        else:
            pl.when(pl.program_id(0) < n_b)(lambda: b.body(pt_ref, *ins_b, *outs_b))

    outs = pl.pallas_call(
        body, out_shape=tuple(a.out_shape) + tuple(b.out_shape),
        grid_spec=pltpu.PrefetchScalarGridSpec(
            num_scalar_prefetch=1, grid=(n,),
            in_specs=[respec(sp, for_a) for sp in a.in_specs] + [respec(sp, for_b) for sp in b.in_specs],
            out_specs=tuple([respec(sp, for_a) for sp in a.out_specs] + [respec(sp, for_b) for sp in b.out_specs]),
            scratch_shapes=list(a.scratch)),
        input_output_aliases={1 + i: o for i, o in a.aliases.items()},
        compiler_params=_cparams(("arbitrary",)), name=a.name + "_zip_" + b.name,
    )(b.page_table, *a.args, *b.args)
    return outs[:na_out], outs[na_out:]


def _conv_piece(idx):
    out = []
    for half in range(2):
        col = idx * D_CONV + half * CBLK
        out.append((col // SEG, (col % SEG) // CBLK))
    return out


def _inproj_heads_kernel(idx, x_ref, w_ref, *refs, sb, sl, n_in, bf16_scale):
    o_ref = refs[n_in]
    res = lax.dot_general(x_ref[...], w_ref[...], NT_DIMS, preferred_element_type=F32)
    for h in range(N_HEADS):
        o_ref[0, :, h] = res[:, h * HEAD_DIM:(h + 1) * HEAD_DIM].reshape(sb, sl, HEAD_DIM)
    if bf16_scale is not None:
        scaled = (res * bf16_scale).astype(BF16)
        for h in range(N_HEADS):
            refs[n_in + 1][0, :, h] = scaled[:, h * HEAD_DIM:(h + 1) * HEAD_DIM].reshape(sb, sl, HEAD_DIM)


def inproj_heads(xb, wt, segs, n_seq, seq_len, tm, bf16_scale=None, stack=None):
    t = xb.shape[0]
    assert t == n_seq * seq_len and t % tm == 0 and wt.shape[0] >= N_MAIN and wt.shape[1] == D_MODEL
    if seq_len >= tm:
        assert seq_len % tm == 0
        sb, sl, per_seq = 1, tm, seq_len // tm
        rows = lambda i: (i // per_seq, 0, i % per_seq, 0)
    else:
        assert tm % seq_len == 0
        sb, sl = tm // seq_len, seq_len
        rows = lambda i: (i, 0, 0, 0)
    blk = (1, sb, N_HEADS, sl, HEAD_DIM)
    tail = (n_seq, N_HEADS, seq_len, HEAD_DIM)

    def seg_of(j):
        s = segs[0]
        for idx in range(1, len(segs)):
            s = jnp.where(j == idx, segs[idx], s)
        return s

    in_specs = [pl.BlockSpec((tm, D_MODEL), lambda j, i: (i, 0)),
                pl.BlockSpec((SEG, D_MODEL), lambda j, i: (seg_of(j), 0))]
    args = [xb, wt]
    aliases = {}
    if stack is None:
        shapes = [jax.ShapeDtypeStruct((len(segs),) + tail, F32)]
        specs = [pl.BlockSpec(blk, lambda j, i: (j,) + rows(i))]
    else:
        depth, layer, prev = stack
        assert len(segs) == 1
        shapes = [jax.ShapeDtypeStruct((depth,) + tail, F32)]
        specs = [pl.BlockSpec(blk, lambda j, i: (layer,) + rows(i))]
        if prev is not None:
            in_specs.append(pl.BlockSpec(memory_space=pl.ANY))
            args.append(prev)
            aliases = {2: 0}
    if bf16_scale is not None:
        shapes.append(jax.ShapeDtypeStruct((len(segs),) + tail, BF16))
        specs.append(pl.BlockSpec(blk, lambda j, i: (j,) + rows(i)))
    body = functools.partial(_inproj_heads_kernel, sb=sb, sl=sl, n_in=len(args) - 2, bf16_scale=bf16_scale)
    return _Call("inproj_heads", body, (len(segs), t // tm), in_specs, specs, shapes, args, aliases)


def _inproj_conv_kernel(idx, x_ref, w_ref, o_ref):
    o_ref[0] = lax.dot_general(x_ref[...], w_ref[...], NT_DIMS, preferred_element_type=F32)


def inproj_conv(xb, wt, tm):
    t = xb.shape[0]
    assert t % tm == 0 and wt.shape == (N_CONV_SEG * SEG, D_MODEL)
    return _Call("inproj_conv", _inproj_conv_kernel, (N_CONV_SEG, t // tm),
                 [pl.BlockSpec((tm, D_MODEL), lambda j, i: (i, 0)),
                  pl.BlockSpec((SEG, D_MODEL), lambda j, i: (j, 0))],
                 [pl.BlockSpec((1, tm, SEG), lambda j, i: (j, i, 0))],
                 [jax.ShapeDtypeStruct((N_CONV_SEG, t, SEG), F32)], [xb, wt], {})


def _t5_bucket(d):
    max_exact = N_BUCKETS // 2
    df = jnp.maximum(d, 1).astype(F32)
    large = max_exact + (jnp.log(df / max_exact) / math.log(MAX_DISTANCE / max_exact)
                         * (N_BUCKETS - max_exact)).astype(jnp.int32)
    large = jnp.minimum(large, N_BUCKETS - 1)
    return jnp.where(d < max_exact, d, large)


def _prompt_bias_kernel(rb_ref, o_ref):
    h = pl.program_id(0)
    s = lax.broadcasted_iota(jnp.int32, (MOBA_BLOCK, MOBA_BLOCK), 0)
    t = lax.broadcasted_iota(jnp.int32, (MOBA_BLOCK, MOBA_BLOCK), 1)
    for k in range(2):
        dist = MOBA_BLOCK * k + t - s
        bucket = _t5_bucket(jnp.maximum(dist, 0))
        val = jnp.zeros((MOBA_BLOCK, MOBA_BLOCK), F32)
        for b in range(N_BUCKETS):
            val = jnp.where(bucket == b, rb_ref[b, h], val)
        o_ref[0, k] = jnp.where(dist >= 0, val * LOG2E, NEG)


def prompt_bias_tables(rel_bias):
    return pl.pallas_call(
        _prompt_bias_kernel,
        out_shape=jax.ShapeDtypeStruct((N_HEADS, 2, MOBA_BLOCK, MOBA_BLOCK), F32),
        grid=(N_HEADS,),
        in_specs=[pl.BlockSpec(memory_space=pltpu.SMEM)],
        out_specs=pl.BlockSpec((1, 2, MOBA_BLOCK, MOBA_BLOCK), lambda h: (h, 0, 0, 0)),
        compiler_params=_cparams(("parallel",)),
        name="prompt_bias_tables",
    )(rel_bias)


def _sample_bias_kernel(rb_ref, o_ref, *, t_new, past):
    rows = N_HEADS * t_new
    r = lax.broadcasted_iota(jnp.int32, (rows, LANE), 0)
    lane = lax.broadcasted_iota(jnp.int32, (rows, LANE), 1)
    t = r % t_new
    hh = r // t_new

    def lookup(dist):
        bucket = _t5_bucket(jnp.maximum(dist, 0))
        val = jnp.zeros((rows, LANE), F32)
        for b in range(N_BUCKETS):
            rb = jnp.zeros((rows, LANE), F32)
            for h in range(N_HEADS):
                rb = jnp.where(hh == h, rb_ref[b, h], rb)
            val = jnp.where(bucket == b, rb, val)
        return val

    o_ref[0] = lookup(LANE + t - lane)
    dist_new = t - lane
    o_ref[1] = jnp.where(dist_new >= 0, lookup(dist_new), NEG)
    o_ref[2] = lookup(jnp.full((rows, LANE), past, jnp.int32))


def sample_bias_tables(rel_bias, t_new, past):
    rows = N_HEADS * t_new
    return pl.pallas_call(
        functools.partial(_sample_bias_kernel, t_new=t_new, past=past),
        out_shape=jax.ShapeDtypeStruct((3, rows, LANE), F32),
        in_specs=[pl.BlockSpec(memory_space=pltpu.SMEM)],
        out_specs=pl.BlockSpec(memory_space=pltpu.VMEM),
        name="sample_bias_tables",
    )(rel_bias)


def _moba_select_kernel(rb_ref, k_ref, q_ref, o_ref, *, n_blk):
    h = pl.program_id(1)
    seq = q_ref.shape[3]
    kmean = jnp.sum(k_ref[0, 0, 0].reshape(n_blk, MOBA_BLOCK, HEAD_DIM), axis=1) * (1.0 / MOBA_BLOCK)
    gate = lax.dot_general(kmean, q_ref[0, 0, 0], NT_DIMS, precision=HIGHEST, preferred_element_type=F32)
    own = lax.broadcasted_iota(jnp.int32, (1, seq), 1) // MOBA_BLOCK
    far = rb_ref[N_BUCKETS - 1, h] * LOG2E
    o_ref[0, 0] = jnp.where(_top_blocks(gate, own, 0), far, NEG)


def moba_select(k, q, rel_bias, n_batch, seq):
    n_blk = seq // MOBA_BLOCK
    spec = lambda a: pl.BlockSpec((1, 1, 1, seq, HEAD_DIM), lambda b, h: (a[1], b, h, 0, 0))
    return pl.pallas_call(
        functools.partial(_moba_select_kernel, n_blk=n_blk),
        out_shape=jax.ShapeDtypeStruct((n_batch, N_HEADS, n_blk, seq), F32),
        grid=(n_batch, N_HEADS),
        in_specs=[pl.BlockSpec(memory_space=pltpu.SMEM), spec(k), spec(q)],
        out_specs=pl.BlockSpec((1, 1, n_blk, seq), lambda b, h: (b, h, 0, 0)),
        compiler_params=_cparams(("parallel", "parallel")),
        name="moba_select",
    )(rel_bias, k[0], q[0])


def _log_sigmoid(x):
    return -(jnp.maximum(-x, 0.0) + jnp.log1p(jnp.exp(-jnp.abs(x))))


def _split3(x):
    hi = x.astype(BF16).astype(F32)
    r1 = x - hi
    mid = r1.astype(BF16).astype(F32)
    lo = (r1 - mid).astype(BF16).astype(F32)
    return hi, mid, lo


def _fox_prep_kernel(idx, f_ref, bf_ref, lf_ref, kx_ref, qx_ref, carry_ref, *, blk):
    i = idx[1]

    @pl.when(i == 0)
    def _():
        carry_ref[...] = jnp.zeros_like(carry_ref)

    lane = lax.broadcasted_iota(jnp.int32, (blk, LANE), 1)
    lf = jnp.where(lane < N_HEADS, _log_sigmoid(f_ref[0] + bf_ref[...]), 0.0)
    lf_ref[...] = lf
    row = lax.broadcasted_iota(jnp.int32, (blk, blk), 0)
    col = lax.broadcasted_iota(jnp.int32, (blk, blk), 1)
    tril = (row >= col).astype(BF16)
    cum_all = carry_ref[0:1, :]
    for piece in _split3(lf):
        cum_all = cum_all + jnp.dot(tril, piece.astype(BF16), preferred_element_type=F32)
    carry_ref[0:1, :] = cum_all[blk - 1:blk, :]
    for h in range(N_HEADS):
        cum = jnp.broadcast_to(cum_all[:, h:h + 1], (blk, LANE))
        hi, mid, lo = _split3(cum * LOG2E)
        kx_ref[0, h] = jnp.where(lane == 0, -hi, jnp.where(lane == 1, -mid, jnp.where(
            lane == 2, -lo, jnp.where(lane < 6, 1.0, 0.0)))).astype(BF16)
        qx_ref[0, h] = jnp.where(lane == 3, hi, jnp.where(lane == 4, mid, jnp.where(
            lane == 5, lo, jnp.where(lane < 3, 1.0, 0.0)))).astype(BF16)


def fox_prep(zc, bf_row, n_batch, seq, blk=256):
    n_i = seq // blk
    xshape = jax.ShapeDtypeStruct((n_batch, N_HEADS, seq, LANE), BF16)
    xspec = pl.BlockSpec((1, N_HEADS, blk, LANE), lambda b, i: (b, 0, i, 0))
    return _Call("fox_prep", functools.partial(_fox_prep_kernel, blk=blk), (n_batch, n_i),
                 [pl.BlockSpec((1, blk, LANE), lambda b, i: (F_SEG, b * n_i + i, F_LANEBLK)),
                  pl.BlockSpec((1, LANE), lambda b, i: (0, 0))],
                 [pl.BlockSpec((blk, LANE), lambda b, i: (b * n_i + i, 0)), xspec, xspec],
                 [jax.ShapeDtypeStruct((n_batch * seq, LANE), F32), xshape, xshape], [zc, bf_row], {}, None,
                 (pltpu.VMEM((SUBLANE, LANE), F32),))


HEADS_PER_STEP = 6
HEAD_GROUP = 3


def _attn_tiles(qs, k_ref, kx_ref, v_ref, b0, nb, bias_fn, state):
    m_ref, l_ref, acc_ref = state
    start = pl.multiple_of(b0 * MOBA_BLOCK, MOBA_BLOCK)
    tk = nb * MOBA_BLOCK
    hp = len(qs)
    scores = {}

    def logits(heads):
        for i in heads:
            kt = k_ref[0, 0, i, pl.ds(start, tk), :]
            if kx_ref is not None:
                kt = jnp.concatenate([kt, kx_ref[0, i, pl.ds(start, tk), :]], axis=1)
            scores[i] = bias_fn(lax.dot_general(kt, qs[i], NT_DIMS, preferred_element_type=F32), i, b0)

    def update(heads):
        probs = {}
        for i in heads:
            m = m_ref[i]
            m_new = jnp.maximum(m, jnp.max(scores[i], axis=0, keepdims=True))
            alpha = jnp.exp2(m - m_new)
            p = jnp.exp2(scores.pop(i) - m_new)
            m_ref[i] = m_new
            l_ref[i] = alpha * l_ref[i] + jnp.sum(p, axis=0, keepdims=True)
            probs[i] = (p.astype(BF16), alpha)
        for i in heads:
            p, alpha = probs[i]
            vt = v_ref[0, 0, i, pl.ds(start, tk), :]
            pv = lax.dot_general(vt, p, TN_DIMS, preferred_element_type=F32)
            acc_ref[i] = alpha * acc_ref[i] + pv

    groups = [range(i, min(i + HEAD_GROUP, hp)) for i in range(0, hp, HEAD_GROUP)]
    logits(groups[0])
    for gi, heads in enumerate(groups):
        if gi + 1 < len(groups):
            logits(groups[gi + 1])
        update(heads)


def _attn_state_shapes(tq, hp):
    return [pltpu.VMEM((hp, 1, tq), F32), pltpu.VMEM((hp, 1, tq), F32), pltpu.VMEM((hp, HEAD_DIM, tq), F32)]


def _attn_init(state):
    m_ref, l_ref, acc_ref = state
    m_ref[...] = jnp.full(m_ref.shape, NEG, F32)
    l_ref[...] = jnp.zeros(l_ref.shape, F32)
    acc_ref[...] = jnp.zeros(acc_ref.shape, F32)


def _attn_finish(o_ref, state):
    m_ref, l_ref, acc_ref = state
    for i in range(acc_ref.shape[0]):
        o_ref[0, i] = (acc_ref[i] / l_ref[i]).T.astype(o_ref.dtype)


PAST_BLOCKS = 4


def _attn_drive(j, tiles):
    jf = jnp.maximum(j - 1, 0)

    @pl.loop(0, jf // PAST_BLOCKS)
    def _(n):
        tiles(n * PAST_BLOCKS, PAST_BLOCKS, False)

    nb = PAST_BLOCKS // 2
    while nb >= 1:
        @pl.when(jf % (2 * nb) >= nb)
        def _(nb=nb):
            tiles((jf // (2 * nb)) * (2 * nb), nb, False)
        nb //= 2

    @pl.when(j > 0)
    def _():
        tiles(j - 1, 2, True)

    @pl.when(j == 0)
    def _():
        tiles(0, 1, True)


def _fox_prompt_kernel(idx, q_ref, qx_ref, k_ref, kx_ref, v_ref, o_ref, *state, tq, hp):
    j = idx[2]
    qs = [jnp.concatenate([q_ref[0, 0, i], qx_ref[0, i]], axis=1) for i in range(hp)]

    def tiles(b0, nb, last):
        def bias(s, i, b0):
            if not last:
                return s
            key = lax.broadcasted_iota(jnp.int32, s.shape, 0) - (nb - 1) * tq
            qry = lax.broadcasted_iota(jnp.int32, s.shape, 1)
            return jnp.where(key <= qry, s, NEG)
        _attn_tiles(qs, k_ref, kx_ref, v_ref, b0, nb, bias, state)

    _attn_init(state)
    _attn_drive(j, tiles)
    _attn_finish(o_ref, state)


def _head_spec(seg, hp, rows, row_map, **kw):
    return pl.BlockSpec((1, 1, hp, rows, HEAD_DIM), lambda b, g, j: (seg, b, g, row_map(j), 0), **kw)


_RESIDENT = dict(pipeline_mode=pl.Buffered(1))


def fox_prompt(q, k, v, kx, qx, n_batch, seq, tq=256, hp=HEADS_PER_STEP):
    nq = seq // tq
    return _Call(
        "fox_prompt", functools.partial(_fox_prompt_kernel, tq=tq, hp=hp), (n_batch, N_HEADS // hp, nq),
        [_head_spec(q[1], hp, tq, lambda j: j),
         pl.BlockSpec((1, hp, tq, LANE), lambda b, g, j: (b, g, j, 0)),
         _head_spec(k[1], hp, seq, lambda j: 0, **_RESIDENT),
         pl.BlockSpec((1, hp, seq, LANE), lambda b, g, j: (b, g, 0, 0), **_RESIDENT),
         _head_spec(v[1], hp, seq, lambda j: 0, **_RESIDENT)],
        [pl.BlockSpec((1, hp, tq, HEAD_DIM), lambda b, g, j: (b, g, j, 0))],
        [jax.ShapeDtypeStruct((n_batch, N_HEADS, seq, HEAD_DIM), BF16)],
        [q[0], qx, k[0], kx, v[0]], {}, None, tuple(_attn_state_shapes(tq, hp)))


def _top_blocks(gate, n_valid, axis):
    n_blk = gate.shape[axis]
    blk = lax.broadcasted_iota(jnp.int32, gate.shape, axis)
    blk_f = blk.astype(F32)
    g = jnp.where(blk < n_valid, gate, -jnp.inf)
    sel = jnp.zeros(gate.shape, jnp.bool_)
    for _ in range(MOBA_TOPK):
        mx = jnp.max(g, axis=axis, keepdims=True)
        cand = (g == mx) & (mx > -jnp.inf)
        first = jnp.min(jnp.where(cand, blk_f, float(n_blk)), axis=axis, keepdims=True)
        pick = blk_f == first
        sel = sel | pick
        g = jnp.where(pick, -jnp.inf, g)
    return sel


def _moba_prompt_kernel(idx, rb_ref, q_ref, k_ref, v_ref, mb_ref, tab_ref, o_ref, *state, tq, hp):
    g, j = idx[1], idx[2]
    qs = [q_ref[0, 0, i] for i in range(hp)]
    fars = [rb_ref[N_BUCKETS - 1, g * hp + i] * LOG2E for i in range(hp)]

    def tiles(b0, nb, last):
        def bias(s, i, b0):
            rows = lambda r: s[r * tq:(r + 1) * tq]
            if not last:
                return jnp.concatenate([rows(r) + mb_ref[0, i, pl.ds(b0 + r, 1), :] for r in range(nb)], axis=0)
            own = rows(nb - 1) + tab_ref[i, 0]
            if nb == 1:
                return own
            prev = rows(0) + tab_ref[i, 1] + (mb_ref[0, i, pl.ds(b0, 1), :] - fars[i])
            return jnp.concatenate([prev, own], axis=0)
        _attn_tiles(qs, k_ref, None, v_ref, b0, nb, bias, state)

    _attn_init(state)
    _attn_drive(j, tiles)
    _attn_finish(o_ref, state)


def moba_prompt(q, k, v, mb, tab, rel_bias, n_batch, seq, hp=HEADS_PER_STEP):
    tq = MOBA_BLOCK
    nq = seq // tq
    return _Call(
        "moba_prompt", functools.partial(_moba_prompt_kernel, tq=tq, hp=hp), (n_batch, N_HEADS // hp, nq),
        [pl.BlockSpec(memory_space=pltpu.SMEM),
         _head_spec(q[1], hp, tq, lambda j: j),
         _head_spec(k[1], hp, seq, lambda j: 0, **_RESIDENT),
         _head_spec(v[1], hp, seq, lambda j: 0, **_RESIDENT),
         pl.BlockSpec((1, hp, nq, tq), lambda b, g, j: (b, g, 0, j)),
         pl.BlockSpec((hp, 2, tq, tq), lambda b, g, j: (g, 0, 0, 0), **_RESIDENT)],
        [pl.BlockSpec((1, hp, tq, HEAD_DIM), lambda b, g, j: (b, g, j, 0))],
        [jax.ShapeDtypeStruct((n_batch, N_HEADS, seq, HEAD_DIM), BF16)],
        [rel_bias, q[0], k[0], v[0], mb, tab], {}, None, tuple(_attn_state_shapes(tq, hp)))


def _pad_rows(x, rows):
    return jnp.concatenate([x, jnp.zeros((rows - x.shape[0], x.shape[1]), x.dtype)], axis=0)


def _lanes(a, b):
    return jnp.concatenate([a, b], axis=1)


def _rows(a, b):
    return jnp.concatenate([a, b], axis=0)


def _pair_queries(q_ref, g, scale):
    q0, q1 = q_ref[0, 0, 2 * g], q_ref[0, 0, 2 * g + 1]
    z = jnp.zeros_like(q0)
    return _rows(_lanes(q0, z), _lanes(z, q1)) * scale


def _pair_tile(pages, p0, g):
    t = lambda p: _lanes(pages[p][0, 0, 2 * g].astype(BF16), pages[p][0, 0, 2 * g + 1].astype(BF16))
    return _rows(t(p0), t(p0 + 1))


def _pair_new(x_ref, g, page):
    return _lanes(_pad_rows(x_ref[0, 0, 2 * g], page), _pad_rows(x_ref[0, 0, 2 * g + 1], page)).astype(BF16)


def _pair_bias(top, bottom, t_new):
    return _rows(jnp.broadcast_to(top, (t_new, top.shape[1])), jnp.broadcast_to(bottom, (t_new, bottom.shape[1])))


def _pair_probs(s_list, s_new):
    m = s_list[0]
    for s in s_list[1:]:
        m = jnp.maximum(m, s)
    m = jnp.max(jnp.maximum(jnp.maximum(m[:, :LANE], m[:, LANE:]), s_new), axis=1, keepdims=True)
    l = None
    p_list = []
    for s in s_list:
        p = jnp.exp(s - m)
        l = p if l is None else l + p
        p_list.append(p.astype(BF16))
    p = jnp.exp(s_new - m)
    l = jnp.sum(l[:, :LANE] + l[:, LANE:] + p, axis=1, keepdims=True)
    return p_list, p.astype(BF16), l


def _pair_pv(o_ref, g, probs, v_pages, v_ref, t_new, page):
    p_list, p_new, l = probs
    acc = jnp.dot(p_new, _pair_new(v_ref, g, page), preferred_element_type=F32)
    for n, p in enumerate(p_list):
        acc = acc + jnp.dot(p, _pair_tile(v_pages, 2 * n, g), preferred_element_type=F32)
    out = acc / l
    o_ref[0, 2 * g] = out[:t_new, :HEAD_DIM]
    o_ref[0, 2 * g + 1] = out[t_new:, HEAD_DIM:]


def _fox_sample_kernel(pt_ref, q_ref, k_ref, v_ref, f_ref, bf_ref, *rest, t_new, n_pages, page):
    k_pages = rest[:n_pages]
    v_pages = rest[n_pages:2 * n_pages]
    lf_pages = rest[2 * n_pages:3 * n_pages]
    o_ref, lf_ref = rest[3 * n_pages:]

    n_r = n_pages * SUBLANE
    lane = lax.broadcasted_iota(jnp.int32, (n_r, page), 1)
    srow = lax.broadcasted_iota(jnp.int32, (n_r, page), 0)
    lfc = jnp.concatenate([r[0, 0] for r in lf_pages], axis=0)
    incl = lfc
    step = 1
    while step < page:
        incl = incl + jnp.where(lane + step < page, pltpu.roll(incl, page - step, axis=1), 0.0)
        step *= 2
    tot = jnp.broadcast_to(jnp.sum(jnp.where(lane == 0, incl, 0.0), axis=1, keepdims=True), (n_r, page))
    run = tot
    step = SUBLANE
    while step < n_r:
        run = run + jnp.where(srow + step < n_r, pltpu.roll(run, n_r - step, axis=0), 0.0)
        step *= 2
    suf = (incl - lfc) + (run - tot)

    flane = lax.broadcasted_iota(jnp.int32, (t_new, LANE), 1)
    lf_new = jnp.where(flane < N_HEADS, _log_sigmoid(f_ref[0] + bf_ref[...]), 0.0)
    lf_ref[...] = lf_new
    cum = _pad_rows(lf_new, LANE).T
    lane2 = lax.broadcasted_iota(jnp.int32, (LANE, LANE), 1)
    step = 1
    while step < t_new:
        cum = cum + jnp.where(lane2 >= step, pltpu.roll(cum, step, axis=1), 0.0)
        step *= 2
    r = lax.broadcasted_iota(jnp.int32, (2 * t_new, page), 0) % t_new
    c = lax.broadcasted_iota(jnp.int32, (2 * t_new, page), 1)
    suf_row = lambda p, h: suf[p * SUBLANE + h:p * SUBLANE + h + 1, :]

    pairs = range(N_HEADS // 2)
    logits = []
    for g in pairs:
        h0, h1 = 2 * g, 2 * g + 1
        qp = _pair_queries(q_ref, g, SCALE).astype(BF16)
        s_list = []
        for p0 in range(0, n_pages, 2):
            s = lax.dot_general(qp, _pair_tile(k_pages, p0, g), NT_DIMS, preferred_element_type=F32)
            bias = _pair_bias(_lanes(suf_row(p0, h0), suf_row(p0 + 1, h0)),
                              _lanes(suf_row(p0, h1), suf_row(p0 + 1, h1)), t_new)
            s_list.append(s + bias)
        s_new = lax.dot_general(qp, _pair_new(k_ref, g, page), NT_DIMS, preferred_element_type=F32)
        s_new = jnp.where(c <= r, s_new - _pair_bias(cum[h0:h0 + 1, :], cum[h1:h1 + 1, :], t_new), NEG)
        logits.append((s_list, s_new))
    probs = [_pair_probs(*logits[g]) for g in pairs]
    for g in pairs:
        _pair_pv(o_ref, g, probs[g], v_pages, v_ref, t_new, page)


def _page_specs(n_pages, layer, shape, b0):
    def spec(p):
        return pl.BlockSpec((1, 1) + shape, lambda b, pt: (layer, pt[b0 + b, p]) + (0,) * len(shape))
    return [spec(p) for p in range(n_pages)]


def _new_spec(seg, t_new, b0):
    return pl.BlockSpec((1, 1, N_HEADS, t_new, HEAD_DIM), lambda b, pt: (seg, b0 + b, 0, 0, 0))


def fox_sample(q, k, v, zc, bf_row, pool_k, pool_v, pool_lft, page_table, layer, t_new, b0, n):
    n_pages = page_table.shape[1]
    page = pool_k.shape[3]
    in_specs = ([_new_spec(q[1], t_new, b0), _new_spec(k[1], t_new, b0), _new_spec(v[1], t_new, b0),
                 pl.BlockSpec((1, t_new, LANE), lambda b, pt: (F_SEG, b0 + b, F_LANEBLK)),
                 pl.BlockSpec((1, LANE), lambda b, pt: (0, 0))]
                + _page_specs(n_pages, layer, (N_HEADS, page, HEAD_DIM), b0)
                + _page_specs(n_pages, layer, (N_HEADS, page, HEAD_DIM), b0)
                + _page_specs(n_pages, layer, (SUBLANE, page), b0))
    return _Call(
        "fox_sample", functools.partial(_fox_sample_kernel, t_new=t_new, n_pages=n_pages, page=page), (n,), in_specs,
        [pl.BlockSpec((1, N_HEADS, t_new, HEAD_DIM), lambda b, pt: (b, 0, 0, 0)),
         pl.BlockSpec((t_new, LANE), lambda b, pt: (b, 0))],
        [jax.ShapeDtypeStruct((n, N_HEADS, t_new, HEAD_DIM), F32), jax.ShapeDtypeStruct((n * t_new, LANE), F32)],
        [q[0], k[0], v[0], zc, bf_row] + [pool_k] * n_pages + [pool_v] * n_pages + [pool_lft] * n_pages, {},
        page_table)


def _moba_sample_kernel(pt_ref, q_ref, k_ref, v_ref, tab_ref, *rest, t_new, n_pages, page):
    k_pages = rest[:n_pages]
    v_pages = rest[n_pages:2 * n_pages]
    o_ref = rest[2 * n_pages]
    assert MOBA_BLOCK == 2 * page
    n_blk = n_pages // 2

    def block_means(h):
        sums = [jnp.sum(k_pages[2 * n][0, 0, h] + k_pages[2 * n + 1][0, 0, h], axis=0, keepdims=True)
                for n in range(n_blk)]
        return _pad_rows(jnp.concatenate(sums, axis=0) * (1.0 / MOBA_BLOCK), LANE)

    pairs = range(N_HEADS // 2)
    logits = []
    for g in pairs:
        q32 = _pair_queries(q_ref, g, 1.0)
        qp = (q32 * SCALE).astype(BF16)
        kmean = _lanes(block_means(2 * g), block_means(2 * g + 1))
        gate = lax.dot_general(q32, kmean, NT_DIMS, precision=HIGHEST, preferred_element_type=F32)
        sel = _top_blocks(gate, n_blk, 1)
        mask = jnp.where(sel, 0.0, NEG)
        rows = slice(2 * g * t_new, (2 * g + 2) * t_new)
        far = tab_ref[2, rows, :]
        s_list = []
        for n in range(n_blk):
            s = lax.dot_general(qp, _pair_tile(k_pages, 2 * n, g), NT_DIMS, preferred_element_type=F32)
            bias = _lanes(far, tab_ref[0, rows, :] if n == n_blk - 1 else far)
            s_list.append(s + bias + mask[:, n:n + 1])
        s_new = lax.dot_general(qp, _pair_new(k_ref, g, page), NT_DIMS, preferred_element_type=F32)
        logits.append((s_list, s_new + tab_ref[1, rows, :]))
    probs = [_pair_probs(*logits[g]) for g in pairs]
    for g in pairs:
        _pair_pv(o_ref, g, probs[g], v_pages, v_ref, t_new, page)


def moba_sample(q, k, v, tab, pool_k, pool_v, page_table, layer, t_new, b0, n):
    n_pages = page_table.shape[1]
    page = pool_k.shape[3]
    rows = N_HEADS * t_new
    in_specs = ([_new_spec(q[1], t_new, b0), _new_spec(k[1], t_new, b0), _new_spec(v[1], t_new, b0),
                 pl.BlockSpec((3, rows, LANE), lambda b, pt: (0, 0, 0))]
                + _page_specs(n_pages, layer, (N_HEADS, page, HEAD_DIM), b0)
                + _page_specs(n_pages, layer, (N_HEADS, page, HEAD_DIM), b0))
    return _Call(
        "moba_sample", functools.partial(_moba_sample_kernel, t_new=t_new, n_pages=n_pages, page=page), (n,),
        in_specs, [pl.BlockSpec((1, N_HEADS, t_new, HEAD_DIM), lambda b, pt: (b, 0, 0, 0))],
        [jax.ShapeDtypeStruct((n, N_HEADS, t_new, HEAD_DIM), F32)],
        [q[0], k[0], v[0], tab] + [pool_k] * n_pages + [pool_v] * n_pages, {}, page_table)


def _silu(g):
    return g / (1.0 + jnp.exp(-g))


def _token_major(v, tm):
    return jnp.concatenate([v[:, h].reshape(tm, HEAD_DIM) for h in range(N_HEADS)], axis=1)


def _combine_kernel(*refs, tm, seq_tiles, seq_rows, alpha):
    (x_ref, oa_ref, of_ref, ga_ref, gf_ref, b0, b1, c0, c1, h0, h1, g0, g1) = refs[:13]
    if seq_rows is None:
        pc0, pc1, ph0, ph1 = refs[13:17]
        rest = refs[17:]
    else:
        hist1_ref, hist2_ref = refs[13:15]
        rest = refs[15:]
    wout_ref, cw_ref, lng_ref, lnb_ref, y_ref, yb_ref, u_ref = rest
    cat = lambda a, b: jnp.concatenate([a[0], b[0]], axis=1)
    u = cat(c0, c1) * cat(h0, h1)
    if seq_rows is None:
        keep = (pl.program_id(0) % seq_tiles != 0).astype(F32)
        uprev = cat(pc0, pc1) * cat(ph0, ph1) * keep
        ext = jnp.concatenate([uprev, u], axis=0)
        u1 = pltpu.roll(ext, 1, axis=0)[SUBLANE:]
        u2 = pltpu.roll(ext, 2, axis=0)[SUBLANE:]
        u_ref[0] = u[tm - SUBLANE:, :]
    else:
        r = lax.broadcasted_iota(jnp.int32, (tm, D_CONV), 0) % seq_rows
        u1 = jnp.where(r >= 1, pltpu.roll(u, 1, axis=0), hist1_ref[...])
        u2 = jnp.where(r >= 2, pltpu.roll(u, 2, axis=0), hist2_ref[...])
        u_ref[...] = u
    conv = cw_ref[0:1, :] * u2 + cw_ref[1:2, :] * u1 + cw_ref[2:3, :] * u
    oc = (cat(b0, b1) * conv * _silu(cat(g0, g1))).astype(BF16)
    oa = (_token_major(oa_ref[...].astype(F32), tm) * _silu(_token_major(ga_ref[0], tm))).astype(BF16)
    of = (_token_major(of_ref[...].astype(F32), tm) * _silu(_token_major(gf_ref[0], tm))).astype(BF16)
    proj = (jnp.dot(oa, wout_ref[0:D_ATT, :], preferred_element_type=F32)
            + jnp.dot(of, wout_ref[D_ATT:2 * D_ATT, :], preferred_element_type=F32)
            + jnp.dot(oc, wout_ref[2 * D_ATT:, :], preferred_element_type=F32))
    res = alpha * x_ref[...] + proj
    mu = jnp.mean(res, axis=-1, keepdims=True)
    cen = res - mu
    var = jnp.mean(cen * cen, axis=-1, keepdims=True)
    y = cen * lax.rsqrt(var + LN_EPS) * lng_ref[...] + lnb_ref[...]
    y_ref[...] = y
    yb_ref[...] = y.astype(BF16)


def combine(x, oa, of, zg, zc, wout, cw, lng, lnb, alpha, tm, hist=None, gates=(0, 1)):
    t = x.shape[0]
    n_seq, _, seq_len, _ = oa.shape
    assert t == n_seq * seq_len and t % tm == 0
    if seq_len >= tm:
        assert seq_len % tm == 0
        sb, sl, per_seq = 1, tm, seq_len // tm
        hmap = lambda i: (i // per_seq, 0, i % per_seq, 0)
    else:
        assert tm % seq_len == 0
        sb, sl, per_seq = tm // seq_len, seq_len, 1
        hmap = lambda i: (i, 0, 0, 0)
    hblk = (sb, N_HEADS, sl, HEAD_DIM)
    zblk = lambda sc: pl.BlockSpec((1, tm, CBLK), lambda i: (sc[0], i, sc[1]))
    pieces = [sc for idx in (0, 1, 2, 3) for sc in _conv_piece(idx)]
    in_specs = [pl.BlockSpec((tm, D_MODEL), lambda i: (i, 0)),
                pl.BlockSpec(hblk, hmap),
                pl.BlockSpec(hblk, hmap),
                pl.BlockSpec((1,) + hblk, lambda i: (gates[0],) + hmap(i)),
                pl.BlockSpec((1,) + hblk, lambda i: (gates[1],) + hmap(i))] + [zblk(sc) for sc in pieces]
    args = [x, oa, of, zg, zg] + [zc] * 8
    if hist is None:
        seq_rows = None
        tb = tm // SUBLANE
        prev = lambda sc: pl.BlockSpec((1, SUBLANE, CBLK), lambda i: (sc[0], jnp.maximum(i * tb - 1, 0), sc[1]))
        in_specs += [prev(sc) for sc in pieces[2:6]]
        args += [zc] * 4
        u_shape = jax.ShapeDtypeStruct((n_seq, SUBLANE, D_CONV), F32)
        u_spec = pl.BlockSpec((1, SUBLANE, D_CONV), lambda i: (i // per_seq, 0, 0))
    else:
        seq_rows = seq_len
        in_specs += [pl.BlockSpec((tm, D_CONV), lambda i: (i, 0))] * 2
        args += list(hist)
        u_shape = jax.ShapeDtypeStruct((t, D_CONV), F32)
        u_spec = pl.BlockSpec((tm, D_CONV), lambda i: (i, 0))
    in_specs += [pl.BlockSpec((D_MODEL, D_MODEL), lambda i: (0, 0)),
                 pl.BlockSpec((SUBLANE, D_CONV), lambda i: (0, 0)),
                 pl.BlockSpec((1, D_MODEL), lambda i: (0, 0)),
                 pl.BlockSpec((1, D_MODEL), lambda i: (0, 0))]
    args += [wout, cw, lng, lnb]
    return pl.pallas_call(
        functools.partial(_combine_kernel, tm=tm, seq_tiles=per_seq, seq_rows=seq_rows, alpha=alpha),
        out_shape=(jax.ShapeDtypeStruct((t, D_MODEL), F32), jax.ShapeDtypeStruct((t, D_MODEL), BF16), u_shape),
        grid=(t // tm,),
        in_specs=in_specs,
        out_specs=(pl.BlockSpec((tm, D_MODEL), lambda i: (i, 0)), pl.BlockSpec((tm, D_MODEL), lambda i: (i, 0)),
                   u_spec),
        compiler_params=_cparams(("arbitrary",)),
        name="combine_prompt" if hist is None else "combine_sample",
    )(*args)


def kernel(x_prompt, x_sample, cache_moba_k, cache_moba_v, cache_fox_k, cache_fox_v, cache_fox_logf,
           state_conv, page_table, rel_bias, w_in, b_f, conv_w, w_out, ln_g, ln_b):
    depth = w_in.shape[0]
    alpha = (2 * depth) ** 0.25
    n_batch, seq, _ = x_prompt.shape
    n_seq, t_new, _ = x_sample.shape
    n_phys, page = cache_moba_k.shape[1:3]
    past = page_table.shape[1] * page
    assert page == LANE and past % MOBA_BLOCK == 0 and t_new == SUBLANE and seq % MOBA_BLOCK == 0
    assert past >= MAX_DISTANCE and w_in.shape[2] == N_MAIN + N_HEADS + 4 * D_CONV

    xp = x_prompt.reshape(n_batch * seq, D_MODEL)
    xs = x_sample.reshape(n_seq * t_new, D_MODEL)
    xpb, xsb = xp.astype(BF16), xs.astype(BF16)
    pool = lambda c: jnp.swapaxes(c, 2, 3)
    pk_a, pv_a, pk_f, pv_f = pool(cache_moba_k), pool(cache_moba_v), pool(cache_fox_k), pool(cache_fox_v)
    pool_lft = jnp.pad(jnp.swapaxes(cache_fox_logf, 2, 3), ((0, 0), (0, 0), (0, SUBLANE - N_HEADS), (0, 0)))
    tab_p = prompt_bias_tables(rel_bias)
    tab_s = sample_bias_tables(rel_bias, t_new, past)
    tm_s = min(256, n_seq * t_new)

    outs = {k: [] for k in ("p_lf", "p_cv", "s_lf", "s_cv")}
    kv_segs = {"ka": SEG_KA, "va": SEG_VA, "kf": SEG_KF, "vf": SEG_VF}
    p_kv = dict.fromkeys(kv_segs)
    s_kv = {name: [] for name in kv_segs}
    for layer in range(depth):
        wt_h = jnp.swapaxes(w_in[layer], 0, 1).astype(BF16)
        n_f = N_MAIN + N_HEADS
        wt_c = jnp.concatenate(
            [wt_h[n_f:], wt_h[N_MAIN:n_f],
             jnp.zeros((N_CONV_SEG * SEG - 4 * D_CONV - N_HEADS, D_MODEL), BF16)], axis=0)
        wo = w_out[layer].astype(BF16)
        bf_row = jnp.pad(b_f[layer], (0, LANE - N_HEADS)).reshape(1, LANE)
        cw = jnp.pad(conv_w[layer], ((0, SUBLANE - CONV_WIDTH), (0, 0)))
        lng, lnb = ln_g[layer].reshape(1, D_MODEL), ln_b[layer].reshape(1, D_MODEL)

        (zhs,) = run(inproj_heads(xsb, wt_h, tuple(range(N_HEAD_SEG)), n_seq=n_seq, seq_len=t_new, tm=n_seq * t_new))
        (zcs,) = run(inproj_conv(xsb, wt_c, tm=n_seq * t_new))
        for name, seg in kv_segs.items():
            s_kv[name].append(zhs[seg])

        sample = {
            "fox": functools.partial(fox_sample, (zhs, SEG_QF), (zhs, SEG_KF), (zhs, SEG_VF), zcs, bf_row,
                                     pk_f, pv_f, pool_lft, page_table, layer, t_new),
            "moba": functools.partial(moba_sample, (zhs, SEG_QA), (zhs, SEG_KA), (zhs, SEG_VA), tab_s,
                                      pk_a, pv_a, page_table, layer, t_new)}
        next_seq = {"fox": 0, "moba": 0}
        chunks = {"fox": [], "moba": []}

        def run_with_sample(call, reserve=0):
            steps = math.prod(call.grid)
            for kind in sample:
                n_b = min(steps, n_seq - next_seq[kind] - reserve)
                if n_b == steps or (reserve and n_b > 0):
                    outs_a, outs_b = run_zipped(call, sample[kind](next_seq[kind], n_b))
                    next_seq[kind] += n_b
                    chunks[kind].append(outs_b)
                    return outs_a
            return run(call)

        proj = functools.partial(inproj_heads, xpb, wt_h, n_seq=n_batch, seq_len=seq, tm=512)
        zq, zqb = run_with_sample(proj((SEG_QA, SEG_QF), bf16_scale=QSCALE))
        (zg,) = run_with_sample(proj((SEG_GA, SEG_GF)))
        kvb = {}
        for name, seg in kv_segs.items():
            p_kv[name], kvb[name] = run_with_sample(proj((seg,), bf16_scale=1.0, stack=(depth, layer, p_kv[name])))
        (zc,) = run_with_sample(inproj_conv(xpb, wt_c, tm=512))

        mb = moba_select((p_kv["ka"], layer), (zq, 0), rel_bias, n_batch, seq)
        (oa,) = run_with_sample(
            moba_prompt((zqb, 0), (kvb["ka"], 0), (kvb["va"], 0), mb, tab_p, rel_bias, n_batch, seq))
        fox_steps = n_batch * (N_HEADS // HEADS_PER_STEP) * (seq // MOBA_BLOCK)
        lf, kx, qx = run_with_sample(fox_prep(zc, bf_row, n_batch, seq), reserve=fox_steps)
        (of,) = run_with_sample(fox_prompt((zqb, 1), (kvb["kf"], 0), (kvb["vf"], 0), kx, qx, n_batch, seq))
        for kind in sample:
            if next_seq[kind] < n_seq:
                chunks[kind].append(run(sample[kind](next_seq[kind], n_seq - next_seq[kind])))
        ofs = jnp.concatenate([c[0] for c in chunks["fox"]], axis=0)
        lfs = jnp.concatenate([c[1] for c in chunks["fox"]], axis=0)
        oas = jnp.concatenate([c[0] for c in chunks["moba"]], axis=0)
        xp, xpb, ulast = combine(xp, oa, of, zg, zc, wo, cw, lng, lnb, alpha, tm=256)
        outs["p_lf"].append(lf[:, :N_HEADS].reshape(n_batch, seq, N_HEADS))
        outs["p_cv"].append(ulast[:, SUBLANE - (CONV_WIDTH - 1):, :])

        st = state_conv[layer]
        zero = jnp.zeros((n_seq, t_new, D_CONV), F32)
        hist1 = zero.at[:, 0].set(st[:, 1]).reshape(n_seq * t_new, D_CONV)
        hist2 = zero.at[:, 0].set(st[:, 0]).at[:, 1].set(st[:, 1]).reshape(n_seq * t_new, D_CONV)
        xs, xsb, us = combine(xs, oas, ofs, zhs, zcs, wo, cw, lng, lnb, alpha, tm=tm_s, hist=(hist1, hist2),
                              gates=(SEG_GA, SEG_GF))
        outs["s_lf"].append(lfs[:, :N_HEADS].reshape(n_seq, t_new, N_HEADS))
        outs["s_cv"].append(us.reshape(n_seq, t_new, D_CONV)[:, t_new - (CONV_WIDTH - 1):, :])

    st = {k: jnp.stack(v) for k, v in outs.items()}
    tok_major = lambda a: jnp.swapaxes(a, 2, 3)
    return (xp.reshape(n_batch, seq, D_MODEL), xs.reshape(n_seq, t_new, D_MODEL),
            tok_major(p_kv["ka"]), tok_major(p_kv["va"]), tok_major(p_kv["kf"]), tok_major(p_kv["vf"]),
            st["p_lf"], st["p_cv"],
            *(tok_major(jnp.stack(s_kv[name])) for name in ("ka", "va", "kf", "vf")),
            st["s_lf"], st["s_cv"])
```
